```python
import math
import functools
import jax
import jax.numpy as jnp
from jax import lax
import numpy as np

D_MODEL = 2048
BATCH = 8
SEQ = 2048
DEPTH = 1
DEC_BATCH = 128
DEC_SEQ = 8
PAST_LEN = 2048
PAGE_SIZE = 128

HEAD_DIM = 128
N_HEADS = D_MODEL // HEAD_DIM
N_ATT_HEADS = N_HEADS // 2
N_GDN_HEADS = N_HEADS - N_ATT_HEADS
W_ATT = N_ATT_HEADS * HEAD_DIM
W_GDN = N_GDN_HEADS * HEAD_DIM
D_IN = 3 * W_ATT + 4 * W_GDN + 2 * N_GDN_HEADS
MOBA_BLOCK = 256
MOBA_TOP_K = 3
Q_CHUNK = 16
GDN_CONV = 4
GDN_CHUNK = 64
D_FF = 256 * ((8 * D_MODEL // 3 + 255) // 256)
FFN_CONV = 3
NUM_BUCKETS = 32
MAX_DISTANCE = 128
EPS = 1e-6
NEG = -1e30

kernel_name = 'hymba_gdn_moba_convffn_step'


def _rmsnorm(x, w):
    x32 = x.astype(jnp.float32)
    y = x32 * lax.rsqrt(jnp.mean(x32 * x32, axis=-1, keepdims=True) + EPS)
    return (y * w.astype(jnp.float32)).astype(x.dtype)


def _l2norm(x):
    return x * lax.rsqrt(jnp.sum(x * x, axis=-1, keepdims=True) + 1e-6)


def _causal_dwconv(u, w, prev):
    t = u.shape[1]
    ext = jnp.concatenate([prev.astype(u.dtype), u], axis=1)
    y = ext[:, 0:t] * w[0]
    for j in range(1, w.shape[0]):
        y = y + ext[:, j:j + t] * w[j]
    return y, ext[:, t:]


def _rel_bucket(dist):
    n = jnp.maximum(dist, 0)
    max_exact = NUM_BUCKETS // 2
    nf = jnp.maximum(n, 1).astype(jnp.float32)
    large = max_exact + (jnp.log(nf / max_exact) / math.log(MAX_DISTANCE / max_exact)
                         * (NUM_BUCKETS - max_exact)).astype(jnp.int32)
    large = jnp.minimum(large, NUM_BUCKETS - 1)
    return jnp.where(n < max_exact, n, large)


def _moba_core(q, q_pos, k_own, v_own, own_pos, rel_bias, sel=None):
    scale = HEAD_DIM ** -0.5
    lo = jnp.einsum('bqhd,bkhd->bqhk', q, k_own).astype(jnp.float32) * scale
    b_own = rel_bias[_rel_bucket(q_pos[:, None] - own_pos[None, :])].astype(jnp.float32)
    lo = lo + jnp.transpose(b_own, (0, 2, 1))[None]
    causal = (own_pos[None, :] <= q_pos[:, None])[None, :, None, :]
    lo = jnp.where(causal, lo, NEG)
    if sel is None:
        p = jax.nn.softmax(lo, axis=-1).astype(v_own.dtype)
        return jnp.einsum('bqhk,bkhd->bqhd', p, v_own)
    k_sel, v_sel, sel_pos, sel_ok = sel
    ls = jnp.einsum('bqhd,bqhsd->bqhs', q, k_sel).astype(jnp.float32) * scale
    hidx = jnp.arange(q.shape[2])[:, None]
    ls = ls + rel_bias[_rel_bucket(q_pos[None, :, None, None] - sel_pos), hidx].astype(jnp.float32)
    ls = jnp.where(sel_ok, ls, NEG)
    n_s = ls.shape[-1]
    p = jax.nn.softmax(jnp.concatenate([ls, lo], axis=-1), axis=-1).astype(v_own.dtype)
    return (jnp.einsum('bqhs,bqhsd->bqhd', p[..., :n_s], v_sel)
            + jnp.einsum('bqhk,bkhd->bqhd', p[..., n_s:], v_own))


def _select_blocks(q, means, cur, n_sel):
    s = jnp.einsum('bqhd,bjhd->bqhj', q.astype(jnp.float32), means)
    valid = jnp.arange(means.shape[1])[None, None, None, :] < cur[None, :, None, None]
    s = jnp.where(valid, s, NEG)
    _, idx = lax.top_k(s, n_sel)
    ok = jnp.arange(n_sel)[None, None, None, :] < cur[None, :, None, None]
    return idx, jnp.broadcast_to(ok, idx.shape)


def _moba_prompt(q, k, v, rel_bias):
    b, s, h, d = q.shape
    nb = -(-s // MOBA_BLOCK)
    pad = nb * MOBA_BLOCK - s
    kb = jnp.pad(k, ((0, 0), (0, pad), (0, 0), (0, 0))).reshape(b, nb, MOBA_BLOCK, h, d)
    vb = jnp.pad(v, ((0, 0), (0, pad), (0, 0), (0, 0))).reshape(b, nb, MOBA_BLOCK, h, d)
    n_sel = min(MOBA_TOP_K, nb - 1)
    n_chunks = s // Q_CHUNK
    qc = q.reshape(b, n_chunks, Q_CHUNK, h, d).swapaxes(0, 1)
    blk_off = jnp.arange(MOBA_BLOCK)
    bidx = jnp.arange(b)[:, None, None, None]
    hidx = jnp.arange(h)[None, None, :, None]
    if n_sel > 0:
        means = kb[:, :nb - 1].astype(jnp.float32).sum(axis=2) / MOBA_BLOCK

    def one_chunk(args):
        c, q_c = args
        q_pos = c * Q_CHUNK + jnp.arange(Q_CHUNK)
        blk = (c * Q_CHUNK) // MOBA_BLOCK
        k_own = lax.dynamic_index_in_dim(kb, blk, axis=1, keepdims=False)
        v_own = lax.dynamic_index_in_dim(vb, blk, axis=1, keepdims=False)
        own_pos = blk * MOBA_BLOCK + blk_off
        sel = None
        if n_sel > 0:
            idx, ok = _select_blocks(q_c, means, q_pos // MOBA_BLOCK, n_sel)
            k_sel = kb[bidx, idx, :, hidx].reshape(b, Q_CHUNK, h, n_sel * MOBA_BLOCK, d)
            v_sel = vb[bidx, idx, :, hidx].reshape(b, Q_CHUNK, h, n_sel * MOBA_BLOCK, d)
            sel_pos = (idx[..., None] * MOBA_BLOCK + blk_off).reshape(b, Q_CHUNK, h, -1)
            sel_ok = jnp.broadcast_to(ok[..., None], idx.shape + (MOBA_BLOCK,)).reshape(b, Q_CHUNK, h, -1)
            sel = (k_sel, v_sel, sel_pos, sel_ok)
        return _moba_core(q_c, q_pos, k_own, v_own, own_pos, rel_bias, sel)

    out = lax.map(one_chunk, (jnp.arange(n_chunks), qc))
    return out.swapaxes(0, 1).reshape(b, s, h, d)


def _moba_sample(q, k_new, v_new, cache_k, cache_v, page_table, rel_bias):
    db, t, h, d = q.shape
    n_pages = page_table.shape[1]
    past = n_pages * PAGE_SIZE
    ppb = MOBA_BLOCK // PAGE_SIZE
    cur = past // MOBA_BLOCK
    q_pos = past + jnp.arange(t)
    first_own_page = cur * ppb
    n_own_past = n_pages - first_own_page
    if n_own_past > 0:
        pt = page_table[:, first_own_page:]
        k_own = jnp.concatenate([cache_k[pt].reshape(db, n_own_past * PAGE_SIZE, h, d), k_new], axis=1)
        v_own = jnp.concatenate([cache_v[pt].reshape(db, n_own_past * PAGE_SIZE, h, d), v_new], axis=1)
    else:
        k_own, v_own = k_new, v_new
    own_pos = cur * MOBA_BLOCK + jnp.arange(k_own.shape[1])
    n_sel = min(MOBA_TOP_K, cur)
    if n_sel == 0:
        return _moba_core(q, q_pos, k_own, v_own, own_pos, rel_bias)
    page_sums = lax.map(lambda j: cache_k[page_table[:, j]].astype(jnp.float32).sum(axis=1),
                        jnp.arange(cur * ppb))
    means = page_sums.swapaxes(0, 1).reshape(db, cur, ppb, h, d).sum(axis=2) / MOBA_BLOCK
    idx, ok = _select_blocks(q, means, q_pos // MOBA_BLOCK, n_sel)
    pidx = jnp.arange(db)[:, None, None, None, None]
    hidx = jnp.arange(h)[None, None, :, None, None]
    page_off = jnp.arange(ppb)
    row_off = jnp.arange(PAGE_SIZE)

    def one_tok(args):
        q_t, idx_t, ok_t, pos_t = args
        q_t, idx_t, ok_t = q_t[:, None], idx_t[:, None], ok_t[:, None]
        logical = idx_t[..., None] * ppb + page_off
        phys = page_table[pidx, logical]
        k_sel = cache_k[phys, :, hidx].reshape(db, 1, h, n_sel * MOBA_BLOCK, d)
        v_sel = cache_v[phys, :, hidx].reshape(db, 1, h, n_sel * MOBA_BLOCK, d)
        sel_pos = (logical[..., None] * PAGE_SIZE + row_off).reshape(db, 1, h, -1)
        sel_ok = jnp.broadcast_to(ok_t[..., None, None], logical.shape + (PAGE_SIZE,)).reshape(db, 1, h, -1)
        o = _moba_core(q_t, pos_t[None], k_own, v_own, own_pos, rel_bias, (k_sel, v_sel, sel_pos, sel_ok))
        return o[:, 0]

    out = lax.map(one_tok, (q.swapaxes(0, 1), idx.swapaxes(0, 1), ok.swapaxes(0, 1), q_pos))
    return out.swapaxes(0, 1)


def _gated_delta_chunked(q, k, v, g, beta, s0, chunk):
    b, t, h, _ = q.shape
    dv = v.shape[-1]
    n = t // chunk

    def blocks(x):
        x = x.reshape((b, n, chunk, h) + x.shape[3:])
        return jnp.moveaxis(x, (1, 3), (0, 2))

    incl = jnp.tri(chunk, dtype=bool)
    strict = jnp.tri(chunk, k=-1, dtype=bool)
    eye = jnp.eye(chunk, dtype=jnp.float32)

    def step(s, xs):
        qc, kc, vc, gc, bc = xs
        cg = jnp.cumsum(gc, axis=-1)
        decay = jnp.exp(jnp.where(incl, cg[..., :, None] - cg[..., None, :], -jnp.inf))
        kk = jnp.einsum('bhid,bhjd->bhij', kc, kc)
        lower = eye + jnp.where(strict, bc[..., :, None] * decay * kk, 0.0)
        eg = jnp.exp(cg)
        rhs = bc[..., None] * vc - (bc * eg)[..., None] * jnp.einsum('bhcd,bhde->bhce', kc, s)
        u = lax.linalg.triangular_solve(lower, rhs, left_side=True, lower=True, unit_diagonal=True)
        qk = jnp.einsum('bhid,bhjd->bhij', qc, kc) * decay
        o = eg[..., None] * jnp.einsum('bhcd,bhde->bhce', qc, s) + jnp.einsum('bhij,bhje->bhie', qk, u)
        g_last = cg[..., -1]
        k_dec = kc * jnp.exp(g_last[..., None] - cg)[..., None]
        s_new = jnp.exp(g_last)[..., None, None] * s + jnp.einsum('bhcd,bhce->bhde', k_dec, u)
        return s_new, o

    s, o = lax.scan(step, s0, (blocks(q), blocks(k), blocks(v), blocks(g), blocks(beta)))
    o = jnp.moveaxis(o, (0, 2), (1, 3)).reshape(b, t, h, dv)
    return o, s


def _gated_deltanet(qkv, z, a, b_logit, conv_w, conv_prev, a_log, dt_bias, norm_w, s0):
    bsz, t, _ = qkv.shape
    qkv_c, conv_new = _causal_dwconv(qkv, conv_w, conv_prev)
    qkv_c = jax.nn.silu(qkv_c.astype(jnp.float32))
    q = qkv_c[..., 0:W_GDN].reshape(bsz, t, N_GDN_HEADS, HEAD_DIM)
    k = qkv_c[..., W_GDN:2 * W_GDN].reshape(bsz, t, N_GDN_HEADS, HEAD_DIM)
    v = qkv_c[..., 2 * W_GDN:3 * W_GDN].reshape(bsz, t, N_GDN_HEADS, HEAD_DIM)
    q = _l2norm(q) * HEAD_DIM ** -0.5
    k = _l2norm(k)
    beta = jax.nn.sigmoid(b_logit.astype(jnp.float32))
    g = -jnp.exp(a_log.astype(jnp.float32)) * jax.nn.softplus(a.astype(jnp.float32) + dt_bias.astype(jnp.float32))
    chunk = GDN_CHUNK if t % GDN_CHUNK == 0 else t
    o, s = _gated_delta_chunked(q, k, v, g, beta, s0.astype(jnp.float32), chunk)
    zg = jax.nn.silu(z.astype(jnp.float32)).reshape(bsz, t, N_GDN_HEADS, HEAD_DIM)
    o = o * lax.rsqrt(jnp.mean(o * o, axis=-1, keepdims=True) + EPS) * norm_w.astype(jnp.float32) * zg
    return o.reshape(bsz, t, W_GDN).astype(qkv.dtype), s.astype(s0.dtype), conv_new


def _conv_ffn(h, w_up, conv_w, conv_b, w_down, prev):
    u = h @ w_up
    u, new_prev = _causal_dwconv(u, conv_w, prev)
    gate, val = jnp.split(u + conv_b, 2, axis=-1)
    return (jax.nn.silu(gate) * val) @ w_down, new_prev


def _layer(x, gdn_s0, gdn_conv_prev, ffn_conv_prev, attend, rms_mix_w, w_in, gdn_conv_w, gdn_a_log,
           gdn_dt_bias, gdn_norm_w, w_out, rms_ffn_w, w_up, ffn_conv_w, ffn_conv_b, w_down):
    b, t, _ = x.shape
    p = _rmsnorm(x, rms_mix_w) @ w_in
    q_att = p[..., 0:W_ATT].reshape(b, t, N_ATT_HEADS, HEAD_DIM)
    k_att = p[..., W_ATT:2 * W_ATT].reshape(b, t, N_ATT_HEADS, HEAD_DIM)
    v_att = p[..., 2 * W_ATT:3 * W_ATT].reshape(b, t, N_ATT_HEADS, HEAD_DIM)
    o = 3 * W_ATT
    qkv_gdn = p[..., o:o + 3 * W_GDN]
    o += 3 * W_GDN
    z = p[..., o:o + W_GDN]
    o += W_GDN
    a = p[..., o:o + N_GDN_HEADS]
    o += N_GDN_HEADS
    b_logit = p[..., o:o + N_GDN_HEADS]
    att = attend(q_att, k_att, v_att).reshape(b, t, W_ATT)
    gdn, s_new, gconv_new = _gated_deltanet(qkv_gdn, z, a, b_logit, gdn_conv_w, gdn_conv_prev,
                                            gdn_a_log, gdn_dt_bias, gdn_norm_w, gdn_s0)
    x = x + jnp.concatenate([att, gdn], axis=-1) @ w_out
    f, fconv_new = _conv_ffn(_rmsnorm(x, rms_ffn_w), w_up, ffn_conv_w, ffn_conv_b, w_down, ffn_conv_prev)
    return x + f, k_att, v_att, s_new, gconv_new, fconv_new


def setup_inputs(seed: int = 0) -> dict:
    key = jax.random.key(seed)
    ks = jax.random.split(key, 24)
    f32 = jnp.float32
    n_pages = PAST_LEN // PAGE_SIZE
    n_pool = (5 * DEC_BATCH * n_pages + 3) // 4

    def nrm(k, shape, s):
        return jax.random.normal(k, shape, f32) * s

    page_table = jax.random.permutation(ks[4], n_pool)[:DEC_BATCH * n_pages].reshape(DEC_BATCH, n_pages).astype(jnp.int32)
    dt = jnp.exp(jax.random.uniform(ks[12], (DEPTH, N_GDN_HEADS), f32, math.log(1e-3), math.log(1e-1)))
    return {
        'x_prompt': nrm(ks[0], (BATCH, SEQ, D_MODEL), 1.0),
        'x_sample': nrm(ks[1], (DEC_BATCH, DEC_SEQ, D_MODEL), 1.0),
        'cache_k': nrm(ks[2], (DEPTH, n_pool, PAGE_SIZE, N_ATT_HEADS, HEAD_DIM), 1.0),
        'cache_v': nrm(ks[3], (DEPTH, n_pool, PAGE_SIZE, N_ATT_HEADS, HEAD_DIM), 1.0),
        'page_table': page_table,
        'state_gdn': nrm(ks[5], (DEPTH, DEC_BATCH, N_GDN_HEADS, HEAD_DIM, HEAD_DIM), 0.1),
        'state_gdn_conv': nrm(ks[6], (DEPTH, DEC_BATCH, GDN_CONV - 1, 3 * W_GDN), 1.0),
        'state_ffn_conv': nrm(ks[7], (DEPTH, DEC_BATCH, FFN_CONV - 1, 2 * D_FF), 1.0),
        'rms_mix_w': 1.0 + nrm(ks[8], (DEPTH, D_MODEL), 0.02),
        'w_in': nrm(ks[9], (DEPTH, D_MODEL, D_IN), D_MODEL ** -0.5),
        'gdn_conv_w': nrm(ks[10], (DEPTH, GDN_CONV, 3 * W_GDN), GDN_CONV ** -0.5),
        'gdn_a_log': jnp.log(jax.random.uniform(ks[11], (DEPTH, N_GDN_HEADS), f32, 1.0, 16.0)),
        'gdn_dt_bias': dt + jnp.log(-jnp.expm1(-dt)),
        'gdn_norm_w': 1.0 + nrm(ks[13], (DEPTH, HEAD_DIM), 0.02),
        'w_out': nrm(ks[14], (DEPTH, D_MODEL, D_MODEL), D_MODEL ** -0.5),
        'rms_ffn_w': 1.0 + nrm(ks[15], (DEPTH, D_MODEL), 0.02),
        'w_up': nrm(ks[16], (DEPTH, D_MODEL, 2 * D_FF), D_MODEL ** -0.5),
        'ffn_conv_w': nrm(ks[17], (DEPTH, FFN_CONV, 2 * D_FF), FFN_CONV ** -0.5),
        'ffn_conv_b': nrm(ks[18], (DEPTH, 2 * D_FF), 0.02),
        'w_down': nrm(ks[19], (DEPTH, D_FF, D_MODEL), D_FF ** -0.5),
        'rel_bias': nrm(ks[20], (NUM_BUCKETS, N_ATT_HEADS), 0.5),
        'final_norm_w': 1.0 + nrm(ks[21], (D_MODEL,), 0.02),
    }


def reference(x_prompt, x_sample, cache_k, cache_v, page_table, state_gdn, state_gdn_conv, state_ffn_conv,
              rms_mix_w, w_in, gdn_conv_w, gdn_a_log, gdn_dt_bias, gdn_norm_w, w_out, rms_ffn_w, w_up,
              ffn_conv_w, ffn_conv_b, w_down, rel_bias, final_norm_w):
    b = x_prompt.shape[0]
    dt = x_prompt.dtype
    xp, xs = x_prompt, x_sample
    kp_l, vp_l, sp_l, gcp_l, fcp_l = [], [], [], [], []
    ks_l, vs_l, ss_l, gcs_l, fcs_l = [], [], [], [], []
    attend_p = functools.partial(_moba_prompt, rel_bias=rel_bias)
    for l in range(DEPTH):
        wl = (rms_mix_w[l], w_in[l], gdn_conv_w[l], gdn_a_log[l], gdn_dt_bias[l], gdn_norm_w[l], w_out[l],
              rms_ffn_w[l], w_up[l], ffn_conv_w[l], ffn_conv_b[l], w_down[l])
        xp, kp, vp, sp, gcp, fcp = _layer(
            xp, jnp.zeros((b, N_GDN_HEADS, HEAD_DIM, HEAD_DIM), dt),
            jnp.zeros((b, GDN_CONV - 1, 3 * W_GDN), dt), jnp.zeros((b, FFN_CONV - 1, 2 * D_FF), dt),
            attend_p, *wl)
        attend_s = functools.partial(_moba_sample, cache_k=cache_k[l], cache_v=cache_v[l],
                                     page_table=page_table, rel_bias=rel_bias)
        xs, ks_, vs_, ss, gcs, fcs = _layer(xs, state_gdn[l], state_gdn_conv[l], state_ffn_conv[l], attend_s, *wl)
        kp_l.append(kp); vp_l.append(vp); sp_l.append(sp); gcp_l.append(gcp); fcp_l.append(fcp)
        ks_l.append(ks_); vs_l.append(vs_); ss_l.append(ss); gcs_l.append(gcs); fcs_l.append(fcs)
    y_prompt = _rmsnorm(xp, final_norm_w)
    y_sample = _rmsnorm(xs, final_norm_w)
    return (y_prompt, y_sample,
            jnp.stack(kp_l), jnp.stack(vp_l), jnp.stack(sp_l), jnp.stack(gcp_l), jnp.stack(fcp_l),
            jnp.stack(ks_l), jnp.stack(vs_l), jnp.stack(ss_l), jnp.stack(gcs_l), jnp.stack(fcs_l))
```

```python
import functools
import math

import jax
import jax.numpy as jnp
import numpy as np
from jax import lax
from jax.experimental import pallas as pl
from jax.experimental.pallas import tpu as pltpu

F32 = jnp.float32
BF16 = jnp.bfloat16
HI = lax.Precision.HIGHEST

HEAD_DIM = 128
MOBA_BLOCK = 256
MOBA_TOP_K = 3
PAGE_SIZE = 128
GDN_CONV = 4
GDN_CHUNK = 64
FFN_CONV = 3
NUM_BUCKETS = 32
MAX_DISTANCE = 128
EPS = 1e-6
NEG = -1e30

LANES = 128
SUBLANES = 8
VMEM_LIMIT = 56 * 1024 * 1024


def _bucket_thresholds():
    n = np.arange(0, 4 * MAX_DISTANCE)
    max_exact = NUM_BUCKETS // 2
    nf = np.maximum(n, 1).astype(np.float32)
    large = max_exact + (np.log(nf / np.float32(max_exact)) / np.float32(math.log(MAX_DISTANCE / max_exact))
                         * np.float32(NUM_BUCKETS - max_exact)).astype(np.int32)
    large = np.minimum(large, NUM_BUCKETS - 1)
    b = np.where(n < max_exact, n, large)
    return [int(np.argmax(b >= k)) for k in range(1, NUM_BUCKETS)]


BUCKET_THR = _bucket_thresholds()


def _dot(a, b, precision=None):
    return jnp.dot(a, b, preferred_element_type=F32, precision=precision)


def _dot_nt(a, b, precision=None):
    return lax.dot_general(a, b, (((1,), (1,)), ((), ())), preferred_element_type=F32, precision=precision)


def _dot_tn(a, b, precision=None):
    return lax.dot_general(a, b, (((0,), (0,)), ((), ())), preferred_element_type=F32, precision=precision)


def _silu(x):
    return x * jax.nn.sigmoid(x)


def _softplus(x):
    return jnp.maximum(x, 0.0) + jnp.log1p(jnp.exp(-jnp.abs(x)))


def _params(*semantics):
    return pltpu.CompilerParams(dimension_semantics=semantics, vmem_limit_bytes=VMEM_LIMIT)


def _rmsnorm_rows(load, g, store, rows, chunk=256):
    def body(r, carry):
        sl = pl.ds(pl.multiple_of(r * chunk, chunk), chunk)
        x = load(sl)
        ms = jnp.mean(x * x, axis=-1, keepdims=True)
        store(sl, x * lax.rsqrt(ms + EPS) * g)
        return carry
    lax.fori_loop(0, rows // chunk, body, 0)


def _inproj_kernel(x_ref, g_ref, w_ref, wab_ref, q_ref, k_ref, v_ref, gq_ref, z_ref, ab_ref, h_ref, *, tm):
    j = pl.program_id(1)

    @pl.when(j == 0)
    def _():
        def store(sl, y):
            h_ref[sl, :] = y.astype(BF16)
        _rmsnorm_rows(lambda sl: x_ref[sl, :], g_ref[...], store, tm)
        ab_ref[...] = _dot(h_ref[...], wab_ref[...])

    res = _dot(h_ref[...], w_ref[...])
    for idx, ref in ((0, q_ref), (1, k_ref), (2, v_ref), (6, z_ref)):
        @pl.when(j == idx)
        def _(ref=ref):
            ref[...] = res

    @pl.when((j >= 3) & (j <= 5))
    def _():
        gq_ref[...] = res


def _inproj(x2d, g, w_main, w_ab, *, tm):
    t, d = x2d.shape
    tn = 1024
    n_tiles = w_main.shape[1] // tn
    assert n_tiles == 7 and t % tm == 0
    wide = lambda lo, hi: pl.BlockSpec((tm, tn), lambda i, j: (i, jnp.clip(j - lo, 0, hi - lo)))
    one = pl.BlockSpec((tm, tn), lambda i, j: (i, 0))
    return pl.pallas_call(
        functools.partial(_inproj_kernel, tm=tm),
        grid=(t // tm, n_tiles),
        in_specs=[pl.BlockSpec((tm, d), lambda i, j: (i, 0)),
                  pl.BlockSpec((1, d), lambda i, j: (0, 0)),
                  pl.BlockSpec((d, tn), lambda i, j: (0, j)),
                  pl.BlockSpec((d, LANES), lambda i, j: (0, 0))],
        out_specs=[one, one, one, wide(3, 5), one, pl.BlockSpec((tm, LANES), lambda i, j: (i, 0))],
        out_shape=[jax.ShapeDtypeStruct((t, tn), F32)] * 3
        + [jax.ShapeDtypeStruct((t, 3 * tn), F32), jax.ShapeDtypeStruct((t, tn), F32),
           jax.ShapeDtypeStruct((t, LANES), F32)],
        scratch_shapes=[pltpu.VMEM((tm, d), BF16)],
        compiler_params=_params("parallel", "arbitrary"),
        name="inproj",
    )(x2d, g, w_main, w_ab)


def _bias_chain(dist, value_of_bucket):
    val = value_of_bucket(0)
    for k in range(1, NUM_BUCKETS):
        val = jnp.where(dist >= BUCKET_THR[k - 1], value_of_bucket(k), val)
    return val


def _bias_prompt_kernel(rb_ref, tiles_ref, far_ref, *, blk):
    h = pl.program_id(0)
    qi = lax.broadcasted_iota(jnp.int32, (blk, blk), 0)
    kj = lax.broadcasted_iota(jnp.int32, (blk, blk), 1)
    for delta in range(2):
        dist = delta * blk + qi - kj
        val = _bias_chain(dist, lambda k: rb_ref[k, h])
        if delta == 0:
            val = jnp.where(dist >= 0, val, NEG)
        tiles_ref[0, delta] = val
    far_ref[0] = jnp.full((SUBLANES, blk), rb_ref[NUM_BUCKETS - 1, h], F32)


def _bias_prompt(rel_bias, blk):
    nh = rel_bias.shape[1]
    assert blk + 1 >= BUCKET_THR[-1]
    return pl.pallas_call(
        functools.partial(_bias_prompt_kernel, blk=blk),
        grid=(nh,),
        in_specs=[pl.BlockSpec(memory_space=pltpu.SMEM)],
        out_specs=[pl.BlockSpec((1, 2, blk, blk), lambda h: (h, 0, 0, 0)),
                   pl.BlockSpec((1, SUBLANES, blk), lambda h: (h, 0, 0))],
        out_shape=[jax.ShapeDtypeStruct((nh, 2, blk, blk), F32), jax.ShapeDtypeStruct((nh, SUBLANES, blk), F32)],
        compiler_params=_params("arbitrary"),
        name="bias_prompt",
    )(rel_bias)


def _bias_sample_kernel(rbx_ref, last_ref, own_ref, far_ref, *, blk, t_new):
    tok = lax.broadcasted_iota(jnp.int32, (blk, LANES), 1) & (t_new - 1)
    pos = lax.broadcasted_iota(jnp.int32, (blk, LANES), 0)
    row = lambda k: rbx_ref[k:k + 1, :]
    last_ref[...] = _bias_chain(blk + tok - pos, row)
    tok = lax.broadcasted_iota(jnp.int32, (LANES, LANES), 1) & (t_new - 1)
    pos = lax.broadcasted_iota(jnp.int32, (LANES, LANES), 0)
    dist = tok - pos
    own_ref[...] = jnp.where((dist >= 0) & (pos < t_new), _bias_chain(dist, row), NEG)
    far_ref[...] = jnp.broadcast_to(row(NUM_BUCKETS - 1), (SUBLANES, LANES))


def _bias_sample(rel_bias, blk, t_new):
    nh = rel_bias.shape[1]
    assert nh * t_new <= LANES and blk >= LANES
    rbx = jnp.pad(jnp.repeat(rel_bias, t_new, axis=1), ((0, 0), (0, LANES - nh * t_new)))
    return pl.pallas_call(
        functools.partial(_bias_sample_kernel, blk=blk, t_new=t_new),
        out_shape=[jax.ShapeDtypeStruct((blk, LANES), F32), jax.ShapeDtypeStruct((LANES, LANES), F32),
                   jax.ShapeDtypeStruct((SUBLANES, LANES), F32)],
        name="bias_sample",
    )(rbx)


def _moba_prompt_kernel(q_ref, k_ref, v_ref, bias_ref, far_ref, o_ref, kb_ref, vb_ref, *, nb, blk, n_sel):
    scale = HEAD_DIM ** -0.5
    k = k_ref[0]
    kb_ref[...] = k.astype(BF16)
    vb_ref[...] = v_ref[0].astype(BF16)
    means = jnp.sum(k.reshape(nb, blk, HEAD_DIM), axis=1) * (1.0 / blk)
    far = far_ref[0, 0:1, :]
    jidx = lax.broadcasted_iota(jnp.int32, (blk, nb), 1)
    for c in range(nb):
        rows = slice(c * blk, (c + 1) * blk)
        q = q_ref[0, rows, :]
        qb = (q * scale).astype(BF16)
        rank = None
        if c > n_sel:
            sc = jnp.where(jidx < c, _dot_nt(q, means, HI), NEG)
            rank = jnp.zeros((blk, nb), jnp.int32)
            for i in range(c):
                ci = sc[:, i:i + 1]
                tie = (jidx > i).astype(jnp.int32)
                rank = rank + jnp.where(ci > sc, 1, jnp.where(ci == sc, tie, 0))
        parts = []
        for j in range(c):
            sj = _dot_nt(qb, kb_ref[j * blk:(j + 1) * blk, :])
            sj = sj + (bias_ref[0, 1] if j == c - 1 else far)
            if rank is not None:
                sj = jnp.where(rank[:, j:j + 1] < n_sel, sj, NEG)
            parts.append(sj)
        parts.append(_dot_nt(qb, kb_ref[rows, :]) + bias_ref[0, 0])
        mm = parts[0]
        for p in parts[1:]:
            mm = jnp.maximum(mm, p)
        m = jnp.max(mm, axis=1, keepdims=True)
        ps = [jnp.exp(p - m) for p in parts]
        tot = ps[0]
        for p in ps[1:]:
            tot = tot + p
        l = jnp.sum(tot, axis=1, keepdims=True)
        pcat = jnp.concatenate([p.astype(BF16) for p in ps], axis=1)
        o = _dot(pcat, vb_ref[0:(c + 1) * blk, :])
        o_ref[0, rows, :] = (o / l).astype(o_ref.dtype)


def _moba_prompt(q, k, v, bias_tiles, bias_far):
    b, s, w = q.shape
    nh = w // HEAD_DIM
    blk = MOBA_BLOCK
    assert s % blk == 0
    nb = s // blk
    n_sel = min(MOBA_TOP_K, nb - 1)
    col = pl.BlockSpec((1, s, HEAD_DIM), lambda bi, h: (bi, 0, h))
    return pl.pallas_call(
        functools.partial(_moba_prompt_kernel, nb=nb, blk=blk, n_sel=n_sel),
        grid=(b, nh),
        in_specs=[col, col, col,
                  pl.BlockSpec((1, 2, blk, blk), lambda bi, h: (h, 0, 0, 0)),
                  pl.BlockSpec((1, SUBLANES, blk), lambda bi, h: (h, 0, 0))],
        out_specs=col,
        out_shape=jax.ShapeDtypeStruct((b, s, w), BF16),
        scratch_shapes=[pltpu.VMEM((s, HEAD_DIM), BF16), pltpu.VMEM((s, HEAD_DIM), BF16)],
        compiler_params=_params("parallel", "parallel"),
        name="moba_prompt",
    )(q, k, v, bias_tiles, bias_far)


def _moba_sample_kernel(pt_ref, q_ref, kn_ref, vn_ref, blast_ref, bown_ref, bfar_ref, *rest,
                        n_pages, ppb, n_sel):
    del pt_ref
    kps, vps, o_ref = rest[:n_pages], rest[n_pages:2 * n_pages], rest[2 * n_pages]
    scale = HEAD_DIM ** -0.5
    t_new, width = q_ref.shape
    nh = width // HEAD_DIM
    cur = n_pages // ppb
    blk = ppb * PAGE_SIZE
    rid = lax.broadcasted_iota(jnp.int32, (LANES, width), 0)
    cid = lax.broadcasted_iota(jnp.int32, (LANES, width), 1)
    q_rep = jnp.concatenate([q_ref[...]] * (LANES // t_new), axis=0)
    qt = jnp.where((rid >> int(math.log2(t_new))) == (cid >> int(math.log2(HEAD_DIM))), q_rep, 0.0)
    qt_s = (qt * scale).astype(BF16)

    logits, ksums = [], []
    for j in range(n_pages):
        kj = kps[j][0]
        logits.append(_dot_nt(kj.astype(BF16), qt_s))
        ksums.append(jnp.sum(kj, axis=0, keepdims=True))
    means = []
    for jb in range(cur):
        tot = ksums[jb * ppb]
        for r in range(1, ppb):
            tot = tot + ksums[jb * ppb + r]
        means.append(tot * (1.0 / blk))
    sc = _dot_nt(jnp.concatenate(means, axis=0), qt, HI)
    bidx = lax.broadcasted_iota(jnp.int32, (cur, LANES), 0)
    rank = jnp.zeros((cur, LANES), jnp.int32)
    for i in range(cur):
        ri = sc[i:i + 1, :]
        tie = (bidx > i).astype(jnp.int32)
        rank = rank + jnp.where(ri > sc, 1, jnp.where(ri == sc, tie, 0))

    parts = []
    for jb in range(cur):
        sb = jnp.concatenate(logits[jb * ppb:(jb + 1) * ppb], axis=0)
        sb = sb + (blast_ref[...] if jb == cur - 1 else bfar_ref[0:1, :])
        parts.append(jnp.where(rank[jb:jb + 1, :] < n_sel, sb, NEG))
    pad = jnp.zeros((LANES - t_new, width), F32)
    kn = jnp.concatenate([kn_ref[...], pad], axis=0)
    vn = jnp.concatenate([vn_ref[...], pad], axis=0)
    parts.append(_dot_nt(kn.astype(BF16), qt_s) + bown_ref[...])
    mm = parts[0]
    for p in parts[1:-1]:
        mm = jnp.maximum(mm, p)
    m = jnp.maximum(jnp.max(mm, axis=0, keepdims=True), jnp.max(parts[-1], axis=0, keepdims=True))
    acc = jnp.zeros((LANES, width), F32)
    lsum = jnp.zeros((LANES, 1), F32)
    for jb in range(cur + 1):
        pt = jnp.exp(parts[jb] - m).T
        lsum = lsum + jnp.sum(pt, axis=1, keepdims=True)
        if jb < cur:
            vb = jnp.concatenate([vps[jb * ppb + r][0].astype(BF16) for r in range(ppb)], axis=0)
        else:
            vb = vn.astype(BF16)
        acc = acc + _dot(pt.astype(BF16), vb)
    out = acc / lsum
    for h in range(nh):
        o_ref[:, h * HEAD_DIM:(h + 1) * HEAD_DIM] = out[h * t_new:(h + 1) * t_new, h * HEAD_DIM:(h + 1) * HEAD_DIM]


def _moba_sample(q, k_new, v_new, cache_k, cache_v, page_table, blast, bown, bfar, *, t_new):
    rows, width = q.shape
    db = rows // t_new
    n_pool = cache_k.shape[0]
    n_pages = page_table.shape[1]
    ppb = MOBA_BLOCK // PAGE_SIZE
    assert n_pages % ppb == 0, "new tokens must start a fresh MoBA block"
    assert t_new == SUBLANES
    cur = n_pages // ppb
    n_sel = min(MOBA_TOP_K, cur)
    assert n_sel > 0
    ck = cache_k.reshape(n_pool, PAGE_SIZE, width)
    cv = cache_v.reshape(n_pool, PAGE_SIZE, width)
    new = pl.BlockSpec((t_new, width), lambda s, pt: (s, 0))
    const = lambda shape: pl.BlockSpec(shape, lambda s, pt: (0, 0))
    page = lambda j: pl.BlockSpec((1, PAGE_SIZE, width), lambda s, pt, j=j: (pt[s, j], 0, 0))
    grid_spec = pltpu.PrefetchScalarGridSpec(
        num_scalar_prefetch=1,
        grid=(db,),
        in_specs=[new, new, new, const(blast.shape), const(bown.shape), const(bfar.shape)]
        + [page(j) for j in range(n_pages)] * 2,
        out_specs=new,
    )
    return pl.pallas_call(
        functools.partial(_moba_sample_kernel, n_pages=n_pages, ppb=ppb, n_sel=n_sel),
        grid_spec=grid_spec,
        out_shape=jax.ShapeDtypeStruct((rows, width), F32),
        compiler_params=_params("parallel"),
        name="moba_sample",
    )(page_table, q, k_new, v_new, blast, bown, bfar, *([ck] * n_pages), *([cv] * n_pages))


def _gdn_kernel(alog_ref, dtb_ref, q_ref, k_ref, v_ref, z_ref, ab_ref, cwq_ref, cwk_ref, cwv_ref,
                pq_ref, pk_ref, pv_ref, nw_ref, s0_ref, o_ref, so_ref, *scratch, tt, chunk, sample, n_heads):
    h = pl.program_id(1)
    n_chunks = tt // chunk
    raw = [q_ref[0], k_ref[0], v_ref[0]]
    cws = [cwq_ref[...], cwk_ref[...], cwv_ref[...]]
    prevs = [pq_ref, pk_ref, pv_ref]
    rid = lax.broadcasted_iota(jnp.int32, (tt, HEAD_DIM), 0)
    rid8 = rid[:SUBLANES]
    if sample:
        rmod = rid & (chunk - 1)
    else:
        s_scr, tail_scr = scratch
        i = pl.program_id(2)

        @pl.when(i == 0)
        def _():
            s_scr[...] = s0_ref[0, 0]
            for n in range(3):
                tail_scr[n] = prevs[n][0]

    xs = []
    for n in range(3):
        x, w = raw[n], cws[n]
        acc = None
        for shift in range(GDN_CONV - 1, 0, -1):
            rolled = pltpu.roll(x, shift, axis=0)
            if sample:
                xm = jnp.where(rmod >= shift, rolled, prevs[n][shift - 1, 0])
            else:
                first = jnp.where(rid8 < shift, pltpu.roll(tail_scr[n], shift, axis=0), rolled[:SUBLANES])
                xm = jnp.concatenate([first, rolled[SUBLANES:]], axis=0)
            tap = GDN_CONV - 1 - shift
            term = xm * w[tap:tap + 1, :]
            acc = term if acc is None else acc + term
        acc = acc + x * w[GDN_CONV - 1:GDN_CONV, :]
        xs.append(_silu(acc))
        if not sample:
            tail_scr[n] = x[tt - SUBLANES:]
    xq, xk, xv = xs

    ab = ab_ref[0]
    lane = lax.broadcasted_iota(jnp.int32, (tt, LANES), 1)
    a_col = jnp.sum(jnp.where(lane == h, ab, 0.0), axis=1, keepdims=True)
    b_col = jnp.sum(jnp.where(lane == h + n_heads, ab, 0.0), axis=1, keepdims=True)
    alog = jnp.full((tt, 1), alog_ref[h], F32)
    dtb = jnp.full((tt, 1), dtb_ref[h], F32)
    g_col = -jnp.exp(alog) * _softplus(a_col + dtb)
    beta = jax.nn.sigmoid(b_col)
    g_rep = jnp.broadcast_to(g_col, (tt, HEAD_DIM))

    ri = lax.broadcasted_iota(jnp.int32, (tt, tt), 0)
    ci = lax.broadcasted_iota(jnp.int32, (tt, tt), 1)
    shift_c = int(math.log2(chunk))
    same = (ri >> shift_c) == (ci >> shift_c)
    incl = same & (ri >= ci)
    strict = same & (ri > ci)
    incl_t = same & (ri <= ci)
    cg_rep = _dot(incl.astype(F32), g_rep, HI)
    gl_rep = _dot(same.astype(F32), g_rep, HI)
    cg_col = cg_rep[:, 0:1]
    cg_row = jnp.sum(jnp.where(incl_t, g_col, 0.0), axis=0, keepdims=True)
    decay = jnp.exp(jnp.where(incl, cg_col - cg_row, NEG))

    qn = xq * lax.rsqrt(jnp.sum(xq * xq, axis=-1, keepdims=True) + 1e-6) * (HEAD_DIM ** -0.5)
    kn = xk * lax.rsqrt(jnp.sum(xk * xk, axis=-1, keepdims=True) + 1e-6)
    kk = _dot_nt(kn, kn, HI)
    xp = -jnp.where(strict, beta * decay * kk, 0.0)
    tinv = jnp.where(ri == ci, 1.0, 0.0) + xp
    for _ in range(int(math.log2(chunk)) - 1):
        xp = _dot(xp, xp, HI)
        tinv = tinv + _dot(tinv, xp, HI)
    eg = jnp.exp(cg_rep)
    wu = _dot(tinv, jnp.concatenate([kn * (beta * eg), xv * beta], axis=1), HI)
    w, u0 = wu[:, :HEAD_DIM], wu[:, HEAD_DIM:]
    amat = jnp.where(incl, _dot_nt(qn, kn, HI) * decay, 0.0)
    qe = qn * eg
    kd = kn * jnp.exp(gl_rep - cg_rep)
    egl = jnp.exp(gl_rep)

    if sample:
        rs = []
        for c in range(n_chunks):
            cs = slice(c * chunk, (c + 1) * chunk)
            rs.append(_dot(jnp.concatenate([w[cs], qe[cs]], axis=0), s0_ref[c, 0], HI))
        u = u0 - jnp.concatenate([r[:chunk] for r in rs], axis=0)
        o = jnp.concatenate([r[chunk:] for r in rs], axis=0) + _dot(amat, u, HI)
        for c in range(n_chunks):
            cs = slice(c * chunk, (c + 1) * chunk)
            so_ref[c, 0] = egl[c * chunk:c * chunk + 1, :] * s0_ref[c, 0] + _dot_tn(kd[cs], u[cs], HI)
    else:
        kd_t = kd.T
        s = s_scr[...]
        outs = []
        for c in range(n_chunks):
            cs = slice(c * chunk, (c + 1) * chunk)
            r = _dot(jnp.concatenate([w[cs], qe[cs]], axis=0), s, HI)
            u = u0[cs] - r[:chunk]
            pieces = [u]
            if c > 0:
                pieces.insert(0, jnp.zeros((c * chunk, HEAD_DIM), F32))
            if c < n_chunks - 1:
                pieces.append(jnp.zeros((tt - (c + 1) * chunk, HEAD_DIM), F32))
            u_full = jnp.concatenate(pieces, axis=0)
            outs.append(r[chunk:] + _dot(amat[cs, :], u_full, HI))
            s = egl[c * chunk:c * chunk + 1, :] * s + _dot(kd_t, u_full, HI)
        s_scr[...] = s
        so_ref[0, 0] = s
        o = jnp.concatenate(outs, axis=0)

    z = z_ref[0]
    on = o * lax.rsqrt(jnp.mean(o * o, axis=-1, keepdims=True) + EPS) * nw_ref[...] * _silu(z)
    o_ref[0] = on.astype(o_ref.dtype)


def _gdn(gqkv, z, ab, conv_w, prev, a_log, dt_bias, norm_w, s0, *, sample):
    g, r, w3 = gqkv.shape
    w = w3 // 3
    nh = w // HEAD_DIM
    tt = 256
    assert r % tt == 0
    smem = pl.BlockSpec(memory_space=pltpu.SMEM)
    if sample:
        assert r == tt
        chunk = SUBLANES
        seqs = tt // chunk
        grid = (g, nh)
        col = lambda off: pl.BlockSpec((1, tt, HEAD_DIM), lambda gi, h, off=off: (gi, 0, off + h))
        prev_spec = lambda off: pl.BlockSpec((GDN_CONV - 1, 1, tt, HEAD_DIM), lambda gi, h, off=off: (0, gi, 0, off + h))
        state = pl.BlockSpec((seqs, 1, HEAD_DIM, HEAD_DIM), lambda gi, h: (gi, h, 0, 0))
        cw = lambda off: pl.BlockSpec((GDN_CONV, HEAD_DIM), lambda gi, h, off=off: (0, off + h))
        row = pl.BlockSpec((1, HEAD_DIM), lambda gi, h: (0, 0))
        abs_ = pl.BlockSpec((1, tt, LANES), lambda gi, h: (gi, 0, 0))
        scratch = []
        sem = ("parallel", "parallel")
    else:
        chunk = GDN_CHUNK
        grid = (g, nh, r // tt)
        col = lambda off: pl.BlockSpec((1, tt, HEAD_DIM), lambda gi, h, i, off=off: (gi, i, off + h))
        prev_spec = lambda off: pl.BlockSpec((1, SUBLANES, HEAD_DIM), lambda gi, h, i, off=off: (gi, 0, off + h))
        state = pl.BlockSpec((1, 1, HEAD_DIM, HEAD_DIM), lambda gi, h, i: (gi, h, 0, 0))
        cw = lambda off: pl.BlockSpec((GDN_CONV, HEAD_DIM), lambda gi, h, i, off=off: (0, off + h))
        row = pl.BlockSpec((1, HEAD_DIM), lambda gi, h, i: (0, 0))
        abs_ = pl.BlockSpec((1, tt, LANES), lambda gi, h, i: (gi, i, 0))
        scratch = [pltpu.VMEM((HEAD_DIM, HEAD_DIM), F32), pltpu.VMEM((3, SUBLANES, HEAD_DIM), F32)]
        sem = ("parallel", "parallel", "arbitrary")
    return pl.pallas_call(
        functools.partial(_gdn_kernel, tt=tt, chunk=chunk, sample=sample, n_heads=nh),
        grid=grid,
        in_specs=[smem, smem, col(0), col(nh), col(2 * nh), col(0), abs_, cw(0), cw(nh), cw(2 * nh),
                  prev_spec(0), prev_spec(nh), prev_spec(2 * nh), row, state],
        out_specs=[col(0), state],
        out_shape=[jax.ShapeDtypeStruct((g, r, w), BF16), jax.ShapeDtypeStruct(s0.shape, F32)],
        scratch_shapes=scratch,
        compiler_params=_params(*sem),
        name="gdn_sample" if sample else "gdn_prompt",
    )(a_log, dt_bias, gqkv, gqkv, gqkv, z, ab, conv_w, conv_w, conv_w, prev, prev, prev, norm_w, s0)


def _outproj_kernel(x_ref, att_ref, gdn_ref, wa_ref, wb_ref, o_ref):
    o_ref[...] = (x_ref[...] + _dot(att_ref[...].astype(BF16), wa_ref[...])
                  + _dot(gdn_ref[...].astype(BF16), wb_ref[...]))


def _outproj(x2d, att, gdn, w_att, w_gdn, *, tm):
    t, d = x2d.shape
    wa = att.shape[1]
    wg = gdn.shape[1]
    return pl.pallas_call(
        _outproj_kernel,
        grid=(t // tm,),
        in_specs=[pl.BlockSpec((tm, d), lambda i: (i, 0)), pl.BlockSpec((tm, wa), lambda i: (i, 0)),
                  pl.BlockSpec((tm, wg), lambda i: (i, 0)), pl.BlockSpec((wa, d), lambda i: (0, 0)),
                  pl.BlockSpec((wg, d), lambda i: (0, 0))],
        out_specs=pl.BlockSpec((tm, d), lambda i: (i, 0)),
        out_shape=jax.ShapeDtypeStruct((t, d), F32),
        compiler_params=_params("parallel"),
        name="outproj",
    )(x2d, att, gdn, w_att, w_gdn)


def _ffn_kernel(x_ref, g_ref, wg_ref, wv_ref, cwg_ref, cwv_ref, cbg_ref, cbv_ref, wd_ref, gf_ref, *rest,
                tm, tf, nf, sample, tiles_per_seq, final_norm):
    if sample:
        pg_ref, pv_ref, y_ref, ug_ref, uv_ref, h_ref, acc_ref = rest
    else:
        y_ref, ug_ref, uv_ref, h_ref, acc_ref, carry_g, carry_v = rest
    i = pl.program_id(0)
    f = pl.program_id(1)

    @pl.when(f == 0)
    def _():
        def store(sl, y):
            h_ref[sl, :] = y.astype(BF16)
        _rmsnorm_rows(lambda sl: x_ref[sl, :], g_ref[...], store, tm)
        acc_ref[...] = jnp.zeros_like(acc_ref)

    hb = h_ref[...]
    rid8 = lax.broadcasted_iota(jnp.int32, (SUBLANES, tf), 0)
    if sample:
        rmod = lax.broadcasted_iota(jnp.int32, (tm, tf), 0) & (SUBLANES - 1)

    def conv(u, cw_ref, cb_ref, prev_ref, carry_ref, out_ref):
        shifted = []
        for shift in range(1, FFN_CONV):
            rolled = pltpu.roll(u, shift, axis=0)
            if sample:
                shifted.append(jnp.where(rmod >= shift, rolled, prev_ref[shift - 1]))
            else:
                prev8 = jnp.where(i % tiles_per_seq == 0, 0.0, carry_ref[f])
                first = jnp.where(rid8 < shift, pltpu.roll(prev8, shift, axis=0), rolled[:SUBLANES])
                shifted.append(jnp.concatenate([first, rolled[SUBLANES:]], axis=0))
        if sample:
            out_ref[...] = u
        else:
            carry_ref[f] = u[tm - SUBLANES:]
            out_ref[0] = u[tm - SUBLANES:]
        cw = cw_ref[...]
        y = shifted[1] * cw[0:1, :] + shifted[0] * cw[1:2, :] + u * cw[2:3, :]
        return y + cb_ref[...]

    if sample:
        yg = conv(_dot(hb, wg_ref[...]), cwg_ref, cbg_ref, pg_ref, None, ug_ref)
        yv = conv(_dot(hb, wv_ref[...]), cwv_ref, cbv_ref, pv_ref, None, uv_ref)
    else:
        yg = conv(_dot(hb, wg_ref[...]), cwg_ref, cbg_ref, None, carry_g, ug_ref)
        yv = conv(_dot(hb, wv_ref[...]), cwv_ref, cbv_ref, None, carry_v, uv_ref)
    act = (_silu(yg) * yv).astype(BF16)
    acc_ref[...] += _dot(act, wd_ref[...])

    @pl.when(f == nf - 1)
    def _():
        if final_norm:
            def store(sl, y):
                y_ref[sl, :] = y
            _rmsnorm_rows(lambda sl: x_ref[sl, :] + acc_ref[sl, :], gf_ref[...], store, tm)
        else:
            y_ref[...] = x_ref[...] + acc_ref[...]


def _ffn(x2d, g, w_up, conv_w, conv_b, w_down, gf, prev, *, tm, seq_rows, sample, final_norm):
    t, d = x2d.shape
    dff = w_down.shape[0]
    tf = 512
    nf = dff // tf
    assert dff % tf == 0 and t % tm == 0
    gate = lambda shape: pl.BlockSpec(shape, lambda i, f: (0, f))
    val = lambda shape: pl.BlockSpec(shape, lambda i, f: (0, nf + f))
    in_specs = [pl.BlockSpec((tm, d), lambda i, f: (i, 0)), pl.BlockSpec((1, d), lambda i, f: (0, 0)),
                gate((d, tf)), val((d, tf)), gate((FFN_CONV, tf)), val((FFN_CONV, tf)),
                gate((1, tf)), val((1, tf)),
                pl.BlockSpec((tf, d), lambda i, f: (f, 0)), pl.BlockSpec((1, d), lambda i, f: (0, 0))]
    args = [x2d, g, w_up, w_up, conv_w, conv_w, conv_b, conv_b, w_down, gf]
    scratch = [pltpu.VMEM((tm, d), BF16), pltpu.VMEM((tm, d), F32)]
    y_spec = pl.BlockSpec((tm, d), lambda i, f: (i, 0))
    if sample:
        tiles_per_seq = 0
        in_specs += [pl.BlockSpec((FFN_CONV - 1, tm, tf), lambda i, f: (0, i, f)),
                     pl.BlockSpec((FFN_CONV - 1, tm, tf), lambda i, f: (0, i, nf + f))]
        args += [prev, prev]
        u_spec = pl.BlockSpec((tm, tf), lambda i, f: (i, f))
        u_shape = jax.ShapeDtypeStruct((t, dff), F32)
    else:
        assert seq_rows % tm == 0
        tiles_per_seq = seq_rows // tm
        u_spec = pl.BlockSpec((1, SUBLANES, tf), lambda i, f: (i, 0, f))
        u_shape = jax.ShapeDtypeStruct((t // tm, SUBLANES, dff), F32)
        scratch += [pltpu.VMEM((nf, SUBLANES, tf), F32)] * 2
    return pl.pallas_call(
        functools.partial(_ffn_kernel, tm=tm, tf=tf, nf=nf, sample=sample, tiles_per_seq=tiles_per_seq,
                          final_norm=final_norm),
        grid=(t // tm, nf),
        in_specs=in_specs,
        out_specs=[y_spec, u_spec, u_spec],
        out_shape=[jax.ShapeDtypeStruct((t, d), F32), u_shape, u_shape],
        scratch_shapes=scratch,
        compiler_params=_params("arbitrary", "arbitrary"),
        name="ffn_sample" if sample else "ffn_prompt",
    )(*args)


def _spread_prev(state, n_shift, t_new):
    db, wm1, c = state.shape
    out = []
    for k in range(1, n_shift + 1):
        rows = [state[:, wm1 - k + r] for r in range(k)]
        rows += [jnp.zeros((db, c), state.dtype)] * (t_new - k)
        out.append(jnp.stack(rows, axis=1).reshape(db * t_new, c))
    return jnp.stack(out, axis=0)


def kernel(x_prompt, x_sample, cache_k, cache_v, page_table, state_gdn, state_gdn_conv, state_ffn_conv,
           rms_mix_w, w_in, gdn_conv_w, gdn_a_log, gdn_dt_bias, gdn_norm_w, w_out, rms_ffn_w, w_up,
           ffn_conv_w, ffn_conv_b, w_down, rel_bias, final_norm_w):
    b, s, d = x_prompt.shape
    db, t_new, _ = x_sample.shape
    depth = w_in.shape[0]
    n_att = cache_k.shape[3]
    w_att = n_att * HEAD_DIM
    n_gdn = gdn_a_log.shape[1]
    w_gdn = n_gdn * HEAD_DIM
    dff = w_down.shape[1]
    assert w_att == w_gdn == 1024 and w_in.shape[2] == 3 * w_att + 4 * w_gdn + 2 * n_gdn
    n_main = 3 * w_att + 4 * w_gdn
    grp = 256 // t_new
    tm = 512

    bias_tiles, bias_far = _bias_prompt(rel_bias, MOBA_BLOCK)
    blast, bown, bfar = _bias_sample(rel_bias, MOBA_BLOCK, t_new)

    xp = x_prompt.reshape(b * s, d)
    xs = x_sample.reshape(db * t_new, d)
    outs_p = [[] for _ in range(5)]
    outs_s = [[] for _ in range(5)]
    for l in range(depth):
        w_in_b = w_in[l].astype(BF16)
        w_main = w_in_b[:, :n_main]
        w_ab = jnp.pad(w_in_b[:, n_main:], ((0, 0), (0, LANES - 2 * n_gdn)))
        w_out_b = w_out[l].astype(BF16)
        w_up_b = w_up[l].astype(BF16)
        w_down_b = w_down[l].astype(BF16)
        g_mix = rms_mix_w[l].reshape(1, d)
        g_ffn = rms_ffn_w[l].reshape(1, d)
        g_fin = final_norm_w.reshape(1, d)
        norm_w = gdn_norm_w[l].reshape(1, HEAD_DIM)
        conv_b = ffn_conv_b[l].reshape(1, 2 * dff)
        last = l == depth - 1

        q, k, v, gq, z, ab = _inproj(xp, g_mix, w_main, w_ab, tm=tm)
        att = _moba_prompt(q.reshape(b, s, w_att), k.reshape(b, s, w_att), v.reshape(b, s, w_att),
                           bias_tiles, bias_far)
        gdn, s_new = _gdn(gq.reshape(b, s, 3 * w_gdn), z.reshape(b, s, w_gdn), ab.reshape(b, s, LANES),
                          gdn_conv_w[l], jnp.zeros((b, SUBLANES, 3 * w_gdn), F32), gdn_a_log[l], gdn_dt_bias[l],
                          norm_w, jnp.zeros((b, n_gdn, HEAD_DIM, HEAD_DIM), F32), sample=False)
        x1 = _outproj(xp, att.reshape(b * s, w_att), gdn.reshape(b * s, w_gdn), w_out_b[:w_att], w_out_b[w_att:],
                      tm=tm)
        y, ug, uv = _ffn(x1, g_ffn, w_up_b, ffn_conv_w[l], conv_b, w_down_b, g_fin, None,
                         tm=tm, seq_rows=s, sample=False, final_norm=last)
        outs_p[0].append(k.reshape(b, s, n_att, HEAD_DIM))
        outs_p[1].append(v.reshape(b, s, n_att, HEAD_DIM))
        outs_p[2].append(s_new)
        outs_p[3].append(gq.reshape(b, s, 3 * w_gdn)[:, s - (GDN_CONV - 1):])
        tps = s // tm
        outs_p[4].append(jnp.concatenate([ug, uv], axis=-1)[tps - 1::tps, SUBLANES - (FFN_CONV - 1):])
        xp = y

        ts = db * t_new
        q, k, v, gq, z, ab = _inproj(xs, g_mix, w_main, w_ab, tm=tm)
        att = _moba_sample(q, k, v, cache_k[l], cache_v[l], page_table, blast, bown, bfar, t_new=t_new)
        prev_g = _spread_prev(state_gdn_conv[l], GDN_CONV - 1, t_new).reshape(GDN_CONV - 1, db // grp, 256, 3 * w_gdn)
        gdn, s_new = _gdn(gq.reshape(db // grp, 256, 3 * w_gdn), z.reshape(db // grp, 256, w_gdn),
                          ab.reshape(db // grp, 256, LANES), gdn_conv_w[l], prev_g, gdn_a_log[l], gdn_dt_bias[l],
                          norm_w, state_gdn[l], sample=True)
        x1 = _outproj(xs, att, gdn.reshape(ts, w_gdn), w_out_b[:w_att], w_out_b[w_att:], tm=tm)
        prev_f = _spread_prev(state_ffn_conv[l], FFN_CONV - 1, t_new)
        y, ug, uv = _ffn(x1, g_ffn, w_up_b, ffn_conv_w[l], conv_b, w_down_b, g_fin, prev_f,
                         tm=tm, seq_rows=t_new, sample=True, final_norm=last)
        outs_s[0].append(k.reshape(db, t_new, n_att, HEAD_DIM))
        outs_s[1].append(v.reshape(db, t_new, n_att, HEAD_DIM))
        outs_s[2].append(s_new)
        outs_s[3].append(gq.reshape(db, t_new, 3 * w_gdn)[:, t_new - (GDN_CONV - 1):])
        outs_s[4].append(jnp.concatenate([ug, uv], axis=-1).reshape(db, t_new, 2 * dff)[:, t_new - (FFN_CONV - 1):])
        xs = y

    return (xp.reshape(b, s, d), xs.reshape(db, t_new, d),
            *[jnp.stack(o) for o in outs_p], *[jnp.stack(o) for o in outs_s])
```

```python
import functools
import math

import jax
import jax.numpy as jnp
import numpy as np
from jax import lax
from jax.experimental import pallas as pl
from jax.experimental.pallas import tpu as pltpu

F32 = jnp.float32
BF16 = jnp.bfloat16
HI = lax.Precision.HIGHEST

HEAD_DIM = 128
MOBA_BLOCK = 256
MOBA_TOP_K = 3
PAGE_SIZE = 128
GDN_CONV = 4
GDN_CHUNK = 64
FFN_CONV = 3
NUM_BUCKETS = 32
MAX_DISTANCE = 128
EPS = 1e-6
NEG = -1e30

LANES = 128
SUBLANES = 8
VMEM_LIMIT = 56 * 1024 * 1024


def _bucket_thresholds():
    n = np.arange(0, 4 * MAX_DISTANCE)
    max_exact = NUM_BUCKETS // 2
    nf = np.maximum(n, 1).astype(np.float32)
    large = max_exact + (np.log(nf / np.float32(max_exact)) / np.float32(math.log(MAX_DISTANCE / max_exact))
                         * np.float32(NUM_BUCKETS - max_exact)).astype(np.int32)
    large = np.minimum(large, NUM_BUCKETS - 1)
    b = np.where(n < max_exact, n, large)
    return [int(np.argmax(b >= k)) for k in range(1, NUM_BUCKETS)]


BUCKET_THR = _bucket_thresholds()


def _dot(a, b, precision=None):
    return jnp.dot(a, b, preferred_element_type=F32, precision=precision)


def _dot_nt(a, b, precision=None):
    return lax.dot_general(a, b, (((1,), (1,)), ((), ())), preferred_element_type=F32, precision=precision)


def _dot_tn(a, b, precision=None):
    return lax.dot_general(a, b, (((0,), (0,)), ((), ())), preferred_element_type=F32, precision=precision)


def _silu(x):
    return x * jax.nn.sigmoid(x)


def _softplus(x):
    return jnp.maximum(x, 0.0) + jnp.log1p(jnp.exp(-jnp.abs(x)))


def _params(*semantics):
    return pltpu.CompilerParams(dimension_semantics=semantics, vmem_limit_bytes=VMEM_LIMIT)


def _rmsnorm_rows(load, g, store, rows, chunk=256):
    def body(r, carry):
        sl = pl.ds(pl.multiple_of(r * chunk, chunk), chunk)
        x = load(sl)
        ms = jnp.mean(x * x, axis=-1, keepdims=True)
        store(sl, x * lax.rsqrt(ms + EPS) * g)
        return carry
    lax.fori_loop(0, rows // chunk, body, 0)


def _inproj_kernel(x_ref, g_ref, w_ref, wab_ref, q_ref, k_ref, v_ref, gq_ref, z_ref, ab_ref, h_ref, *, tm):
    j = pl.program_id(1)

    @pl.when(j == 0)
    def _():
        def store(sl, y):
            h_ref[sl, :] = y.astype(BF16)
        _rmsnorm_rows(lambda sl: x_ref[sl, :], g_ref[...], store, tm)
        ab_ref[...] = _dot(h_ref[...], wab_ref[...])

    for idx, ref in ((0, q_ref), (1, k_ref), (2, v_ref), (6, z_ref)):
        @pl.when(j == idx)
        def _(ref=ref):
            ref[...] = _dot(h_ref[...], w_ref[...])

    @pl.when((j >= 3) & (j <= 5))
    def _():
        gq_ref[...] = _dot(h_ref[...], w_ref[...])


def _inproj(x2d, g, w_main, w_ab, *, tm):
    t, d = x2d.shape
    tn = 1024
    n_tiles = w_main.shape[1] // tn
    assert n_tiles == 7 and t % tm == 0
    wide = lambda lo, hi: pl.BlockSpec((tm, tn), lambda i, j: (i, jnp.clip(j - lo, 0, hi - lo)))
    one = pl.BlockSpec((tm, tn), lambda i, j: (i, 0))
    return pl.pallas_call(
        functools.partial(_inproj_kernel, tm=tm),
        grid=(t // tm, n_tiles),
        in_specs=[pl.BlockSpec((tm, d), lambda i, j: (i, 0)),
                  pl.BlockSpec((1, d), lambda i, j: (0, 0)),
                  pl.BlockSpec((d, tn), lambda i, j: (0, j)),
                  pl.BlockSpec((d, LANES), lambda i, j: (0, 0))],
        out_specs=[one, one, one, wide(3, 5), one, pl.BlockSpec((tm, LANES), lambda i, j: (i, 0))],
        out_shape=[jax.ShapeDtypeStruct((t, tn), F32)] * 3
        + [jax.ShapeDtypeStruct((t, 3 * tn), F32), jax.ShapeDtypeStruct((t, tn), F32),
           jax.ShapeDtypeStruct((t, LANES), F32)],
        scratch_shapes=[pltpu.VMEM((tm, d), BF16)],
        compiler_params=_params("parallel", "arbitrary"),
        name="inproj",
    )(x2d, g, w_main, w_ab)


def _bias_chain(dist, value_of_bucket):
    val = value_of_bucket(0)
    for k in range(1, NUM_BUCKETS):
        val = jnp.where(dist >= BUCKET_THR[k - 1], value_of_bucket(k), val)
    return val


def _bias_prompt_kernel(rb_ref, tiles_ref, far_ref, *, blk):
    h = pl.program_id(0)
    qi = lax.broadcasted_iota(jnp.int32, (blk, blk), 0)
    kj = lax.broadcasted_iota(jnp.int32, (blk, blk), 1)
    for delta in range(2):
        dist = delta * blk + qi - kj
        val = _bias_chain(dist, lambda k: rb_ref[k, h])
        if delta == 0:
            val = jnp.where(dist >= 0, val, NEG)
        tiles_ref[0, delta] = val
    far_ref[0] = jnp.full((SUBLANES, blk), rb_ref[NUM_BUCKETS - 1, h], F32)


def _bias_prompt(rel_bias, blk):
    nh = rel_bias.shape[1]
    assert blk + 1 >= BUCKET_THR[-1]
    return pl.pallas_call(
        functools.partial(_bias_prompt_kernel, blk=blk),
        grid=(nh,),
        in_specs=[pl.BlockSpec(memory_space=pltpu.SMEM)],
        out_specs=[pl.BlockSpec((1, 2, blk, blk), lambda h: (h, 0, 0, 0)),
                   pl.BlockSpec((1, SUBLANES, blk), lambda h: (h, 0, 0))],
        out_shape=[jax.ShapeDtypeStruct((nh, 2, blk, blk), F32), jax.ShapeDtypeStruct((nh, SUBLANES, blk), F32)],
        compiler_params=_params("arbitrary"),
        name="bias_prompt",
    )(rel_bias)


def _bias_sample_kernel(rbx_ref, last_ref, own_ref, far_ref, *, blk, t_new):
    tok = lax.broadcasted_iota(jnp.int32, (blk, LANES), 1) & (t_new - 1)
    pos = lax.broadcasted_iota(jnp.int32, (blk, LANES), 0)
    row = lambda k: rbx_ref[k:k + 1, :]
    last_ref[...] = _bias_chain(blk + tok - pos, row)
    tok = lax.broadcasted_iota(jnp.int32, (LANES, LANES), 1) & (t_new - 1)
    pos = lax.broadcasted_iota(jnp.int32, (LANES, LANES), 0)
    dist = tok - pos
    own_ref[...] = jnp.where((dist >= 0) & (pos < t_new), _bias_chain(dist, row), NEG)
    far_ref[...] = jnp.broadcast_to(row(NUM_BUCKETS - 1), (SUBLANES, LANES))


def _bias_sample(rel_bias, blk, t_new):
    nh = rel_bias.shape[1]
    assert nh * t_new <= LANES and blk >= LANES
    rbx = jnp.pad(jnp.repeat(rel_bias, t_new, axis=1), ((0, 0), (0, LANES - nh * t_new)))
    return pl.pallas_call(
        functools.partial(_bias_sample_kernel, blk=blk, t_new=t_new),
        out_shape=[jax.ShapeDtypeStruct((blk, LANES), F32), jax.ShapeDtypeStruct((LANES, LANES), F32),
                   jax.ShapeDtypeStruct((SUBLANES, LANES), F32)],
        name="bias_sample",
    )(rbx)


def _moba_prompt_kernel(q_ref, k_ref, v_ref, bias_ref, far_ref, o_ref, kb_ref, vb_ref, *, nb, blk, n_sel):
    scale = HEAD_DIM ** -0.5
    k = k_ref[0]
    kb_ref[...] = k.astype(BF16)
    vb_ref[...] = v_ref[0].astype(BF16)
    means = jnp.sum(k.reshape(nb, blk, HEAD_DIM), axis=1) * (1.0 / blk)
    far = far_ref[0, 0:1, :]
    jidx = lax.broadcasted_iota(jnp.int32, (blk, nb), 1)
    for c in range(nb):
        rows = slice(c * blk, (c + 1) * blk)
        q = q_ref[0, rows, :]
        qb = (q * scale).astype(BF16)
        rank = None
        if c > n_sel:
            sc = jnp.where(jidx < c, _dot_nt(q, means, HI), NEG)
            rank = jnp.zeros((blk, nb), jnp.int32)
            for i in range(c):
                ci = sc[:, i:i + 1]
                tie = (jidx > i).astype(jnp.int32)
                rank = rank + jnp.where(ci > sc, 1, jnp.where(ci == sc, tie, 0))
        parts = []
        for j in range(c):
            sj = _dot_nt(qb, kb_ref[j * blk:(j + 1) * blk, :])
            sj = sj + (bias_ref[0, 1] if j == c - 1 else far)
            if rank is not None:
                sj = jnp.where(rank[:, j:j + 1] < n_sel, sj, NEG)
            parts.append(sj)
        parts.append(_dot_nt(qb, kb_ref[rows, :]) + bias_ref[0, 0])
        mm = parts[0]
        for p in parts[1:]:
            mm = jnp.maximum(mm, p)
        m = jnp.max(mm, axis=1, keepdims=True)
        ps = [jnp.exp(p - m) for p in parts]
        tot = ps[0]
        for p in ps[1:]:
            tot = tot + p
        l = jnp.sum(tot, axis=1, keepdims=True)
        pcat = jnp.concatenate([p.astype(BF16) for p in ps], axis=1)
        o = _dot(pcat, vb_ref[0:(c + 1) * blk, :])
        o_ref[0, rows, :] = (o / l).astype(o_ref.dtype)


def _moba_prompt(q, k, v, bias_tiles, bias_far):
    b, s, w = q.shape
    nh = w // HEAD_DIM
    blk = MOBA_BLOCK
    assert s % blk == 0
    nb = s // blk
    n_sel = min(MOBA_TOP_K, nb - 1)
    col = pl.BlockSpec((1, s, HEAD_DIM), lambda bi, h: (bi, 0, h))
    return pl.pallas_call(
        functools.partial(_moba_prompt_kernel, nb=nb, blk=blk, n_sel=n_sel),
        grid=(b, nh),
        in_specs=[col, col, col,
                  pl.BlockSpec((1, 2, blk, blk), lambda bi, h: (h, 0, 0, 0)),
                  pl.BlockSpec((1, SUBLANES, blk), lambda bi, h: (h, 0, 0))],
        out_specs=col,
        out_shape=jax.ShapeDtypeStruct((b, s, w), BF16),
        scratch_shapes=[pltpu.VMEM((s, HEAD_DIM), BF16), pltpu.VMEM((s, HEAD_DIM), BF16)],
        compiler_params=_params("parallel", "parallel"),
        name="moba_prompt",
    )(q, k, v, bias_tiles, bias_far)


def _moba_sample_kernel(pt_ref, q_ref, kn_ref, vn_ref, blast_ref, bown_ref, bfar_ref, *rest,
                        n_pages, ppb, n_sel):
    del pt_ref
    kps, vps, o_ref = rest[:n_pages], rest[n_pages:2 * n_pages], rest[2 * n_pages]
    scale = HEAD_DIM ** -0.5
    t_new, width = q_ref.shape
    nh = width // HEAD_DIM
    cur = n_pages // ppb
    blk = ppb * PAGE_SIZE
    rid = lax.broadcasted_iota(jnp.int32, (LANES, width), 0)
    cid = lax.broadcasted_iota(jnp.int32, (LANES, width), 1)
    q_rep = jnp.concatenate([q_ref[...]] * (LANES // t_new), axis=0)
    qt = jnp.where((rid >> int(math.log2(t_new))) == (cid >> int(math.log2(HEAD_DIM))), q_rep, 0.0)
    qt_s = (qt * scale).astype(BF16)

    logits, ksums = [], []
    for j in range(n_pages):
        kj = kps[j][0]
        logits.append(_dot_nt(kj.astype(BF16), qt_s))
        ksums.append(jnp.sum(kj, axis=0, keepdims=True))
    means = []
    for jb in range(cur):
        tot = ksums[jb * ppb]
        for r in range(1, ppb):
            tot = tot + ksums[jb * ppb + r]
        means.append(tot * (1.0 / blk))
    sc = _dot_nt(jnp.concatenate(means, axis=0), qt, HI)
    bidx = lax.broadcasted_iota(jnp.int32, (cur, LANES), 0)
    rank = jnp.zeros((cur, LANES), jnp.int32)
    for i in range(cur):
        ri = sc[i:i + 1, :]
        tie = (bidx > i).astype(jnp.int32)
        rank = rank + jnp.where(ri > sc, 1, jnp.where(ri == sc, tie, 0))

    parts = []
    for jb in range(cur):
        sb = jnp.concatenate(logits[jb * ppb:(jb + 1) * ppb], axis=0)
        sb = sb + (blast_ref[...] if jb == cur - 1 else bfar_ref[0:1, :])
        parts.append(jnp.where(rank[jb:jb + 1, :] < n_sel, sb, NEG))
    pad = jnp.zeros((LANES - t_new, width), F32)
    kn = jnp.concatenate([kn_ref[...], pad], axis=0)
    vn = jnp.concatenate([vn_ref[...], pad], axis=0)
    parts.append(_dot_nt(kn.astype(BF16), qt_s) + bown_ref[...])
    mm = parts[0]
    for p in parts[1:-1]:
        mm = jnp.maximum(mm, p)
    m = jnp.maximum(jnp.max(mm, axis=0, keepdims=True), jnp.max(parts[-1], axis=0, keepdims=True))
    acc = jnp.zeros((LANES, width), F32)
    lsum = jnp.zeros((LANES, 1), F32)
    for jb in range(cur + 1):
        pt = jnp.exp(parts[jb] - m).T
        lsum = lsum + jnp.sum(pt, axis=1, keepdims=True)
        if jb < cur:
            vb = jnp.concatenate([vps[jb * ppb + r][0].astype(BF16) for r in range(ppb)], axis=0)
        else:
            vb = vn.astype(BF16)
        acc = acc + _dot(pt.astype(BF16), vb)
    out = acc / lsum
    for h in range(nh):
        o_ref[:, h * HEAD_DIM:(h + 1) * HEAD_DIM] = out[h * t_new:(h + 1) * t_new, h * HEAD_DIM:(h + 1) * HEAD_DIM]


def _moba_sample(q, k_new, v_new, cache_k, cache_v, page_table, blast, bown, bfar, *, t_new):
    rows, width = q.shape
    db = rows // t_new
    n_pool = cache_k.shape[0]
    n_pages = page_table.shape[1]
    ppb = MOBA_BLOCK // PAGE_SIZE
    assert n_pages % ppb == 0, "new tokens must start a fresh MoBA block"
    assert t_new == SUBLANES
    cur = n_pages // ppb
    n_sel = min(MOBA_TOP_K, cur)
    assert n_sel > 0
    ck = cache_k.reshape(n_pool, PAGE_SIZE, width)
    cv = cache_v.reshape(n_pool, PAGE_SIZE, width)
    new = pl.BlockSpec((t_new, width), lambda s, pt: (s, 0))
    const = lambda shape: pl.BlockSpec(shape, lambda s, pt: (0, 0))
    page = lambda j: pl.BlockSpec((1, PAGE_SIZE, width), lambda s, pt, j=j: (pt[s, j], 0, 0))
    grid_spec = pltpu.PrefetchScalarGridSpec(
        num_scalar_prefetch=1,
        grid=(db,),
        in_specs=[new, new, new, const(blast.shape), const(bown.shape), const(bfar.shape)]
        + [page(j) for j in range(n_pages)] * 2,
        out_specs=new,
    )
    return pl.pallas_call(
        functools.partial(_moba_sample_kernel, n_pages=n_pages, ppb=ppb, n_sel=n_sel),
        grid_spec=grid_spec,
        out_shape=jax.ShapeDtypeStruct((rows, width), F32),
        compiler_params=_params("parallel"),
        name="moba_sample",
    )(page_table, q, k_new, v_new, blast, bown, bfar, *([ck] * n_pages), *([cv] * n_pages))


def _split(x):
    hi = x.astype(BF16)
    return hi, (x - hi.astype(F32)).astype(BF16)


def _dot3(a, b, nt=False):
    f = _dot_nt if nt else _dot
    return f(a[0], b[0]) + (f(a[0], b[1]) + f(a[1], b[0]))


def _gdn_kernel(alog_ref, dtb_ref, q_ref, k_ref, v_ref, z_ref, ab_ref, cwq_ref, cwk_ref, cwv_ref,
                pq_ref, pk_ref, pv_ref, nw_ref, s0_ref, o_ref, so_ref, *scratch, tt, chunk, sample, n_heads, hp):
    hg = pl.program_id(1)
    n_chunks = tt // chunk
    ins = [q_ref, k_ref, v_ref]
    cw_refs = [cwq_ref, cwk_ref, cwv_ref]
    prevs = [pq_ref, pk_ref, pv_ref]
    rid = lax.broadcasted_iota(jnp.int32, (tt, HEAD_DIM), 0)
    rid8 = lax.broadcasted_iota(jnp.int32, (SUBLANES, HEAD_DIM), 0)
    if sample:
        rmod = rid & (chunk - 1)
    else:
        s_scr, tail_scr = scratch
        i = pl.program_id(2)

        @pl.when(i == 0)
        def _():
            s_scr[...] = s0_ref[0]
            for n in range(3):
                tail_scr[n] = prevs[n][0]

    ri = lax.broadcasted_iota(jnp.int32, (tt, tt), 0)
    ci = lax.broadcasted_iota(jnp.int32, (tt, tt), 1)
    shift_c = int(math.log2(chunk))
    same_block = lambda bits: (ri >> bits) == (ci >> bits)
    same = same_block(shift_c)
    incl = same & (ri >= ci)
    strict = same & (ri > ci)
    incl_t = same & (ri <= ci)
    eye = jnp.where(ri == ci, 1.0, 0.0)
    base_bits = int(math.log2(SUBLANES))
    base = same_block(base_bits)
    level_masks = [same_block(bits + 1) & (ri >> bits != ci >> bits) for bits in range(base_bits, shift_c)]
    sum_mat = jnp.concatenate([jnp.where(incl, 1.0, 0.0), jnp.where(same, 1.0, 0.0)], axis=0).astype(BF16)
    lane = lax.broadcasted_iota(jnp.int32, (tt, LANES), 1)
    ab = ab_ref[0]
    heads = range(hp)
    cols = [slice(hh * HEAD_DIM, (hh + 1) * HEAD_DIM) for hh in heads]
    chunks = [slice(c * chunk, (c + 1) * chunk) for c in range(n_chunks)]


    xq, xk, xv = [], [], []
    for hh in heads:
        for n, dst in enumerate((xq, xk, xv)):
            x = ins[n][0, :, cols[hh]]
            w = cw_refs[n][:, cols[hh]]
            acc = None
            for shift in range(GDN_CONV - 1, 0, -1):
                rolled = pltpu.roll(x, shift, axis=0)
                if sample:
                    xm = jnp.where(rmod >= shift, rolled, prevs[n][shift - 1, 0, :, cols[hh]])
                else:
                    first = jnp.where(rid8 < shift, pltpu.roll(tail_scr[n, :, cols[hh]], shift, axis=0),
                                      rolled[:SUBLANES])
                    xm = jnp.concatenate([first, rolled[SUBLANES:]], axis=0)
                tap = GDN_CONV - 1 - shift
                term = xm * w[tap:tap + 1, :]
                acc = term if acc is None else acc + term
            acc = acc + x * w[GDN_CONV - 1:GDN_CONV, :]
            dst.append(_silu(acc))
            if not sample:
                tail_scr[n, :, cols[hh]] = x[tt - SUBLANES:]

    beta, g_col, sums = [], [], []
    for hh in heads:
        h = hg * hp + hh
        a_col = jnp.sum(jnp.where(lane == h, ab, 0.0), axis=1, keepdims=True)
        b_col = jnp.sum(jnp.where(lane == h + n_heads, ab, 0.0), axis=1, keepdims=True)
        alog = jnp.full((tt, 1), alog_ref[h], F32)
        dtb = jnp.full((tt, 1), dtb_ref[h], F32)
        g_col.append(-jnp.exp(alog) * _softplus(a_col + dtb))
        beta.append(jax.nn.sigmoid(b_col))
        g_rep = jnp.broadcast_to(g_col[hh], (tt, HEAD_DIM))
        g0 = g_rep.astype(BF16)
        r1 = g_rep - g0.astype(F32)
        g1 = r1.astype(BF16)
        g2 = (r1 - g1.astype(F32)).astype(BF16)
        sums.append(_dot(sum_mat, g0) + (_dot(sum_mat, g1) + _dot(sum_mat, g2)))
    cg_rep = [s[:tt] for s in sums]
    gl_rep = [s[tt:] for s in sums]
    decay = []
    for hh in heads:
        cg_row = jnp.sum(jnp.where(incl_t, g_col[hh], 0.0), axis=0, keepdims=True)
        decay.append(jnp.exp(jnp.where(incl, cg_rep[hh][:, 0:1] - cg_row, NEG)))

    qn = [xq[hh] * lax.rsqrt(jnp.sum(xq[hh] * xq[hh], axis=-1, keepdims=True) + 1e-6) * (HEAD_DIM ** -0.5)
          for hh in heads]
    kn = [xk[hh] * lax.rsqrt(jnp.sum(xk[hh] * xk[hh], axis=-1, keepdims=True) + 1e-6) for hh in heads]
    kn_s = [_split(kn[hh]) for hh in heads]
    kk = [_dot3(kn_s[hh], kn_s[hh], nt=True) for hh in heads]
    lmat = [jnp.where(strict, beta[hh] * decay[hh] * kk[hh], 0.0) for hh in heads]

    xp = [-jnp.where(base, lmat[hh], 0.0) for hh in heads]
    tinv = [eye + xp[hh] for hh in heads]
    xp_s = [_split(xp[hh]) for hh in heads]
    for _ in range(base_bits - 1):
        xp_s = [_split(_dot3(xp_s[hh], xp_s[hh])) for hh in heads]
        tinv = [tinv[hh] + _dot3(_split(tinv[hh]), xp_s[hh]) for hh in heads]
    l_s = [_split(lmat[hh]) for hh in heads] if level_masks else None
    for mask in level_masks:
        t_s = [_split(tinv[hh]) for hh in heads]
        lt = [_split(_dot3(l_s[hh], t_s[hh])) for hh in heads]
        tinv = [tinv[hh] - jnp.where(mask, _dot3(t_s[hh], lt[hh]), 0.0) for hh in heads]

    eg = [jnp.exp(cg_rep[hh]) for hh in heads]
    wu = [_dot3(_split(tinv[hh]),
                _split(jnp.concatenate([kn[hh] * (beta[hh] * eg[hh]), xv[hh] * beta[hh]], axis=1)))
          for hh in heads]
    u0 = [wu[hh][:, HEAD_DIM:] for hh in heads]
    amat = [jnp.where(incl, _dot_nt(qn[hh].astype(BF16), kn_s[hh][0]) * decay[hh], 0.0).astype(BF16)
            for hh in heads]
    wq = [[jnp.concatenate([wu[hh][cs, :HEAD_DIM], (qn[hh] * eg[hh])[cs]], axis=0).astype(BF16) for cs in chunks]
          for hh in heads]
    kd = [kn[hh] * jnp.exp(gl_rep[hh] - cg_rep[hh]) for hh in heads]
    egl = [jnp.exp(gl_rep[hh]) for hh in heads]

    if sample:
        o = []
        for hh in heads:
            rs = [_dot(wq[hh][c], s0_ref[c, hh].astype(BF16)) for c in range(n_chunks)]
            u = u0[hh] - jnp.concatenate([r[:chunk] for r in rs], axis=0)
            o.append(jnp.concatenate([r[chunk:] for r in rs], axis=0) + _dot(amat[hh], u.astype(BF16)))
            for c, cs in enumerate(chunks):
                so_ref[c, hh] = (egl[hh][c * chunk:c * chunk + 1, :] * s0_ref[c, hh]
                                 + _dot_tn(kd[hh][cs], u[cs], HI))
    else:
        kd_t = [kd[hh].T.astype(BF16) for hh in heads]
        s = [s_scr[hh] for hh in heads]
        outs = [[] for _ in heads]
        for c, cs in enumerate(chunks):
            for hh in heads:
                r = _dot(wq[hh][c], s[hh].astype(BF16))
                u = (u0[hh][cs] - r[:chunk]).astype(BF16)
                pieces = [u]
                if c > 0:
                    pieces.insert(0, jnp.zeros((c * chunk, HEAD_DIM), BF16))
                if c < n_chunks - 1:
                    pieces.append(jnp.zeros((tt - (c + 1) * chunk, HEAD_DIM), BF16))
                u_full = jnp.concatenate(pieces, axis=0)
                outs[hh].append(r[chunk:] + _dot(amat[hh][cs, :], u_full))
                s[hh] = egl[hh][c * chunk:c * chunk + 1, :] * s[hh] + _dot(kd_t[hh], u_full)
        for hh in heads:
            s_scr[hh] = s[hh]
            so_ref[0, hh] = s[hh]
        o = [jnp.concatenate(outs[hh], axis=0) for hh in heads]

    for hh in heads:
        z = z_ref[0, :, cols[hh]]
        on = (o[hh] * lax.rsqrt(jnp.mean(o[hh] * o[hh], axis=-1, keepdims=True) + EPS) * nw_ref[...] * _silu(z))
        o_ref[0, :, cols[hh]] = on.astype(o_ref.dtype)


def _gdn(gqkv, z, ab, conv_w, prev, a_log, dt_bias, norm_w, s0, *, sample, hp):
    g, r, w3 = gqkv.shape
    w = w3 // 3
    nh = w // HEAD_DIM
    tt = 256
    assert r % tt == 0 and nh % hp == 0
    ng = nh // hp
    wb = hp * HEAD_DIM
    smem = pl.BlockSpec(memory_space=pltpu.SMEM)
    if sample:
        assert r == tt
        chunk = SUBLANES
        seqs = tt // chunk
        grid = (g, ng)
        col = lambda off: pl.BlockSpec((1, tt, wb), lambda gi, h, off=off: (gi, 0, off + h))
        prev_spec = lambda off: pl.BlockSpec((GDN_CONV - 1, 1, tt, wb), lambda gi, h, off=off: (0, gi, 0, off + h))
        state = pl.BlockSpec((seqs, hp, HEAD_DIM, HEAD_DIM), lambda gi, h: (gi, h, 0, 0))
        cw = lambda off: pl.BlockSpec((GDN_CONV, wb), lambda gi, h, off=off: (0, off + h))
        row = pl.BlockSpec((1, HEAD_DIM), lambda gi, h: (0, 0))
        abs_ = pl.BlockSpec((1, tt, LANES), lambda gi, h: (gi, 0, 0))
        scratch = []
        sem = ("parallel", "parallel")
    else:
        chunk = GDN_CHUNK
        grid = (g, ng, r // tt)
        col = lambda off: pl.BlockSpec((1, tt, wb), lambda gi, h, i, off=off: (gi, i, off + h))
        prev_spec = lambda off: pl.BlockSpec((1, SUBLANES, wb), lambda gi, h, i, off=off: (gi, 0, off + h))
        state = pl.BlockSpec((1, hp, HEAD_DIM, HEAD_DIM), lambda gi, h, i: (gi, h, 0, 0))
        cw = lambda off: pl.BlockSpec((GDN_CONV, wb), lambda gi, h, i, off=off: (0, off + h))
        row = pl.BlockSpec((1, HEAD_DIM), lambda gi, h, i: (0, 0))
        abs_ = pl.BlockSpec((1, tt, LANES), lambda gi, h, i: (gi, i, 0))
        scratch = [pltpu.VMEM((hp, HEAD_DIM, HEAD_DIM), F32), pltpu.VMEM((3, SUBLANES, wb), F32)]
        sem = ("parallel", "parallel", "arbitrary")
    return pl.pallas_call(
        functools.partial(_gdn_kernel, tt=tt, chunk=chunk, sample=sample, n_heads=nh, hp=hp),
        grid=grid,
        in_specs=[smem, smem, col(0), col(ng), col(2 * ng), col(0), abs_, cw(0), cw(ng), cw(2 * ng),
                  prev_spec(0), prev_spec(ng), prev_spec(2 * ng), row, state],
        out_specs=[col(0), state],
        out_shape=[jax.ShapeDtypeStruct((g, r, w), BF16), jax.ShapeDtypeStruct(s0.shape, F32)],
        scratch_shapes=scratch,
        compiler_params=_params(*sem),
        name="gdn_sample" if sample else "gdn_prompt",
    )(a_log, dt_bias, gqkv, gqkv, gqkv, z, ab, conv_w, conv_w, conv_w, prev, prev, prev, norm_w, s0)


def _outproj_kernel(x_ref, att_ref, gdn_ref, wa_ref, wb_ref, o_ref):
    o_ref[...] = (x_ref[...] + _dot(att_ref[...].astype(BF16), wa_ref[...])
                  + _dot(gdn_ref[...].astype(BF16), wb_ref[...]))


def _outproj(x2d, att, gdn, w_att, w_gdn, *, tm):
    t, d = x2d.shape
    wa = att.shape[1]
    wg = gdn.shape[1]
    return pl.pallas_call(
        _outproj_kernel,
        grid=(t // tm,),
        in_specs=[pl.BlockSpec((tm, d), lambda i: (i, 0)), pl.BlockSpec((tm, wa), lambda i: (i, 0)),
                  pl.BlockSpec((tm, wg), lambda i: (i, 0)), pl.BlockSpec((wa, d), lambda i: (0, 0)),
                  pl.BlockSpec((wg, d), lambda i: (0, 0))],
        out_specs=pl.BlockSpec((tm, d), lambda i: (i, 0)),
        out_shape=jax.ShapeDtypeStruct((t, d), F32),
        compiler_params=_params("parallel"),
        name="outproj",
    )(x2d, att, gdn, w_att, w_gdn)


def _ffn_up_kernel(x_ref, g_ref, wg_ref, wv_ref, cwg_ref, cwv_ref, cbg_ref, cbv_ref, *rest,
                   tm, tf, sample, tiles_per_seq):
    if sample:
        pg_ref, pv_ref, act_ref, ug_ref, uv_ref, h_ref = rest
    else:
        act_ref, ug_ref, uv_ref, h_ref, carry_g, carry_v, buf_g, buf_v = rest
    i = pl.program_id(0)
    f = pl.program_id(1)

    @pl.when(f == 0)
    def _():
        def store(sl, y):
            h_ref[sl, :] = y.astype(BF16)
        _rmsnorm_rows(lambda sl: x_ref[sl, :], g_ref[...], store, tm)

    hb = h_ref[...]
    if sample:
        rmod = lax.broadcasted_iota(jnp.int32, (tm, tf), 0) & (SUBLANES - 1)

    def conv(w_ref, cw_ref, cb_ref, prev_ref, carry_ref, buf_ref, out_ref):
        cw = cw_ref[...]
        if sample:
            u = _dot(hb, w_ref[...])
            out_ref[...] = u
            shifted = [jnp.where(rmod >= shift, pltpu.roll(u, shift, axis=0), prev_ref[shift - 1])
                       for shift in range(1, FFN_CONV)]
            y = shifted[1] * cw[0:1, :] + shifted[0] * cw[1:2, :] + u * cw[2:3, :]
        else:
            buf_ref[0:SUBLANES, :] = jnp.where(i % tiles_per_seq == 0, 0.0, carry_ref[f])
            buf_ref[SUBLANES:, :] = _dot(hb, w_ref[...])
            last = buf_ref[tm:, :]
            carry_ref[f] = last
            out_ref[0] = last
            y = (buf_ref[SUBLANES - 2:SUBLANES - 2 + tm, :] * cw[0:1, :]
                 + buf_ref[SUBLANES - 1:SUBLANES - 1 + tm, :] * cw[1:2, :]
                 + buf_ref[SUBLANES:, :] * cw[2:3, :])
        return y + cb_ref[...]

    if sample:
        yg = conv(wg_ref, cwg_ref, cbg_ref, pg_ref, None, None, ug_ref)
        yv = conv(wv_ref, cwv_ref, cbv_ref, pv_ref, None, None, uv_ref)
    else:
        yg = conv(wg_ref, cwg_ref, cbg_ref, None, carry_g, buf_g, ug_ref)
        yv = conv(wv_ref, cwv_ref, cbv_ref, None, carry_v, buf_v, uv_ref)
    act_ref[...] = (_silu(yg) * yv).astype(BF16)


def _ffn_up(x2d, g, w_up, conv_w, conv_b, prev, *, tm, seq_rows, sample):
    t, d = x2d.shape
    dff = w_up.shape[1] // 2
    tf = 512
    nf = dff // tf
    assert dff % tf == 0 and t % tm == 0
    gate = lambda shape: pl.BlockSpec(shape, lambda i, f: (0, f))
    val = lambda shape: pl.BlockSpec(shape, lambda i, f: (0, nf + f))
    in_specs = [pl.BlockSpec((tm, d), lambda i, f: (i, 0)), pl.BlockSpec((1, d), lambda i, f: (0, 0)),
                gate((d, tf)), val((d, tf)), gate((FFN_CONV, tf)), val((FFN_CONV, tf)),
                gate((1, tf)), val((1, tf))]
    args = [x2d, g, w_up, w_up, conv_w, conv_w, conv_b, conv_b]
    scratch = [pltpu.VMEM((tm, d), BF16)]
    if sample:
        tiles_per_seq = 0
        in_specs += [pl.BlockSpec((FFN_CONV - 1, tm, tf), lambda i, f: (0, i, f)),
                     pl.BlockSpec((FFN_CONV - 1, tm, tf), lambda i, f: (0, i, nf + f))]
        args += [prev, prev]
        u_spec = pl.BlockSpec((tm, tf), lambda i, f: (i, f))
        u_shape = jax.ShapeDtypeStruct((t, dff), F32)
    else:
        assert seq_rows % tm == 0
        tiles_per_seq = seq_rows // tm
        u_spec = pl.BlockSpec((1, SUBLANES, tf), lambda i, f: (i, 0, f))
        u_shape = jax.ShapeDtypeStruct((t // tm, SUBLANES, dff), F32)
        scratch += [pltpu.VMEM((nf, SUBLANES, tf), F32)] * 2 + [pltpu.VMEM((SUBLANES + tm, tf), F32)] * 2
    return pl.pallas_call(
        functools.partial(_ffn_up_kernel, tm=tm, tf=tf, sample=sample, tiles_per_seq=tiles_per_seq),
        grid=(t // tm, nf),
        in_specs=in_specs,
        out_specs=[pl.BlockSpec((tm, tf), lambda i, f: (i, f)), u_spec, u_spec],
        out_shape=[jax.ShapeDtypeStruct((t, dff), BF16), u_shape, u_shape],
        scratch_shapes=scratch,
        compiler_params=_params("arbitrary", "arbitrary"),
        name="ffn_up_sample" if sample else "ffn_up_prompt",
    )(*args)


def _ffn_down_kernel(x_ref, act_ref, wd_ref, gf_ref, y_ref, xo_ref, *, tm, tn, nn, final_norm):
    n = pl.program_id(1)
    cols = pl.ds(pl.multiple_of(n * tn, tn), tn)
    xo_ref[:, cols] = x_ref[...] + _dot(act_ref[...], wd_ref[...])

    @pl.when(n == nn - 1)
    def _():
        if final_norm:
            def store(sl, y):
                y_ref[sl, :] = y
            _rmsnorm_rows(lambda sl: xo_ref[sl, :], gf_ref[...], store, tm)
        else:
            y_ref[...] = xo_ref[...]


def _ffn_down(x2d, act, w_down, gf, *, tm, final_norm):
    t, d = x2d.shape
    dff = w_down.shape[0]
    tn = 512
    nn = d // tn
    assert d % tn == 0 and t % tm == 0
    return pl.pallas_call(
        functools.partial(_ffn_down_kernel, tm=tm, tn=tn, nn=nn, final_norm=final_norm),
        grid=(t // tm, nn),
        in_specs=[pl.BlockSpec((tm, tn), lambda i, n: (i, n)), pl.BlockSpec((tm, dff), lambda i, n: (i, 0)),
                  pl.BlockSpec((dff, tn), lambda i, n: (0, n)), pl.BlockSpec((1, d), lambda i, n: (0, 0))],
        out_specs=pl.BlockSpec((tm, d), lambda i, n: (i, 0)),
        out_shape=jax.ShapeDtypeStruct((t, d), F32),
        scratch_shapes=[pltpu.VMEM((tm, d), F32)],
        compiler_params=_params("parallel", "arbitrary"),
        name="ffn_down",
    )(x2d, act, w_down, gf)


def _layer_slice(x, l):
    return x.reshape(x.shape[1:]) if x.shape[0] == 1 else x[l]


def _spread_prev(state, n_shift, t_new):
    db, wm1, c = state.shape
    out = []
    for k in range(1, n_shift + 1):
        rows = [state[:, wm1 - k + r] for r in range(k)]
        rows += [jnp.zeros((db, c), state.dtype)] * (t_new - k)
        out.append(jnp.stack(rows, axis=1).reshape(db * t_new, c))
    return jnp.stack(out, axis=0)


def kernel(x_prompt, x_sample, cache_k, cache_v, page_table, state_gdn, state_gdn_conv, state_ffn_conv,
           rms_mix_w, w_in, gdn_conv_w, gdn_a_log, gdn_dt_bias, gdn_norm_w, w_out, rms_ffn_w, w_up,
           ffn_conv_w, ffn_conv_b, w_down, rel_bias, final_norm_w):
    b, s, d = x_prompt.shape
    db, t_new, _ = x_sample.shape
    depth = w_in.shape[0]
    n_att = cache_k.shape[3]
    w_att = n_att * HEAD_DIM
    n_gdn = gdn_a_log.shape[1]
    w_gdn = n_gdn * HEAD_DIM
    dff = w_down.shape[1]
    assert w_att == w_gdn == 1024 and w_in.shape[2] == 3 * w_att + 4 * w_gdn + 2 * n_gdn
    n_main = 3 * w_att + 4 * w_gdn
    n_pool = cache_k.shape[1]
    grp = 256 // t_new
    tm = 512

    bias_tiles, bias_far = _bias_prompt(rel_bias, MOBA_BLOCK)
    blast, bown, bfar = _bias_sample(rel_bias, MOBA_BLOCK, t_new)

    xp = x_prompt.reshape(b * s, d)
    xs = x_sample.reshape(db * t_new, d)
    outs_p = [[] for _ in range(5)]
    outs_s = [[] for _ in range(5)]
    for l in range(depth):
        w_in_b = w_in[l].astype(BF16)
        w_main = w_in_b[:, :n_main]
        w_ab = jnp.pad(w_in_b[:, n_main:], ((0, 0), (0, LANES - 2 * n_gdn)))
        w_out_b = w_out[l].astype(BF16)
        w_up_b = w_up[l].astype(BF16)
        w_down_b = w_down[l].astype(BF16)
        g_mix = rms_mix_w[l].reshape(1, d)
        g_ffn = rms_ffn_w[l].reshape(1, d)
        g_fin = final_norm_w.reshape(1, d)
        norm_w = gdn_norm_w[l].reshape(1, HEAD_DIM)
        conv_b = ffn_conv_b[l].reshape(1, 2 * dff)
        last = l == depth - 1

        q, k, v, gq, z, ab = _inproj(xp, g_mix, w_main, w_ab, tm=tm)
        att = _moba_prompt(q.reshape(b, s, w_att), k.reshape(b, s, w_att), v.reshape(b, s, w_att),
                           bias_tiles, bias_far)
        gdn, s_new = _gdn(gq.reshape(b, s, 3 * w_gdn), z.reshape(b, s, w_gdn), ab.reshape(b, s, LANES),
                          gdn_conv_w[l], jnp.zeros((b, SUBLANES, 3 * w_gdn), F32), gdn_a_log[l], gdn_dt_bias[l],
                          norm_w, jnp.zeros((b, n_gdn, HEAD_DIM, HEAD_DIM), F32), sample=False, hp=8)
        x1 = _outproj(xp, att.reshape(b * s, w_att), gdn.reshape(b * s, w_gdn), w_out_b[:w_att], w_out_b[w_att:],
                      tm=tm)
        act, ug, uv = _ffn_up(x1, g_ffn, w_up_b, ffn_conv_w[l], conv_b, None, tm=2 * tm, seq_rows=s, sample=False)
        y = _ffn_down(x1, act, w_down_b, g_fin, tm=tm, final_norm=last)
        outs_p[0].append(k.reshape(b, s, n_att, HEAD_DIM))
        outs_p[1].append(v.reshape(b, s, n_att, HEAD_DIM))
        outs_p[2].append(s_new)
        outs_p[3].append(gq.reshape(b, s, 3 * w_gdn)[:, s - (GDN_CONV - 1):])
        tps = s // (2 * tm)
        outs_p[4].append(jnp.concatenate([ug, uv], axis=-1)[tps - 1::tps, SUBLANES - (FFN_CONV - 1):])
        xp = y

        ts = db * t_new
        q, k, v, gq, z, ab = _inproj(xs, g_mix, w_main, w_ab, tm=tm)
        att = _moba_sample(q, k, v, cache_k.reshape((depth * n_pool,) + cache_k.shape[2:]),
                           cache_v.reshape((depth * n_pool,) + cache_v.shape[2:]), page_table + l * n_pool,
                           blast, bown, bfar, t_new=t_new)
        prev_g = _spread_prev(state_gdn_conv[l], GDN_CONV - 1, t_new).reshape(GDN_CONV - 1, db // grp, 256, 3 * w_gdn)
        gdn, s_new = _gdn(gq.reshape(db // grp, 256, 3 * w_gdn), z.reshape(db // grp, 256, w_gdn),
                          ab.reshape(db // grp, 256, LANES), gdn_conv_w[l], prev_g, gdn_a_log[l], gdn_dt_bias[l],
                          norm_w, _layer_slice(state_gdn, l), sample=True, hp=2)
        x1 = _outproj(xs, att, gdn.reshape(ts, w_gdn), w_out_b[:w_att], w_out_b[w_att:], tm=tm)
        prev_f = _spread_prev(state_ffn_conv[l], FFN_CONV - 1, t_new)
        act, ug, uv = _ffn_up(x1, g_ffn, w_up_b, ffn_conv_w[l], conv_b, prev_f, tm=tm, seq_rows=t_new, sample=True)
        y = _ffn_down(x1, act, w_down_b, g_fin, tm=tm, final_norm=last)
        outs_s[0].append(k.reshape(db, t_new, n_att, HEAD_DIM))
        outs_s[1].append(v.reshape(db, t_new, n_att, HEAD_DIM))
        outs_s[2].append(s_new)
        outs_s[3].append(gq.reshape(db, t_new, 3 * w_gdn)[:, t_new - (GDN_CONV - 1):])
        outs_s[4].append(jnp.concatenate([ug, uv], axis=-1).reshape(db, t_new, 2 * dff)[:, t_new - (FFN_CONV - 1):])
        xs = y

    return (xp.reshape(b, s, d), xs.reshape(db, t_new, d),
            *[jnp.stack(o) for o in outs_p], *[jnp.stack(o) for o in outs_s])
```

```python
import functools
import math

import jax
import jax.numpy as jnp
import numpy as np
from jax import lax
from jax.experimental import pallas as pl
from jax.experimental.pallas import tpu as pltpu

F32 = jnp.float32
BF16 = jnp.bfloat16
HI = lax.Precision.HIGHEST

HEAD_DIM = 128
MOBA_BLOCK = 256
MOBA_TOP_K = 3
PAGE_SIZE = 128
GDN_CONV = 4
GDN_CHUNK = 64
FFN_CONV = 3
NUM_BUCKETS = 32
MAX_DISTANCE = 128
EPS = 1e-6
NEG = -1e30

LANES = 128
SUBLANES = 8
VMEM_LIMIT = 56 * 1024 * 1024


def _bucket_thresholds():
    n = np.arange(0, 4 * MAX_DISTANCE)
    max_exact = NUM_BUCKETS // 2
    nf = np.maximum(n, 1).astype(np.float32)
    large = max_exact + (np.log(nf / np.float32(max_exact)) / np.float32(math.log(MAX_DISTANCE / max_exact))
                         * np.float32(NUM_BUCKETS - max_exact)).astype(np.int32)
    large = np.minimum(large, NUM_BUCKETS - 1)
    b = np.where(n < max_exact, n, large)
    return [int(np.argmax(b >= k)) for k in range(1, NUM_BUCKETS)]


BUCKET_THR = _bucket_thresholds()


def _dot(a, b, precision=None):
    return jnp.dot(a, b, preferred_element_type=F32, precision=precision)


def _dot_nt(a, b, precision=None):
    return lax.dot_general(a, b, (((1,), (1,)), ((), ())), preferred_element_type=F32, precision=precision)


def _dot_tn(a, b, precision=None):
    return lax.dot_general(a, b, (((0,), (0,)), ((), ())), preferred_element_type=F32, precision=precision)


def _silu(x):
    return x * jax.nn.sigmoid(x)


def _softplus(x):
    return jnp.maximum(x, 0.0) + jnp.log1p(jnp.exp(-jnp.abs(x)))


def _params(*semantics):
    return pltpu.CompilerParams(dimension_semantics=semantics, vmem_limit_bytes=VMEM_LIMIT)


def _rmsnorm_rows(load, g, store, rows, chunk=256):
    def body(r, carry):
        sl = pl.ds(pl.multiple_of(r * chunk, chunk), chunk)
        x = load(sl)
        ms = jnp.mean(x * x, axis=-1, keepdims=True)
        store(sl, x * lax.rsqrt(ms + EPS) * g)
        return carry
    lax.fori_loop(0, rows // chunk, body, 0)


def _inproj_kernel(x_ref, g_ref, w_ref, wab_ref, *rest, tm, ranges):
    out_refs, ab_ref, h_ref = rest[:len(ranges)], rest[len(ranges)], rest[len(ranges) + 1]
    j = pl.program_id(1)

    @pl.when(j == 0)
    def _():
        def store(sl, y):
            h_ref[sl, :] = y.astype(BF16)
        _rmsnorm_rows(lambda sl: x_ref[sl, :], g_ref[...], store, tm)
        ab_ref[...] = _dot(h_ref[...], wab_ref[...])

    for (lo, hi), ref in zip(ranges, out_refs):
        @pl.when((j >= lo) & (j < hi))
        def _(ref=ref):
            ref[...] = _dot(h_ref[...], w_ref[...])


def _inproj(x2d, g, w_main, w_ab, widths, *, tm, tn):
    t, d = x2d.shape
    assert t % tm == 0 and all(w % tn == 0 for w in widths) and sum(widths) == w_main.shape[1]
    ranges, lo = [], 0
    for w in widths:
        ranges.append((lo, lo + w // tn))
        lo += w // tn
    out_spec = lambda lo, hi: pl.BlockSpec((tm, tn), lambda i, j: (i, jnp.clip(j - lo, 0, hi - lo - 1)))
    return pl.pallas_call(
        functools.partial(_inproj_kernel, tm=tm, ranges=tuple(ranges)),
        grid=(t // tm, lo),
        in_specs=[pl.BlockSpec((tm, d), lambda i, j: (i, 0)),
                  pl.BlockSpec((1, d), lambda i, j: (0, 0)),
                  pl.BlockSpec((d, tn), lambda i, j: (0, j)),
                  pl.BlockSpec((d, LANES), lambda i, j: (0, 0))],
        out_specs=[out_spec(*r) for r in ranges] + [pl.BlockSpec((tm, LANES), lambda i, j: (i, 0))],
        out_shape=[jax.ShapeDtypeStruct((t, w), F32) for w in widths] + [jax.ShapeDtypeStruct((t, LANES), F32)],
        scratch_shapes=[pltpu.VMEM((tm, d), BF16)],
        compiler_params=_params("parallel", "arbitrary"),
        name="inproj",
    )(x2d, g, w_main, w_ab)


def _bias_chain(dist, value_of_bucket):
    val = value_of_bucket(0)
    for k in range(1, NUM_BUCKETS):
        val = jnp.where(dist >= BUCKET_THR[k - 1], value_of_bucket(k), val)
    return val


def _bias_prompt_kernel(rb_ref, tiles_ref, far_ref, *, blk):
    h = pl.program_id(0)
    qi = lax.broadcasted_iota(jnp.int32, (blk, blk), 0)
    kj = lax.broadcasted_iota(jnp.int32, (blk, blk), 1)
    for delta in range(2):
        dist = delta * blk + qi - kj
        val = _bias_chain(dist, lambda k: rb_ref[k, h])
        if delta == 0:
            val = jnp.where(dist >= 0, val, NEG)
        tiles_ref[0, delta] = val
    far_ref[0] = jnp.full((SUBLANES, blk), rb_ref[NUM_BUCKETS - 1, h], F32)


def _bias_prompt(rel_bias, blk):
    nh = rel_bias.shape[1]
    assert blk + 1 >= BUCKET_THR[-1]
    return pl.pallas_call(
        functools.partial(_bias_prompt_kernel, blk=blk),
        grid=(nh,),
        in_specs=[pl.BlockSpec(memory_space=pltpu.SMEM)],
        out_specs=[pl.BlockSpec((1, 2, blk, blk), lambda h: (h, 0, 0, 0)),
                   pl.BlockSpec((1, SUBLANES, blk), lambda h: (h, 0, 0))],
        out_shape=[jax.ShapeDtypeStruct((nh, 2, blk, blk), F32), jax.ShapeDtypeStruct((nh, SUBLANES, blk), F32)],
        compiler_params=_params("arbitrary"),
        name="bias_prompt",
    )(rel_bias)


def _bias_sample_kernel(rbx_ref, last_ref, own_ref, far_ref, *, blk, t_new):
    tok = lax.broadcasted_iota(jnp.int32, (blk, LANES), 1) & (t_new - 1)
    pos = lax.broadcasted_iota(jnp.int32, (blk, LANES), 0)
    row = lambda k: rbx_ref[k:k + 1, :]
    last_ref[...] = _bias_chain(blk + tok - pos, row)
    tok = lax.broadcasted_iota(jnp.int32, (LANES, LANES), 1) & (t_new - 1)
    pos = lax.broadcasted_iota(jnp.int32, (LANES, LANES), 0)
    dist = tok - pos
    own_ref[...] = jnp.where((dist >= 0) & (pos < t_new), _bias_chain(dist, row), NEG)
    far_ref[...] = jnp.broadcast_to(row(NUM_BUCKETS - 1), (SUBLANES, LANES))


def _bias_sample(rel_bias, blk, t_new):
    nh = rel_bias.shape[1]
    assert nh * t_new <= LANES and blk >= LANES
    rbx = jnp.pad(jnp.repeat(rel_bias, t_new, axis=1), ((0, 0), (0, LANES - nh * t_new)))
    return pl.pallas_call(
        functools.partial(_bias_sample_kernel, blk=blk, t_new=t_new),
        out_shape=[jax.ShapeDtypeStruct((blk, LANES), F32), jax.ShapeDtypeStruct((LANES, LANES), F32),
                   jax.ShapeDtypeStruct((SUBLANES, LANES), F32)],
        name="bias_sample",
    )(rbx)


def _moba_prompt_kernel(q_ref, k_ref, v_ref, bias_ref, far_ref, o_ref, kb_ref, vb_ref, *, nb, blk, n_sel):
    scale = HEAD_DIM ** -0.5
    k = k_ref[0]
    kb_ref[...] = k.astype(BF16)
    vb_ref[...] = v_ref[0].astype(BF16)
    means = jnp.sum(k.reshape(nb, blk, HEAD_DIM), axis=1) * (1.0 / blk)
    far = far_ref[0, 0:1, :]
    jidx = lax.broadcasted_iota(jnp.int32, (blk, nb), 1)
    for c in range(nb):
        rows = slice(c * blk, (c + 1) * blk)
        q = q_ref[0, rows, :]
        qb = (q * scale).astype(BF16)
        rank = None
        if c > n_sel:
            sc = jnp.where(jidx < c, _dot_nt(q, means, HI), NEG)
            rank = jnp.zeros((blk, nb), jnp.int32)
            for i in range(c):
                ci = sc[:, i:i + 1]
                tie = (jidx > i).astype(jnp.int32)
                rank = rank + jnp.where(ci > sc, 1, jnp.where(ci == sc, tie, 0))
        parts = []
        for j in range(c):
            sj = _dot_nt(qb, kb_ref[j * blk:(j + 1) * blk, :])
            sj = sj + (bias_ref[0, 1] if j == c - 1 else far)
            if rank is not None:
                sj = jnp.where(rank[:, j:j + 1] < n_sel, sj, NEG)
            parts.append(sj)
        parts.append(_dot_nt(qb, kb_ref[rows, :]) + bias_ref[0, 0])
        mm = parts[0]
        for p in parts[1:]:
            mm = jnp.maximum(mm, p)
        m = jnp.max(mm, axis=1, keepdims=True)
        ps = [jnp.exp(p - m) for p in parts]
        tot = ps[0]
        for p in ps[1:]:
            tot = tot + p
        l = jnp.sum(tot, axis=1, keepdims=True)
        pcat = jnp.concatenate([p.astype(BF16) for p in ps], axis=1)
        o = _dot(pcat, vb_ref[0:(c + 1) * blk, :])
        o_ref[0, rows, :] = (o / l).astype(o_ref.dtype)


def _moba_prompt(q, k, v, bias_tiles, bias_far):
    b, s, w = q.shape
    nh = w // HEAD_DIM
    blk = MOBA_BLOCK
    assert s % blk == 0
    nb = s // blk
    n_sel = min(MOBA_TOP_K, nb - 1)
    col = pl.BlockSpec((1, s, HEAD_DIM), lambda bi, h: (bi, 0, h))
    return pl.pallas_call(
        functools.partial(_moba_prompt_kernel, nb=nb, blk=blk, n_sel=n_sel),
        grid=(b, nh),
        in_specs=[col, col, col,
                  pl.BlockSpec((1, 2, blk, blk), lambda bi, h: (h, 0, 0, 0)),
                  pl.BlockSpec((1, SUBLANES, blk), lambda bi, h: (h, 0, 0))],
        out_specs=col,
        out_shape=jax.ShapeDtypeStruct((b, s, w), BF16),
        scratch_shapes=[pltpu.VMEM((s, HEAD_DIM), BF16), pltpu.VMEM((s, HEAD_DIM), BF16)],
        compiler_params=_params("parallel", "parallel"),
        name="moba_prompt",
    )(q, k, v, bias_tiles, bias_far)


def _moba_sample_kernel(pt_ref, q_ref, kn_ref, vn_ref, blast_ref, bown_ref, bfar_ref, *rest,
                        n_pages, ppb, n_sel):
    del pt_ref
    kps, vps, o_ref = rest[:n_pages], rest[n_pages:2 * n_pages], rest[2 * n_pages]
    scale = HEAD_DIM ** -0.5
    t_new, width = q_ref.shape
    nh = width // HEAD_DIM
    cur = n_pages // ppb
    blk = ppb * PAGE_SIZE
    rid = lax.broadcasted_iota(jnp.int32, (LANES, width), 0)
    cid = lax.broadcasted_iota(jnp.int32, (LANES, width), 1)
    q_rep = jnp.concatenate([q_ref[...]] * (LANES // t_new), axis=0)
    qt = jnp.where((rid >> int(math.log2(t_new))) == (cid >> int(math.log2(HEAD_DIM))), q_rep, 0.0)
    qt_s = (qt * scale).astype(BF16)

    def page_matrix(ref):
        return jnp.concatenate([ref[0, pl.ds(h, PAGE_SIZE, stride=nh), :] for h in range(nh)], axis=1)

    logits, ksums = [], []
    for j in range(n_pages):
        kj = page_matrix(kps[j])
        logits.append(_dot_nt(kj.astype(BF16), qt_s))
        ksums.append(jnp.sum(kj, axis=0, keepdims=True))
    means = []
    for jb in range(cur):
        tot = ksums[jb * ppb]
        for r in range(1, ppb):
            tot = tot + ksums[jb * ppb + r]
        means.append(tot * (1.0 / blk))
    sc = _dot_nt(jnp.concatenate(means, axis=0), qt, HI)
    bidx = lax.broadcasted_iota(jnp.int32, (cur, LANES), 0)
    rank = jnp.zeros((cur, LANES), jnp.int32)
    for i in range(cur):
        ri = sc[i:i + 1, :]
        tie = (bidx > i).astype(jnp.int32)
        rank = rank + jnp.where(ri > sc, 1, jnp.where(ri == sc, tie, 0))

    parts = []
    for jb in range(cur):
        sb = jnp.concatenate(logits[jb * ppb:(jb + 1) * ppb], axis=0)
        sb = sb + (blast_ref[...] if jb == cur - 1 else bfar_ref[0:1, :])
        parts.append(jnp.where(rank[jb:jb + 1, :] < n_sel, sb, NEG))
    pad = jnp.zeros((LANES - t_new, width), F32)
    kn = jnp.concatenate([kn_ref[...], pad], axis=0)
    vn = jnp.concatenate([vn_ref[...], pad], axis=0)
    parts.append(_dot_nt(kn.astype(BF16), qt_s) + bown_ref[...])
    mm = parts[0]
    for p in parts[1:-1]:
        mm = jnp.maximum(mm, p)
    m = jnp.maximum(jnp.max(mm, axis=0, keepdims=True), jnp.max(parts[-1], axis=0, keepdims=True))
    acc = jnp.zeros((LANES, width), F32)
    lsum = jnp.zeros((LANES, 1), F32)
    for jb in range(cur + 1):
        pt = jnp.exp(parts[jb] - m).T
        lsum = lsum + jnp.sum(pt, axis=1, keepdims=True)
        if jb < cur:
            vb = jnp.concatenate([page_matrix(vps[jb * ppb + r]).astype(BF16) for r in range(ppb)], axis=0)
        else:
            vb = vn.astype(BF16)
        acc = acc + _dot(pt.astype(BF16), vb)
    out = acc / lsum
    for h in range(nh):
        o_ref[:, h * HEAD_DIM:(h + 1) * HEAD_DIM] = out[h * t_new:(h + 1) * t_new, h * HEAD_DIM:(h + 1) * HEAD_DIM]


def _moba_sample(q, k_new, v_new, cache_k, cache_v, page_table, blast, bown, bfar, *, t_new):
    rows, width = q.shape
    db = rows // t_new
    n_pool = cache_k.shape[0]
    n_pages = page_table.shape[1]
    ppb = MOBA_BLOCK // PAGE_SIZE
    assert n_pages % ppb == 0, "new tokens must start a fresh MoBA block"
    assert t_new == SUBLANES
    cur = n_pages // ppb
    n_sel = min(MOBA_TOP_K, cur)
    assert n_sel > 0
    nh = width // HEAD_DIM
    ck = cache_k.reshape(n_pool, PAGE_SIZE * nh, HEAD_DIM)
    cv = cache_v.reshape(n_pool, PAGE_SIZE * nh, HEAD_DIM)
    new = pl.BlockSpec((t_new, width), lambda s, pt: (s, 0))
    const = lambda shape: pl.BlockSpec(shape, lambda s, pt: (0, 0))
    page = lambda j: pl.BlockSpec((1, PAGE_SIZE * nh, HEAD_DIM), lambda s, pt, j=j: (pt[s, j], 0, 0))
    grid_spec = pltpu.PrefetchScalarGridSpec(
        num_scalar_prefetch=1,
        grid=(db,),
        in_specs=[new, new, new, const(blast.shape), const(bown.shape), const(bfar.shape)]
        + [page(j) for j in range(n_pages)] * 2,
        out_specs=new,
    )
    return pl.pallas_call(
        functools.partial(_moba_sample_kernel, n_pages=n_pages, ppb=ppb, n_sel=n_sel),
        grid_spec=grid_spec,
        out_shape=jax.ShapeDtypeStruct((rows, width), F32),
        compiler_params=_params("parallel"),
        name="moba_sample",
    )(page_table, q, k_new, v_new, blast, bown, bfar, *([ck] * n_pages), *([cv] * n_pages))


def _split(x):
    hi = x.astype(BF16)
    return hi, (x - hi.astype(F32)).astype(BF16)


def _dot3(a, b, nt=False):
    f = _dot_nt if nt else _dot
    return f(a[0], b[0]) + (f(a[0], b[1]) + f(a[1], b[0]))


def _gdn_kernel(alog_ref, dtb_ref, q_ref, k_ref, v_ref, z_ref, ab_ref, cwq_ref, cwk_ref, cwv_ref,
                pq_ref, pk_ref, pv_ref, nw_ref, s0_ref, o_ref, so_ref, *scratch, tt, chunk, sample, n_heads, hp):
    hg = pl.program_id(1)
    n_chunks = tt // chunk
    ins = [q_ref, k_ref, v_ref]
    cw_refs = [cwq_ref, cwk_ref, cwv_ref]
    prevs = [pq_ref, pk_ref, pv_ref]
    rid = lax.broadcasted_iota(jnp.int32, (tt, HEAD_DIM), 0)
    rid8 = lax.broadcasted_iota(jnp.int32, (SUBLANES, HEAD_DIM), 0)
    if sample:
        rmod = rid & (chunk - 1)
    else:
        s_scr, tail_scr = scratch
        i = pl.program_id(2)

        @pl.when(i == 0)
        def _():
            s_scr[...] = s0_ref[0]
            for n in range(3):
                tail_scr[n] = prevs[n][0]

    ri = lax.broadcasted_iota(jnp.int32, (tt, tt), 0)
    ci = lax.broadcasted_iota(jnp.int32, (tt, tt), 1)
    shift_c = int(math.log2(chunk))
    same_block = lambda bits: (ri >> bits) == (ci >> bits)
    same = same_block(shift_c)
    incl = same & (ri >= ci)
    strict = same & (ri > ci)
    incl_t = same & (ri <= ci)
    eye = jnp.where(ri == ci, 1.0, 0.0)
    base_bits = int(math.log2(SUBLANES))
    base = same_block(base_bits)
    level_masks = [same_block(bits + 1) & (ri >> bits != ci >> bits) for bits in range(base_bits, shift_c)]
    sum_mat = jnp.concatenate([jnp.where(incl, 1.0, 0.0), jnp.where(same, 1.0, 0.0)], axis=0).astype(BF16)
    lane = lax.broadcasted_iota(jnp.int32, (tt, LANES), 1)
    ab = ab_ref[0]
    heads = range(hp)
    cols = [slice(hh * HEAD_DIM, (hh + 1) * HEAD_DIM) for hh in heads]
    chunks = [slice(c * chunk, (c + 1) * chunk) for c in range(n_chunks)]


    xq, xk, xv = [], [], []
    for hh in heads:
        for n, dst in enumerate((xq, xk, xv)):
            x = ins[n][0, :, cols[hh]]
            w = cw_refs[n][:, cols[hh]]
            acc = None
            for shift in range(GDN_CONV - 1, 0, -1):
                rolled = pltpu.roll(x, shift, axis=0)
                if sample:
                    xm = jnp.where(rmod >= shift, rolled, prevs[n][shift - 1, 0, :, cols[hh]])
                else:
                    first = jnp.where(rid8 < shift, pltpu.roll(tail_scr[n, :, cols[hh]], shift, axis=0),
                                      rolled[:SUBLANES])
                    xm = jnp.concatenate([first, rolled[SUBLANES:]], axis=0)
                tap = GDN_CONV - 1 - shift
                term = xm * w[tap:tap + 1, :]
                acc = term if acc is None else acc + term
            acc = acc + x * w[GDN_CONV - 1:GDN_CONV, :]
            dst.append(_silu(acc))
            if not sample:
                tail_scr[n, :, cols[hh]] = x[tt - SUBLANES:]

    beta, g_col, sums = [], [], []
    for hh in heads:
        h = hg * hp + hh
        a_col = jnp.sum(jnp.where(lane == h, ab, 0.0), axis=1, keepdims=True)
        b_col = jnp.sum(jnp.where(lane == h + n_heads, ab, 0.0), axis=1, keepdims=True)
        alog = jnp.full((tt, 1), alog_ref[h], F32)
        dtb = jnp.full((tt, 1), dtb_ref[h], F32)
        g_col.append(-jnp.exp(alog) * _softplus(a_col + dtb))
        beta.append(jax.nn.sigmoid(b_col))
        g_rep = jnp.broadcast_to(g_col[hh], (tt, HEAD_DIM))
        g0 = g_rep.astype(BF16)
        r1 = g_rep - g0.astype(F32)
        g1 = r1.astype(BF16)
        g2 = (r1 - g1.astype(F32)).astype(BF16)
        sums.append(_dot(sum_mat, g0) + (_dot(sum_mat, g1) + _dot(sum_mat, g2)))
    cg_rep = [s[:tt] for s in sums]
    gl_rep = [s[tt:] for s in sums]
    decay = []
    for hh in heads:
        cg_row = jnp.sum(jnp.where(incl_t, g_col[hh], 0.0), axis=0, keepdims=True)
        decay.append(jnp.exp(jnp.where(incl, cg_rep[hh][:, 0:1] - cg_row, NEG)))

    qn = [xq[hh] * lax.rsqrt(jnp.sum(xq[hh] * xq[hh], axis=-1, keepdims=True) + 1e-6) * (HEAD_DIM ** -0.5)
          for hh in heads]
    kn = [xk[hh] * lax.rsqrt(jnp.sum(xk[hh] * xk[hh], axis=-1, keepdims=True) + 1e-6) for hh in heads]
    kn_s = [_split(kn[hh]) for hh in heads]
    kk = [_dot3(kn_s[hh], kn_s[hh], nt=True) for hh in heads]
    lmat = [jnp.where(strict, beta[hh] * decay[hh] * kk[hh], 0.0) for hh in heads]

    xp = [-jnp.where(base, lmat[hh], 0.0) for hh in heads]
    tinv = [eye + xp[hh] for hh in heads]
    xp_s = [_split(xp[hh]) for hh in heads]
    for _ in range(base_bits - 1):
        xp_s = [_split(_dot3(xp_s[hh], xp_s[hh])) for hh in heads]
        tinv = [tinv[hh] + _dot3(_split(tinv[hh]), xp_s[hh]) for hh in heads]
    l_s = [_split(lmat[hh]) for hh in heads] if level_masks else None
    for mask in level_masks:
        t_s = [_split(tinv[hh]) for hh in heads]
        lt = [_split(_dot3(l_s[hh], t_s[hh])) for hh in heads]
        tinv = [tinv[hh] - jnp.where(mask, _dot3(t_s[hh], lt[hh]), 0.0) for hh in heads]

    eg = [jnp.exp(cg_rep[hh]) for hh in heads]
    wu = [_dot3(_split(tinv[hh]),
                _split(jnp.concatenate([kn[hh] * (beta[hh] * eg[hh]), xv[hh] * beta[hh]], axis=1)))
          for hh in heads]
    u0 = [wu[hh][:, HEAD_DIM:] for hh in heads]
    amat = [jnp.where(incl, _dot_nt(qn[hh].astype(BF16), kn_s[hh][0]) * decay[hh], 0.0).astype(BF16)
            for hh in heads]
    wq = [[jnp.concatenate([wu[hh][cs, :HEAD_DIM], (qn[hh] * eg[hh])[cs]], axis=0).astype(BF16) for cs in chunks]
          for hh in heads]
    kd = [kn[hh] * jnp.exp(gl_rep[hh] - cg_rep[hh]) for hh in heads]
    egl = [jnp.exp(gl_rep[hh]) for hh in heads]

    if sample:
        o = []
        for hh in heads:
            rs = [_dot(wq[hh][c], s0_ref[c, hh].astype(BF16)) for c in range(n_chunks)]
            u = u0[hh] - jnp.concatenate([r[:chunk] for r in rs], axis=0)
            o.append(jnp.concatenate([r[chunk:] for r in rs], axis=0) + _dot(amat[hh], u.astype(BF16)))
            for c, cs in enumerate(chunks):
                so_ref[c, hh] = (egl[hh][c * chunk:c * chunk + 1, :] * s0_ref[c, hh]
                                 + _dot_tn(kd[hh][cs], u[cs], HI))
    else:
        kd_t = [kd[hh].T.astype(BF16) for hh in heads]
        s = [s_scr[hh] for hh in heads]
        outs = [[] for _ in heads]
        for c, cs in enumerate(chunks):
            for hh in heads:
                r = _dot(wq[hh][c], s[hh].astype(BF16))
                u = (u0[hh][cs] - r[:chunk]).astype(BF16)
                pieces = [u]
                if c > 0:
                    pieces.insert(0, jnp.zeros((c * chunk, HEAD_DIM), BF16))
                if c < n_chunks - 1:
                    pieces.append(jnp.zeros((tt - (c + 1) * chunk, HEAD_DIM), BF16))
                u_full = jnp.concatenate(pieces, axis=0)
                outs[hh].append(r[chunk:] + _dot(amat[hh][cs, :], u_full))
                s[hh] = egl[hh][c * chunk:c * chunk + 1, :] * s[hh] + _dot(kd_t[hh], u_full)
        for hh in heads:
            s_scr[hh] = s[hh]
            so_ref[0, hh] = s[hh]
        o = [jnp.concatenate(outs[hh], axis=0) for hh in heads]

    for hh in heads:
        z = z_ref[0, :, cols[hh]]
        on = (o[hh] * lax.rsqrt(jnp.mean(o[hh] * o[hh], axis=-1, keepdims=True) + EPS) * nw_ref[...] * _silu(z))
        o_ref[0, :, cols[hh]] = on.astype(o_ref.dtype)


def _gdn(gqkv, z, ab, conv_w, prev, a_log, dt_bias, norm_w, s0, *, sample, hp):
    g, r, w3 = gqkv.shape
    w = w3 // 3
    nh = w // HEAD_DIM
    tt = 256
    assert r % tt == 0 and nh % hp == 0
    ng = nh // hp
    wb = hp * HEAD_DIM
    smem = pl.BlockSpec(memory_space=pltpu.SMEM)
    if sample:
        assert r == tt
        chunk = SUBLANES
        seqs = tt // chunk
        grid = (g, ng)
        col = lambda off: pl.BlockSpec((1, tt, wb), lambda gi, h, off=off: (gi, 0, off + h))
        prev_spec = lambda off: pl.BlockSpec((GDN_CONV - 1, 1, tt, wb), lambda gi, h, off=off: (0, gi, 0, off + h))
        state = pl.BlockSpec((seqs, hp, HEAD_DIM, HEAD_DIM), lambda gi, h: (gi, h, 0, 0))
        cw = lambda off: pl.BlockSpec((GDN_CONV, wb), lambda gi, h, off=off: (0, off + h))
        row = pl.BlockSpec((1, HEAD_DIM), lambda gi, h: (0, 0))
        abs_ = pl.BlockSpec((1, tt, LANES), lambda gi, h: (gi, 0, 0))
        scratch = []
        sem = ("parallel", "parallel")
    else:
        chunk = GDN_CHUNK
        grid = (g, ng, r // tt)
        col = lambda off: pl.BlockSpec((1, tt, wb), lambda gi, h, i, off=off: (gi, i, off + h))
        prev_spec = lambda off: pl.BlockSpec((1, SUBLANES, wb), lambda gi, h, i, off=off: (gi, 0, off + h))
        state = pl.BlockSpec((1, hp, HEAD_DIM, HEAD_DIM), lambda gi, h, i: (gi, h, 0, 0))
        cw = lambda off: pl.BlockSpec((GDN_CONV, wb), lambda gi, h, i, off=off: (0, off + h))
        row = pl.BlockSpec((1, HEAD_DIM), lambda gi, h, i: (0, 0))
        abs_ = pl.BlockSpec((1, tt, LANES), lambda gi, h, i: (gi, i, 0))
        scratch = [pltpu.VMEM((hp, HEAD_DIM, HEAD_DIM), F32), pltpu.VMEM((3, SUBLANES, wb), F32)]
        sem = ("parallel", "parallel", "arbitrary")
    return pl.pallas_call(
        functools.partial(_gdn_kernel, tt=tt, chunk=chunk, sample=sample, n_heads=nh, hp=hp),
        grid=grid,
        in_specs=[smem, smem, col(0), col(ng), col(2 * ng), col(0), abs_, cw(0), cw(ng), cw(2 * ng),
                  prev_spec(0), prev_spec(ng), prev_spec(2 * ng), row, state],
        out_specs=[col(0), state],
        out_shape=[jax.ShapeDtypeStruct((g, r, w), BF16), jax.ShapeDtypeStruct(s0.shape, F32)],
        scratch_shapes=scratch,
        compiler_params=_params(*sem),
        name="gdn_sample" if sample else "gdn_prompt",
    )(a_log, dt_bias, gqkv, gqkv, gqkv, z, ab, conv_w, conv_w, conv_w, prev, prev, prev, norm_w, s0)


def _outproj_kernel(x_ref, att_ref, gdn_ref, wa_ref, wb_ref, o_ref):
    o_ref[...] = (x_ref[...] + _dot(att_ref[...].astype(BF16), wa_ref[...])
                  + _dot(gdn_ref[...].astype(BF16), wb_ref[...]))


def _outproj(x2d, att, gdn, w_att, w_gdn, *, tm):
    t, d = x2d.shape
    wa = att.shape[1]
    wg = gdn.shape[1]
    return pl.pallas_call(
        _outproj_kernel,
        grid=(t // tm,),
        in_specs=[pl.BlockSpec((tm, d), lambda i: (i, 0)), pl.BlockSpec((tm, wa), lambda i: (i, 0)),
                  pl.BlockSpec((tm, wg), lambda i: (i, 0)), pl.BlockSpec((wa, d), lambda i: (0, 0)),
                  pl.BlockSpec((wg, d), lambda i: (0, 0))],
        out_specs=pl.BlockSpec((tm, d), lambda i: (i, 0)),
        out_shape=jax.ShapeDtypeStruct((t, d), F32),
        compiler_params=_params("parallel"),
        name="outproj",
    )(x2d, att, gdn, w_att, w_gdn)


def _ffn_up_kernel(x_ref, g_ref, wg_ref, wv_ref, cwg_ref, cwv_ref, cbg_ref, cbv_ref, *rest,
                   tm, tf, sample, tiles_per_seq):
    if sample:
        pg_ref, pv_ref, act_ref, ug_ref, uv_ref, h_ref = rest
    else:
        act_ref, ug_ref, uv_ref, h_ref, carry_g, carry_v, buf_g, buf_v = rest
    i = pl.program_id(0)
    f = pl.program_id(1)

    @pl.when(f == 0)
    def _():
        def store(sl, y):
            h_ref[sl, :] = y.astype(BF16)
        _rmsnorm_rows(lambda sl: x_ref[sl, :], g_ref[...], store, tm)

    hb = h_ref[...]
    if sample:
        rmod = lax.broadcasted_iota(jnp.int32, (tm, tf), 0) & (SUBLANES - 1)

    def conv(w_ref, cw_ref, cb_ref, prev_ref, carry_ref, buf_ref, out_ref):
        cw = cw_ref[...]
        if sample:
            u = _dot(hb, w_ref[...])
            out_ref[...] = u
            shifted = [jnp.where(rmod >= shift, pltpu.roll(u, shift, axis=0), prev_ref[shift - 1])
                       for shift in range(1, FFN_CONV)]
            y = shifted[1] * cw[0:1, :] + shifted[0] * cw[1:2, :] + u * cw[2:3, :]
        else:
            buf_ref[0:SUBLANES, :] = jnp.where(i % tiles_per_seq == 0, 0.0, carry_ref[f])
            buf_ref[SUBLANES:, :] = _dot(hb, w_ref[...])
            last = buf_ref[tm:, :]
            carry_ref[f] = last
            out_ref[0] = last
            y = (buf_ref[SUBLANES - 2:SUBLANES - 2 + tm, :] * cw[0:1, :]
                 + buf_ref[SUBLANES - 1:SUBLANES - 1 + tm, :] * cw[1:2, :]
                 + buf_ref[SUBLANES:, :] * cw[2:3, :])
        return y + cb_ref[...]

    if sample:
        yg = conv(wg_ref, cwg_ref, cbg_ref, pg_ref, None, None, ug_ref)
        yv = conv(wv_ref, cwv_ref, cbv_ref, pv_ref, None, None, uv_ref)
    else:
        yg = conv(wg_ref, cwg_ref, cbg_ref, None, carry_g, buf_g, ug_ref)
        yv = conv(wv_ref, cwv_ref, cbv_ref, None, carry_v, buf_v, uv_ref)
    act_ref[...] = (_silu(yg) * yv).astype(BF16)


def _ffn_up(x2d, g, w_up, conv_w, conv_b, prev, *, tm, seq_rows, sample):
    t, d = x2d.shape
    dff = w_up.shape[1] // 2
    tf = 512
    nf = dff // tf
    assert dff % tf == 0 and t % tm == 0
    gate = lambda shape: pl.BlockSpec(shape, lambda i, f: (0, f))
    val = lambda shape: pl.BlockSpec(shape, lambda i, f: (0, nf + f))
    in_specs = [pl.BlockSpec((tm, d), lambda i, f: (i, 0)), pl.BlockSpec((1, d), lambda i, f: (0, 0)),
                gate((d, tf)), val((d, tf)), gate((FFN_CONV, tf)), val((FFN_CONV, tf)),
                gate((1, tf)), val((1, tf))]
    args = [x2d, g, w_up, w_up, conv_w, conv_w, conv_b, conv_b]
    scratch = [pltpu.VMEM((tm, d), BF16)]
    if sample:
        tiles_per_seq = 0
        in_specs += [pl.BlockSpec((FFN_CONV - 1, tm, tf), lambda i, f: (0, i, f)),
                     pl.BlockSpec((FFN_CONV - 1, tm, tf), lambda i, f: (0, i, nf + f))]
        args += [prev, prev]
        u_spec = pl.BlockSpec((tm, tf), lambda i, f: (i, f))
        u_shape = jax.ShapeDtypeStruct((t, dff), F32)
    else:
        assert seq_rows % tm == 0
        tiles_per_seq = seq_rows // tm
        u_spec = pl.BlockSpec((1, SUBLANES, tf), lambda i, f: (i, 0, f))
        u_shape = jax.ShapeDtypeStruct((t // tm, SUBLANES, dff), F32)
        scratch += [pltpu.VMEM((nf, SUBLANES, tf), F32)] * 2 + [pltpu.VMEM((SUBLANES + tm, tf), F32)] * 2
    return pl.pallas_call(
        functools.partial(_ffn_up_kernel, tm=tm, tf=tf, sample=sample, tiles_per_seq=tiles_per_seq),
        grid=(t // tm, nf),
        in_specs=in_specs,
        out_specs=[pl.BlockSpec((tm, tf), lambda i, f: (i, f)), u_spec, u_spec],
        out_shape=[jax.ShapeDtypeStruct((t, dff), BF16), u_shape, u_shape],
        scratch_shapes=scratch,
        compiler_params=_params("arbitrary", "arbitrary"),
        name="ffn_up_sample" if sample else "ffn_up_prompt",
    )(*args)


def _ffn_down_kernel(x_ref, act_ref, wd_ref, gf_ref, y_ref, xo_ref, *, tm, tn, nn, final_norm):
    n = pl.program_id(1)
    cols = pl.ds(pl.multiple_of(n * tn, tn), tn)
    xo_ref[:, cols] = x_ref[...] + _dot(act_ref[...], wd_ref[...])

    @pl.when(n == nn - 1)
    def _():
        if final_norm:
            def store(sl, y):
                y_ref[sl, :] = y
            _rmsnorm_rows(lambda sl: xo_ref[sl, :], gf_ref[...], store, tm)
        else:
            y_ref[...] = xo_ref[...]


def _ffn_down(x2d, act, w_down, gf, *, tm, final_norm):
    t, d = x2d.shape
    dff = w_down.shape[0]
    tn = 512
    nn = d // tn
    assert d % tn == 0 and t % tm == 0
    return pl.pallas_call(
        functools.partial(_ffn_down_kernel, tm=tm, tn=tn, nn=nn, final_norm=final_norm),
        grid=(t // tm, nn),
        in_specs=[pl.BlockSpec((tm, tn), lambda i, n: (i, n)), pl.BlockSpec((tm, dff), lambda i, n: (i, 0)),
                  pl.BlockSpec((dff, tn), lambda i, n: (0, n)), pl.BlockSpec((1, d), lambda i, n: (0, 0))],
        out_specs=pl.BlockSpec((tm, d), lambda i, n: (i, 0)),
        out_shape=jax.ShapeDtypeStruct((t, d), F32),
        scratch_shapes=[pltpu.VMEM((tm, d), F32)],
        compiler_params=_params("parallel", "arbitrary"),
        name="ffn_down",
    )(x2d, act, w_down, gf)


def _layer_slice(x, l):
    return x.reshape(x.shape[1:]) if x.shape[0] == 1 else x[l]


def _spread_prev(state, n_shift, t_new):
    db, wm1, c = state.shape
    out = []
    for k in range(1, n_shift + 1):
        rows = [state[:, wm1 - k + r] for r in range(k)]
        rows += [jnp.zeros((db, c), state.dtype)] * (t_new - k)
        out.append(jnp.stack(rows, axis=1).reshape(db * t_new, c))
    return jnp.stack(out, axis=0)


def kernel(x_prompt, x_sample, cache_k, cache_v, page_table, state_gdn, state_gdn_conv, state_ffn_conv,
           rms_mix_w, w_in, gdn_conv_w, gdn_a_log, gdn_dt_bias, gdn_norm_w, w_out, rms_ffn_w, w_up,
           ffn_conv_w, ffn_conv_b, w_down, rel_bias, final_norm_w):
    b, s, d = x_prompt.shape
    db, t_new, _ = x_sample.shape
    depth = w_in.shape[0]
    n_att = cache_k.shape[3]
    w_att = n_att * HEAD_DIM
    n_gdn = gdn_a_log.shape[1]
    w_gdn = n_gdn * HEAD_DIM
    dff = w_down.shape[1]
    assert w_att == w_gdn == 1024 and w_in.shape[2] == 3 * w_att + 4 * w_gdn + 2 * n_gdn
    n_main = 3 * w_att + 4 * w_gdn
    n_pool = cache_k.shape[1]
    grp = 256 // t_new
    tm = 512

    bias_tiles, bias_far = _bias_prompt(rel_bias, MOBA_BLOCK)
    blast, bown, bfar = _bias_sample(rel_bias, MOBA_BLOCK, t_new)

    xp = x_prompt.reshape(b * s, d)
    xs = x_sample.reshape(db * t_new, d)
    outs_p = [[] for _ in range(5)]
    outs_s = [[] for _ in range(5)]
    for l in range(depth):
        w_in_b = w_in[l].astype(BF16)
        w_main = w_in_b[:, :n_main]
        w_ab = jnp.pad(w_in_b[:, n_main:], ((0, 0), (0, LANES - 2 * n_gdn)))
        w_out_b = w_out[l].astype(BF16)
        w_up_b = w_up[l].astype(BF16)
        w_down_b = w_down[l].astype(BF16)
        g_mix = rms_mix_w[l].reshape(1, d)
        g_ffn = rms_ffn_w[l].reshape(1, d)
        g_fin = final_norm_w.reshape(1, d)
        norm_w = gdn_norm_w[l].reshape(1, HEAD_DIM)
        conv_b = ffn_conv_b[l].reshape(1, 2 * dff)
        last = l == depth - 1

        widths = (w_att, w_att, w_att, 3 * w_gdn, w_gdn)
        q, k, v, gq, z, ab = _inproj(xp, g_mix, w_main, w_ab, widths, tm=2 * tm, tn=512)
        att = _moba_prompt(q.reshape(b, s, w_att), k.reshape(b, s, w_att), v.reshape(b, s, w_att),
                           bias_tiles, bias_far)
        gdn, s_new = _gdn(gq.reshape(b, s, 3 * w_gdn), z.reshape(b, s, w_gdn), ab.reshape(b, s, LANES),
                          gdn_conv_w[l], jnp.zeros((b, SUBLANES, 3 * w_gdn), F32), gdn_a_log[l], gdn_dt_bias[l],
                          norm_w, jnp.zeros((b, n_gdn, HEAD_DIM, HEAD_DIM), F32), sample=False, hp=8)
        x1 = _outproj(xp, att.reshape(b * s, w_att), gdn.reshape(b * s, w_gdn), w_out_b[:w_att], w_out_b[w_att:],
                      tm=tm)
        act, ug, uv = _ffn_up(x1, g_ffn, w_up_b, ffn_conv_w[l], conv_b, None, tm=2 * tm, seq_rows=s, sample=False)
        y = _ffn_down(x1, act, w_down_b, g_fin, tm=tm, final_norm=last)
        outs_p[0].append(k.reshape(b, s, n_att, HEAD_DIM))
        outs_p[1].append(v.reshape(b, s, n_att, HEAD_DIM))
        outs_p[2].append(s_new)
        outs_p[3].append(gq.reshape(b, s, 3 * w_gdn)[:, s - (GDN_CONV - 1):])
        tps = s // (2 * tm)
        outs_p[4].append(jnp.concatenate([ug, uv], axis=-1)[tps - 1::tps, SUBLANES - (FFN_CONV - 1):])
        xp = y

        ts = db * t_new
        q, k, v, gq, z, ab = _inproj(xs, g_mix, w_main, w_ab, widths, tm=2 * tm, tn=512)
        att = _moba_sample(q, k, v, cache_k.reshape((depth * n_pool,) + cache_k.shape[2:]),
                           cache_v.reshape((depth * n_pool,) + cache_v.shape[2:]), page_table + l * n_pool,
                           blast, bown, bfar, t_new=t_new)
        prev_g = _spread_prev(state_gdn_conv[l], GDN_CONV - 1, t_new).reshape(GDN_CONV - 1, db // grp, 256, 3 * w_gdn)
        gdn, s_new = _gdn(gq.reshape(db // grp, 256, 3 * w_gdn), z.reshape(db // grp, 256, w_gdn),
                          ab.reshape(db // grp, 256, LANES), gdn_conv_w[l], prev_g, gdn_a_log[l], gdn_dt_bias[l],
                          norm_w, _layer_slice(state_gdn, l), sample=True, hp=2)
        x1 = _outproj(xs, att, gdn.reshape(ts, w_gdn), w_out_b[:w_att], w_out_b[w_att:], tm=tm)
        prev_f = _spread_prev(state_ffn_conv[l], FFN_CONV - 1, t_new)
        act, ug, uv = _ffn_up(x1, g_ffn, w_up_b, ffn_conv_w[l], conv_b, prev_f, tm=tm, seq_rows=t_new, sample=True)
        y = _ffn_down(x1, act, w_down_b, g_fin, tm=tm, final_norm=last)
        outs_s[0].append(k.reshape(db, t_new, n_att, HEAD_DIM))
        outs_s[1].append(v.reshape(db, t_new, n_att, HEAD_DIM))
        outs_s[2].append(s_new)
        outs_s[3].append(gq.reshape(db, t_new, 3 * w_gdn)[:, t_new - (GDN_CONV - 1):])
        outs_s[4].append(jnp.concatenate([ug, uv], axis=-1).reshape(db, t_new, 2 * dff)[:, t_new - (FFN_CONV - 1):])
        xs = y

    return (xp.reshape(b, s, d), xs.reshape(db, t_new, d),
            *[jnp.stack(o) for o in outs_p], *[jnp.stack(o) for o in outs_s])
```

```python
import functools
import math

import jax
import jax.numpy as jnp
import numpy as np
from jax import lax
from jax.experimental import pallas as pl
from jax.experimental.pallas import tpu as pltpu

F32 = jnp.float32
BF16 = jnp.bfloat16
HI = lax.Precision.HIGHEST

HEAD_DIM = 128
MOBA_BLOCK = 256
MOBA_TOP_K = 3
PAGE_SIZE = 128
GDN_CONV = 4
GDN_CHUNK = 64
FFN_CONV = 3
NUM_BUCKETS = 32
MAX_DISTANCE = 128
EPS = 1e-6
NEG = -1e30

LANES = 128
SUBLANES = 8
VMEM_LIMIT = 56 * 1024 * 1024


def _bucket_thresholds():
    n = np.arange(0, 4 * MAX_DISTANCE)
    max_exact = NUM_BUCKETS // 2
    nf = np.maximum(n, 1).astype(np.float32)
    large = max_exact + (np.log(nf / np.float32(max_exact)) / np.float32(math.log(MAX_DISTANCE / max_exact))
                         * np.float32(NUM_BUCKETS - max_exact)).astype(np.int32)
    large = np.minimum(large, NUM_BUCKETS - 1)
    b = np.where(n < max_exact, n, large)
    return [int(np.argmax(b >= k)) for k in range(1, NUM_BUCKETS)]


BUCKET_THR = _bucket_thresholds()


def _dot(a, b, precision=None):
    return jnp.dot(a, b, preferred_element_type=F32, precision=precision)


def _dot_nt(a, b, precision=None):
    return lax.dot_general(a, b, (((1,), (1,)), ((), ())), preferred_element_type=F32, precision=precision)


def _dot_tn(a, b, precision=None):
    return lax.dot_general(a, b, (((0,), (0,)), ((), ())), preferred_element_type=F32, precision=precision)


def _silu(x):
    return x * jax.nn.sigmoid(x)


def _softplus(x):
    return jnp.maximum(x, 0.0) + jnp.log1p(jnp.exp(-jnp.abs(x)))


def _params(*semantics):
    return pltpu.CompilerParams(dimension_semantics=semantics, vmem_limit_bytes=VMEM_LIMIT)


def _rmsnorm_rows(load, g, store, rows, chunk=256):
    def body(r, carry):
        sl = pl.ds(pl.multiple_of(r * chunk, chunk), chunk)
        x = load(sl)
        ms = jnp.mean(x * x, axis=-1, keepdims=True)
        store(sl, x * lax.rsqrt(ms + EPS) * g)
        return carry
    lax.fori_loop(0, rows // chunk, body, 0)


def _inproj_kernel(x_ref, g_ref, w_ref, wab_ref, *rest, tn, offsets):
    out_refs, ab_ref = rest[:-1], rest[-1]
    x = x_ref[...]
    ms = jnp.mean(x * x, axis=-1, keepdims=True)
    h = (x * lax.rsqrt(ms + EPS) * g_ref[...]).astype(BF16)
    ab_ref[...] = _dot(h, wab_ref[...])
    for off, ref in zip(offsets, out_refs):
        for c in range(0, ref.shape[1], tn):
            ref[:, c:c + tn] = _dot(h, w_ref[:, off + c:off + c + tn])


def _inproj(x2d, g, w_main, w_ab, widths, *, tm, tn):
    t, d = x2d.shape
    assert t % tm == 0 and all(w % tn == 0 for w in widths) and sum(widths) <= w_main.shape[1]
    offsets = [sum(widths[:n]) for n in range(len(widths))]
    resident = lambda shape: pl.BlockSpec(shape, lambda i: (0, 0), pipeline_mode=pl.Buffered(1))
    return pl.pallas_call(
        functools.partial(_inproj_kernel, tn=tn, offsets=tuple(offsets)),
        grid=(t // tm,),
        in_specs=[pl.BlockSpec((tm, d), lambda i: (i, 0)), resident((1, d)),
                  resident(w_main.shape), resident((d, LANES))],
        out_specs=[pl.BlockSpec((tm, w), lambda i: (i, 0)) for w in widths]
        + [pl.BlockSpec((tm, LANES), lambda i: (i, 0))],
        out_shape=[jax.ShapeDtypeStruct((t, w), F32) for w in widths] + [jax.ShapeDtypeStruct((t, LANES), F32)],
        compiler_params=_params("parallel"),
        name="inproj",
    )(x2d, g, w_main, w_ab)


def _bias_chain(dist, value_of_bucket):
    val = value_of_bucket(0)
    for k in range(1, NUM_BUCKETS):
        val = jnp.where(dist >= BUCKET_THR[k - 1], value_of_bucket(k), val)
    return val


def _bias_prompt_kernel(rb_ref, tiles_ref, *, blk):
    h = pl.program_id(0)
    qi = lax.broadcasted_iota(jnp.int32, (blk, blk), 0)
    kj = lax.broadcasted_iota(jnp.int32, (blk, blk), 1)
    far = rb_ref[NUM_BUCKETS - 1, h]
    for delta in range(2):
        dist = delta * blk + qi - kj
        val = _bias_chain(dist, lambda k: rb_ref[k, h]) - far
        if delta == 0:
            val = jnp.where(dist >= 0, val, NEG)
        tiles_ref[0, delta] = val


def _bias_prompt(rel_bias, blk):
    nh = rel_bias.shape[1]
    assert blk + 1 >= BUCKET_THR[-1]
    return pl.pallas_call(
        functools.partial(_bias_prompt_kernel, blk=blk),
        grid=(nh,),
        in_specs=[pl.BlockSpec(memory_space=pltpu.SMEM)],
        out_specs=pl.BlockSpec((1, 2, blk, blk), lambda h: (h, 0, 0, 0)),
        out_shape=jax.ShapeDtypeStruct((nh, 2, blk, blk), F32),
        compiler_params=_params("arbitrary"),
        name="bias_prompt",
    )(rel_bias)


def _bias_sample_kernel(rbx_ref, last_ref, own_ref, far_ref, *, blk, t_new):
    tok = lax.broadcasted_iota(jnp.int32, (blk, LANES), 1) & (t_new - 1)
    pos = lax.broadcasted_iota(jnp.int32, (blk, LANES), 0)
    row = lambda k: rbx_ref[k:k + 1, :]
    last_ref[...] = _bias_chain(blk + tok - pos, row)
    tok = lax.broadcasted_iota(jnp.int32, (LANES, LANES), 1) & (t_new - 1)
    pos = lax.broadcasted_iota(jnp.int32, (LANES, LANES), 0)
    dist = tok - pos
    own_ref[...] = jnp.where((dist >= 0) & (pos < t_new), _bias_chain(dist, row), NEG)
    far_ref[...] = jnp.broadcast_to(row(NUM_BUCKETS - 1), (SUBLANES, LANES))


def _bias_sample(rel_bias, blk, t_new):
    nh = rel_bias.shape[1]
    assert nh * t_new <= LANES and blk >= LANES
    rbx = jnp.pad(jnp.repeat(rel_bias, t_new, axis=1), ((0, 0), (0, LANES - nh * t_new)))
    return pl.pallas_call(
        functools.partial(_bias_sample_kernel, blk=blk, t_new=t_new),
        out_shape=[jax.ShapeDtypeStruct((blk, LANES), F32), jax.ShapeDtypeStruct((LANES, LANES), F32),
                   jax.ShapeDtypeStruct((SUBLANES, LANES), F32)],
        name="bias_sample",
    )(rbx)


def _moba_prompt_kernel(q_ref, k_ref, v_ref, bias_ref, o_ref, kb_ref, vb_ref, *, nb, blk, n_sel):
    scale = HEAD_DIM ** -0.5
    s = nb * blk
    k = k_ref[0]
    kb_ref[:, 0:HEAD_DIM] = k.astype(BF16)
    pos_blk = lax.broadcasted_iota(jnp.int32, (s, LANES), 0) >> int(math.log2(blk))
    kb_ref[:, HEAD_DIM:] = jnp.where(pos_blk == lax.broadcasted_iota(jnp.int32, (s, LANES), 1), 1.0, 0.0).astype(BF16)
    vb_ref[...] = v_ref[0].astype(BF16)
    means = jnp.sum(k.reshape(nb, blk, HEAD_DIM), axis=1) * (1.0 / blk)
    means_s = _split(jnp.concatenate([means, jnp.zeros((LANES - nb, HEAD_DIM), F32)], axis=0))
    bidx = lax.broadcasted_iota(jnp.int32, (nb, blk), 0)
    for c in range(nb):
        rows = slice(c * blk, (c + 1) * blk)
        q = q_ref[0, rows, :]
        qx = (q * scale).astype(BF16)
        width = HEAD_DIM
        if c > n_sel:
            sc = jnp.where(bidx < c, _dot3(means_s, _split(q), nt=True)[:nb], NEG)
            rank = jnp.zeros((nb, blk), jnp.int32)
            for i in range(c):
                ri = sc[i:i + 1, :]
                tie = (bidx > i).astype(jnp.int32)
                rank = rank + jnp.where(ri > sc, 1, jnp.where(ri == sc, tie, 0))
            drop = jnp.where(bidx < c, jnp.where(rank < n_sel, 0.0, NEG), 0.0)
            drop = jnp.concatenate([drop, jnp.zeros((LANES - nb, blk), F32)], axis=0).T
            qx = jnp.concatenate([qx, drop.astype(BF16)], axis=1)
            width = HEAD_DIM + LANES
        parts = []
        for j in range(c):
            sj = _dot_nt(qx, kb_ref[j * blk:(j + 1) * blk, 0:width])
            parts.append(sj + bias_ref[0, 1] if j == c - 1 else sj)
        parts.append(_dot_nt(qx, kb_ref[rows, 0:width]) + bias_ref[0, 0])
        mm = parts[0]
        for p in parts[1:]:
            mm = jnp.maximum(mm, p)
        m = jnp.max(mm, axis=1, keepdims=True)
        ps = [jnp.exp(p - m) for p in parts]
        tot = ps[0]
        for p in ps[1:]:
            tot = tot + p
        l = jnp.sum(tot, axis=1, keepdims=True)
        pcat = jnp.concatenate([p.astype(BF16) for p in ps], axis=1)
        o = _dot(pcat, vb_ref[0:(c + 1) * blk, :])
        o_ref[0, rows, :] = (o / l).astype(o_ref.dtype)


def _moba_prompt(q, k, v, bias_tiles):
    b, s, w = q.shape
    nh = w // HEAD_DIM
    blk = MOBA_BLOCK
    assert s % blk == 0
    nb = s // blk
    n_sel = min(MOBA_TOP_K, nb - 1)
    col = pl.BlockSpec((1, s, HEAD_DIM), lambda bi, h: (bi, 0, h))
    return pl.pallas_call(
        functools.partial(_moba_prompt_kernel, nb=nb, blk=blk, n_sel=n_sel),
        grid=(b, nh),
        in_specs=[col, col, col,
                  pl.BlockSpec((1, 2, blk, blk), lambda bi, h: (h, 0, 0, 0))],
        out_specs=col,
        out_shape=jax.ShapeDtypeStruct((b, s, w), BF16),
        scratch_shapes=[pltpu.VMEM((s, HEAD_DIM + LANES), BF16), pltpu.VMEM((s, HEAD_DIM), BF16)],
        compiler_params=_params("parallel", "parallel"),
        name="moba_prompt",
    )(q, k, v, bias_tiles)


def _moba_sample_kernel(pt_ref, q_ref, kn_ref, vn_ref, blast_ref, bown_ref, bfar_ref, *rest,
                        n_pages, ppb, n_sel):
    del pt_ref
    kps, vps, o_ref = rest[:n_pages], rest[n_pages:2 * n_pages], rest[2 * n_pages]
    scale = HEAD_DIM ** -0.5
    t_new, width = q_ref.shape
    nh = width // HEAD_DIM
    cur = n_pages // ppb
    blk = ppb * PAGE_SIZE
    rid = lax.broadcasted_iota(jnp.int32, (LANES, width), 0)
    cid = lax.broadcasted_iota(jnp.int32, (LANES, width), 1)
    q_rep = jnp.concatenate([q_ref[...]] * (LANES // t_new), axis=0)
    qt = jnp.where((rid >> int(math.log2(t_new))) == (cid >> int(math.log2(HEAD_DIM))), q_rep, 0.0)
    qt_s = (qt * scale).astype(BF16)

    def page_matrix(ref):
        return jnp.concatenate([ref[0, pl.ds(h, PAGE_SIZE, stride=nh), :] for h in range(nh)], axis=1)

    logits, ksums = [], []
    for j in range(n_pages):
        kj = page_matrix(kps[j])
        logits.append(_dot_nt(kj.astype(BF16), qt_s))
        ksums.append(jnp.sum(kj, axis=0, keepdims=True))
    means = []
    for jb in range(cur):
        tot = ksums[jb * ppb]
        for r in range(1, ppb):
            tot = tot + ksums[jb * ppb + r]
        means.append(tot * (1.0 / blk))
    sc = _dot_nt(jnp.concatenate(means, axis=0), qt, HI)
    bidx = lax.broadcasted_iota(jnp.int32, (cur, LANES), 0)
    rank = jnp.zeros((cur, LANES), jnp.int32)
    for i in range(cur):
        ri = sc[i:i + 1, :]
        tie = (bidx > i).astype(jnp.int32)
        rank = rank + jnp.where(ri > sc, 1, jnp.where(ri == sc, tie, 0))

    parts = []
    for jb in range(cur):
        sb = jnp.concatenate(logits[jb * ppb:(jb + 1) * ppb], axis=0)
        sb = sb + (blast_ref[...] if jb == cur - 1 else bfar_ref[0:1, :])
        parts.append(jnp.where(rank[jb:jb + 1, :] < n_sel, sb, NEG))
    pad = jnp.zeros((LANES - t_new, width), F32)
    kn = jnp.concatenate([kn_ref[...], pad], axis=0)
    vn = jnp.concatenate([vn_ref[...], pad], axis=0)
    parts.append(_dot_nt(kn.astype(BF16), qt_s) + bown_ref[...])
    mm = parts[0]
    for p in parts[1:-1]:
        mm = jnp.maximum(mm, p)
    m = jnp.maximum(jnp.max(mm, axis=0, keepdims=True), jnp.max(parts[-1], axis=0, keepdims=True))
    acc = jnp.zeros((LANES, width), F32)
    lsum = jnp.zeros((LANES, 1), F32)
    for jb in range(cur + 1):
        pt = jnp.exp(parts[jb] - m).T
        lsum = lsum + jnp.sum(pt, axis=1, keepdims=True)
        if jb < cur:
            vb = jnp.concatenate([page_matrix(vps[jb * ppb + r]).astype(BF16) for r in range(ppb)], axis=0)
        else:
            vb = vn.astype(BF16)
        acc = acc + _dot(pt.astype(BF16), vb)
    out = acc / lsum
    for h in range(nh):
        o_ref[:, h * HEAD_DIM:(h + 1) * HEAD_DIM] = out[h * t_new:(h + 1) * t_new, h * HEAD_DIM:(h + 1) * HEAD_DIM]


def _moba_sample(q, k_new, v_new, cache_k, cache_v, page_table, blast, bown, bfar, *, t_new):
    rows, width = q.shape
    db = rows // t_new
    n_pool = cache_k.shape[0]
    n_pages = page_table.shape[1]
    ppb = MOBA_BLOCK // PAGE_SIZE
    assert n_pages % ppb == 0, "new tokens must start a fresh MoBA block"
    assert t_new == SUBLANES
    cur = n_pages // ppb
    n_sel = min(MOBA_TOP_K, cur)
    assert n_sel > 0
    nh = width // HEAD_DIM
    ck = cache_k.reshape(n_pool, PAGE_SIZE * nh, HEAD_DIM)
    cv = cache_v.reshape(n_pool, PAGE_SIZE * nh, HEAD_DIM)
    new = pl.BlockSpec((t_new, width), lambda s, pt: (s, 0))
    const = lambda shape: pl.BlockSpec(shape, lambda s, pt: (0, 0))
    page = lambda j: pl.BlockSpec((1, PAGE_SIZE * nh, HEAD_DIM), lambda s, pt, j=j: (pt[s, j], 0, 0))
    grid_spec = pltpu.PrefetchScalarGridSpec(
        num_scalar_prefetch=1,
        grid=(db,),
        in_specs=[new, new, new, const(blast.shape), const(bown.shape), const(bfar.shape)]
        + [page(j) for j in range(n_pages)] * 2,
        out_specs=new,
    )
    return pl.pallas_call(
        functools.partial(_moba_sample_kernel, n_pages=n_pages, ppb=ppb, n_sel=n_sel),
        grid_spec=grid_spec,
        out_shape=jax.ShapeDtypeStruct((rows, width), F32),
        compiler_params=_params("parallel"),
        name="moba_sample",
    )(page_table, q, k_new, v_new, blast, bown, bfar, *([ck] * n_pages), *([cv] * n_pages))


def _split(x):
    hi = x.astype(BF16)
    return hi, (x - hi.astype(F32)).astype(BF16)


def _dot3(a, b, nt=False):
    lhs = jnp.concatenate([a[0], a[0], a[1]], axis=1)
    if nt:
        return _dot_nt(lhs, jnp.concatenate([b[0], b[1], b[0]], axis=1))
    return _dot(lhs, jnp.concatenate([b[0], b[1], b[0]], axis=0))


def _gdn_kernel(alog_ref, dtb_ref, q_ref, k_ref, v_ref, z_ref, ab_ref, cwq_ref, cwk_ref, cwv_ref,
                pq_ref, pk_ref, pv_ref, nw_ref, s0_ref, o_ref, so_ref, *scratch, tt, chunk, sample, n_heads, hp):
    hg = pl.program_id(1)
    n_chunks = tt // chunk
    ins = [q_ref, k_ref, v_ref]
    cw_refs = [cwq_ref, cwk_ref, cwv_ref]
    prevs = [pq_ref, pk_ref, pv_ref]
    rid8 = lax.broadcasted_iota(jnp.int32, (SUBLANES, HEAD_DIM), 0)
    if sample:
        (shift_scr,) = scratch
    else:
        s_scr, tail_scr = scratch
        i = pl.program_id(2)

        @pl.when(i == 0)
        def _():
            s_scr[...] = s0_ref[0]
            for n in range(3):
                tail_scr[n] = prevs[n][0]

    ri = lax.broadcasted_iota(jnp.int32, (tt, tt), 0)
    ci = lax.broadcasted_iota(jnp.int32, (tt, tt), 1)
    shift_c = int(math.log2(chunk))
    same_block = lambda bits: (ri >> bits) == (ci >> bits)
    same = same_block(shift_c)
    incl = same & (ri >= ci)
    strict = same & (ri > ci)
    incl_t = same & (ri <= ci)
    eye = jnp.where(ri == ci, 1.0, 0.0)
    base_bits = int(math.log2(SUBLANES))
    base = same_block(base_bits)
    level_masks = [same_block(bits + 1) & (ri >> bits != ci >> bits) for bits in range(base_bits, shift_c)]
    sum_mat = jnp.concatenate([jnp.where(incl, 1.0, 0.0), jnp.where(same, 1.0, 0.0)], axis=0).astype(BF16)
    sum_mat3 = jnp.concatenate([sum_mat] * 3, axis=1)
    lane = lax.broadcasted_iota(jnp.int32, (tt, LANES), 1)
    ab = ab_ref[0]
    heads = range(hp)
    cols = [slice(hh * HEAD_DIM, (hh + 1) * HEAD_DIM) for hh in heads]
    chunks = [slice(c * chunk, (c + 1) * chunk) for c in range(n_chunks)]


    xq, xk, xv = [], [], []
    for hh in heads:
        for n, dst in enumerate((xq, xk, xv)):
            x = ins[n][0, :, cols[hh]]
            w = cw_refs[n][:, cols[hh]]
            acc = None
            for shift in range(GDN_CONV - 1, 0, -1):
                rolled = pltpu.roll(x, shift, axis=0)
                if sample:
                    slot = (hh * 3 + n) * (GDN_CONV - 1) + shift - 1
                    shift_scr[slot] = rolled
                    for r in range(shift):
                        shift_scr[slot, pl.ds(r, n_chunks, stride=chunk), :] = (
                            prevs[n][GDN_CONV - 1 - shift + r, 0, :, cols[hh]])
                    xm = shift_scr[slot]
                else:
                    first = jnp.where(rid8 < shift, pltpu.roll(tail_scr[n, :, cols[hh]], shift, axis=0),
                                      rolled[:SUBLANES])
                    xm = jnp.concatenate([first, rolled[SUBLANES:]], axis=0)
                tap = GDN_CONV - 1 - shift
                term = xm * w[tap:tap + 1, :]
                acc = term if acc is None else acc + term
            acc = acc + x * w[GDN_CONV - 1:GDN_CONV, :]
            dst.append(_silu(acc))
            if not sample:
                tail_scr[n, :, cols[hh]] = x[tt - SUBLANES:]

    beta, g_col, sums = [], [], []
    for hh in heads:
        h = hg * hp + hh
        a_col = jnp.sum(jnp.where(lane == h, ab, 0.0), axis=1, keepdims=True)
        b_col = jnp.sum(jnp.where(lane == h + n_heads, ab, 0.0), axis=1, keepdims=True)
        alog = jnp.full((tt, 1), alog_ref[h], F32)
        dtb = jnp.full((tt, 1), dtb_ref[h], F32)
        g_col.append(-jnp.exp(alog) * _softplus(a_col + dtb))
        beta.append(jax.nn.sigmoid(b_col))
        g_rep = jnp.broadcast_to(g_col[hh], (tt, HEAD_DIM))
        g0 = g_rep.astype(BF16)
        r1 = g_rep - g0.astype(F32)
        g1 = r1.astype(BF16)
        g2 = (r1 - g1.astype(F32)).astype(BF16)
        sums.append(_dot(sum_mat3, jnp.concatenate([g0, g1, g2], axis=0)))
    cg_rep = [s[:tt] for s in sums]
    gl_rep = [s[tt:] for s in sums]
    decay = []
    for hh in heads:
        cg_row = jnp.sum(jnp.where(incl_t, g_col[hh], 0.0), axis=0, keepdims=True)
        decay.append(jnp.exp(jnp.where(incl, cg_rep[hh][:, 0:1] - cg_row, NEG)))

    qn = [xq[hh] * lax.rsqrt(jnp.sum(xq[hh] * xq[hh], axis=-1, keepdims=True) + 1e-6) * (HEAD_DIM ** -0.5)
          for hh in heads]
    kn = [xk[hh] * lax.rsqrt(jnp.sum(xk[hh] * xk[hh], axis=-1, keepdims=True) + 1e-6) for hh in heads]
    kn_s = [_split(kn[hh]) for hh in heads]
    kk = [_dot3(kn_s[hh], kn_s[hh], nt=True) for hh in heads]
    lmat = [jnp.where(strict, beta[hh] * decay[hh] * kk[hh], 0.0) for hh in heads]

    xp = [-jnp.where(base, lmat[hh], 0.0) for hh in heads]
    tinv = [eye + xp[hh] for hh in heads]
    xp_s = [_split(xp[hh]) for hh in heads]
    for _ in range(base_bits - 1):
        xp_s = [_split(_dot3(xp_s[hh], xp_s[hh])) for hh in heads]
        tinv = [tinv[hh] + _dot3(_split(tinv[hh]), xp_s[hh]) for hh in heads]
    l_s = [_split(lmat[hh]) for hh in heads] if level_masks else None
    for mask in level_masks:
        t_s = [_split(tinv[hh]) for hh in heads]
        lt = [_split(_dot3(l_s[hh], t_s[hh])) for hh in heads]
        tinv = [tinv[hh] - jnp.where(mask, _dot3(t_s[hh], lt[hh]), 0.0) for hh in heads]

    eg = [jnp.exp(cg_rep[hh]) for hh in heads]
    wu = [_dot3(_split(tinv[hh]),
                _split(jnp.concatenate([kn[hh] * (beta[hh] * eg[hh]), xv[hh] * beta[hh]], axis=1)))
          for hh in heads]
    u0 = [wu[hh][:, HEAD_DIM:] for hh in heads]
    amat = [jnp.where(incl, _dot_nt(qn[hh].astype(BF16), kn_s[hh][0]) * decay[hh], 0.0).astype(BF16)
            for hh in heads]
    wq = [[jnp.concatenate([wu[hh][cs, :HEAD_DIM], (qn[hh] * eg[hh])[cs]], axis=0).astype(BF16) for cs in chunks]
          for hh in heads]
    kd = [kn[hh] * jnp.exp(gl_rep[hh] - cg_rep[hh]) for hh in heads]
    egl = [jnp.exp(gl_rep[hh]) for hh in heads]

    if sample:
        o = []
        for hh in heads:
            rs = [_dot(wq[hh][c], s0_ref[c, hh].astype(BF16)) for c in range(n_chunks)]
            u = u0[hh] - jnp.concatenate([r[:chunk] for r in rs], axis=0)
            o.append(jnp.concatenate([r[chunk:] for r in rs], axis=0) + _dot(amat[hh], u.astype(BF16)))
            for c, cs in enumerate(chunks):
                so_ref[c, hh] = (egl[hh][c * chunk:c * chunk + 1, :] * s0_ref[c, hh]
                                 + _dot_tn(kd[hh][cs], u[cs], HI))
    else:
        kd_t = [kd[hh].T.astype(BF16) for hh in heads]
        s = [s_scr[hh] for hh in heads]
        outs = [[] for _ in heads]
        for c, cs in enumerate(chunks):
            for hh in heads:
                r = _dot(wq[hh][c], s[hh].astype(BF16))
                u = (u0[hh][cs] - r[:chunk]).astype(BF16)
                pieces = [u]
                if c > 0:
                    pieces.insert(0, jnp.zeros((c * chunk, HEAD_DIM), BF16))
                if c < n_chunks - 1:
                    pieces.append(jnp.zeros((tt - (c + 1) * chunk, HEAD_DIM), BF16))
                u_full = jnp.concatenate(pieces, axis=0)
                outs[hh].append(r[chunk:] + _dot(amat[hh][cs, :], u_full))
                s[hh] = egl[hh][c * chunk:c * chunk + 1, :] * s[hh] + _dot(kd_t[hh], u_full)
        for hh in heads:
            s_scr[hh] = s[hh]
            so_ref[0, hh] = s[hh]
        o = [jnp.concatenate(outs[hh], axis=0) for hh in heads]

    for hh in heads:
        z = z_ref[0, :, cols[hh]]
        on = (o[hh] * lax.rsqrt(jnp.mean(o[hh] * o[hh], axis=-1, keepdims=True) + EPS) * nw_ref[...] * _silu(z))
        o_ref[0, :, cols[hh]] = on.astype(o_ref.dtype)


def _gdn(gqkv, z, ab, conv_w, prev, a_log, dt_bias, norm_w, s0, *, sample, hp):
    g, r, w3 = gqkv.shape
    w = w3 // 3
    nh = w // HEAD_DIM
    tt = 256
    assert r % tt == 0 and nh % hp == 0
    ng = nh // hp
    wb = hp * HEAD_DIM
    smem = pl.BlockSpec(memory_space=pltpu.SMEM)
    if sample:
        assert r == tt
        chunk = SUBLANES
        seqs = tt // chunk
        grid = (g, ng)
        col = lambda off: pl.BlockSpec((1, tt, wb), lambda gi, h, off=off: (gi, 0, off + h))
        prev_spec = lambda off: pl.BlockSpec((GDN_CONV - 1, 1, seqs, wb), lambda gi, h, off=off: (0, gi, 0, off + h))
        state = pl.BlockSpec((seqs, hp, HEAD_DIM, HEAD_DIM), lambda gi, h: (gi, h, 0, 0))
        cw = lambda off: pl.BlockSpec((GDN_CONV, wb), lambda gi, h, off=off: (0, off + h))
        row = pl.BlockSpec((1, HEAD_DIM), lambda gi, h: (0, 0))
        abs_ = pl.BlockSpec((1, tt, LANES), lambda gi, h: (gi, 0, 0))
        scratch = [pltpu.VMEM((hp * 3 * (GDN_CONV - 1), tt, HEAD_DIM), F32)]
        sem = ("parallel", "parallel")
    else:
        chunk = GDN_CHUNK
        grid = (g, ng, r // tt)
        col = lambda off: pl.BlockSpec((1, tt, wb), lambda gi, h, i, off=off: (gi, i, off + h))
        prev_spec = lambda off: pl.BlockSpec((1, SUBLANES, wb), lambda gi, h, i, off=off: (gi, 0, off + h))
        state = pl.BlockSpec((1, hp, HEAD_DIM, HEAD_DIM), lambda gi, h, i: (gi, h, 0, 0))
        cw = lambda off: pl.BlockSpec((GDN_CONV, wb), lambda gi, h, i, off=off: (0, off + h))
        row = pl.BlockSpec((1, HEAD_DIM), lambda gi, h, i: (0, 0))
        abs_ = pl.BlockSpec((1, tt, LANES), lambda gi, h, i: (gi, i, 0))
        scratch = [pltpu.VMEM((hp, HEAD_DIM, HEAD_DIM), F32), pltpu.VMEM((3, SUBLANES, wb), F32)]
        sem = ("parallel", "parallel", "arbitrary")
    return pl.pallas_call(
        functools.partial(_gdn_kernel, tt=tt, chunk=chunk, sample=sample, n_heads=nh, hp=hp),
        grid=grid,
        in_specs=[smem, smem, col(0), col(ng), col(2 * ng), col(0), abs_, cw(0), cw(ng), cw(2 * ng),
                  prev_spec(0), prev_spec(ng), prev_spec(2 * ng), row, state],
        out_specs=[col(0), state],
        out_shape=[jax.ShapeDtypeStruct((g, r, w), BF16), jax.ShapeDtypeStruct(s0.shape, F32)],
        scratch_shapes=scratch,
        compiler_params=_params(*sem),
        name="gdn_sample" if sample else "gdn_prompt",
    )(a_log, dt_bias, gqkv, gqkv, gqkv, z, ab, conv_w, conv_w, conv_w, prev, prev, prev, norm_w, s0)


def _outproj_kernel(x_ref, att_ref, gdn_ref, wa_ref, wb_ref, o_ref):
    o_ref[...] = (x_ref[...] + _dot(att_ref[...].astype(BF16), wa_ref[...])
                  + _dot(gdn_ref[...].astype(BF16), wb_ref[...]))


def _outproj(x2d, att, gdn, w_out, *, tm):
    t, d = x2d.shape
    wa = att.shape[1]
    assert gdn.shape[1] == wa and w_out.shape[0] == 2 * wa
    return pl.pallas_call(
        _outproj_kernel,
        grid=(t // tm,),
        in_specs=[pl.BlockSpec((tm, d), lambda i: (i, 0)), pl.BlockSpec((tm, wa), lambda i: (i, 0)),
                  pl.BlockSpec((tm, wa), lambda i: (i, 0)), pl.BlockSpec((wa, d), lambda i: (0, 0)),
                  pl.BlockSpec((wa, d), lambda i: (1, 0))],
        out_specs=pl.BlockSpec((tm, d), lambda i: (i, 0)),
        out_shape=jax.ShapeDtypeStruct((t, d), F32),
        compiler_params=_params("parallel"),
        name="outproj",
    )(x2d, att, gdn, w_out, w_out)


def _ffn_up_kernel(x_ref, g_ref, wg_ref, wv_ref, cwg_ref, cwv_ref, cbg_ref, cbv_ref, *rest,
                   tm, tf, sample, tiles_per_seq):
    if sample:
        (sg0_ref, sg1_ref, sv0_ref, sv1_ref, act_ref, og0_ref, og1_ref, ov0_ref, ov1_ref,
         h_ref, buf_g, buf_v) = rest
    else:
        act_ref, ug_ref, uv_ref, h_ref, carry_g, carry_v, buf_g, buf_v = rest
    i = pl.program_id(0)
    f = pl.program_id(1)

    @pl.when(f == 0)
    def _():
        def store(sl, y):
            h_ref[sl, :] = y.astype(BF16)
        _rmsnorm_rows(lambda sl: x_ref[sl, :], g_ref[...], store, tm)

    hb = h_ref[...]

    def conv(w_ref, cw_ref, cb_ref, state_refs, carry_ref, buf_ref, out_ref):
        cw = cw_ref[...]
        if sample:
            n_seq = tm // SUBLANES
            seq_row = lambda r: pl.ds(r, n_seq, stride=SUBLANES)
            u = _dot(hb, w_ref[...])
            n_col = tf // LANES
            for c in range(n_col):
                cs = slice(c * LANES, (c + 1) * LANES)
                buf_ref[0, c] = u[:, cs]
                for r, o_ref in enumerate(out_ref):
                    o_ref[:, cs] = buf_ref[0, c, seq_row(SUBLANES - (FFN_CONV - 1) + r), :]
                for shift in range(1, FFN_CONV):
                    buf_ref[shift, c] = pltpu.roll(u[:, cs], shift, axis=0)
                    for r in range(shift):
                        buf_ref[shift, c, seq_row(r), :] = state_refs[FFN_CONV - 1 - shift + r][:, cs]
            shifted = {shift: jnp.concatenate([buf_ref[shift, c] for c in range(n_col)], axis=1)
                       for shift in range(1, FFN_CONV)}
            y = shifted[2] * cw[0:1, :] + shifted[1] * cw[1:2, :] + u * cw[2:3, :]
        else:
            buf_ref[0:SUBLANES, :] = jnp.where(i % tiles_per_seq == 0, 0.0, carry_ref[f])
            buf_ref[SUBLANES:, :] = _dot(hb, w_ref[...])
            last = buf_ref[tm:, :]
            carry_ref[f] = last
            out_ref[0] = last
            y = (buf_ref[SUBLANES - 2:SUBLANES - 2 + tm, :] * cw[0:1, :]
                 + buf_ref[SUBLANES - 1:SUBLANES - 1 + tm, :] * cw[1:2, :]
                 + buf_ref[SUBLANES:, :] * cw[2:3, :])
        return y + cb_ref[...]

    if sample:
        yg = conv(wg_ref, cwg_ref, cbg_ref, (sg0_ref, sg1_ref), None, buf_g, (og0_ref, og1_ref))
        yv = conv(wv_ref, cwv_ref, cbv_ref, (sv0_ref, sv1_ref), None, buf_v, (ov0_ref, ov1_ref))
    else:
        yg = conv(wg_ref, cwg_ref, cbg_ref, None, carry_g, buf_g, ug_ref)
        yv = conv(wv_ref, cwv_ref, cbv_ref, None, carry_v, buf_v, uv_ref)
    act_ref[...] = (_silu(yg) * yv).astype(BF16)


def _ffn_up(x2d, g, w_up, conv_w, conv_b, prev, *, tm, seq_rows, sample):
    t, d = x2d.shape
    dff = w_up.shape[1] // 2
    tf = 512
    nf = dff // tf
    assert dff % tf == 0 and t % tm == 0
    gate = lambda shape: pl.BlockSpec(shape, lambda i, f: (0, f))
    val = lambda shape: pl.BlockSpec(shape, lambda i, f: (0, nf + f))
    in_specs = [pl.BlockSpec((tm, d), lambda i, f: (i, 0)), pl.BlockSpec((1, d), lambda i, f: (0, 0)),
                gate((d, tf)), val((d, tf)), gate((FFN_CONV, tf)), val((FFN_CONV, tf)),
                gate((1, tf)), val((1, tf))]
    args = [x2d, g, w_up, w_up, conv_w, conv_w, conv_b, conv_b]
    scratch = [pltpu.VMEM((tm, d), BF16)]
    if sample:
        assert seq_rows == SUBLANES and FFN_CONV == 3
        tiles_per_seq = 0
        n_seq = tm // SUBLANES
        in_specs += [pl.BlockSpec((n_seq, tf), lambda i, f: (i, f))] * 2
        in_specs += [pl.BlockSpec((n_seq, tf), lambda i, f: (i, nf + f))] * 2
        args += [prev[0], prev[1], prev[0], prev[1]]
        u_specs = [pl.BlockSpec((n_seq, tf), lambda i, f: (i, f))] * 4
        u_shapes = [jax.ShapeDtypeStruct((t // SUBLANES, dff), F32)] * 4
        scratch += [pltpu.VMEM((FFN_CONV, tf // LANES, tm, LANES), F32)] * 2
    else:
        assert seq_rows % tm == 0
        tiles_per_seq = seq_rows // tm
        u_specs = [pl.BlockSpec((1, SUBLANES, tf), lambda i, f: (i, 0, f))] * 2
        u_shapes = [jax.ShapeDtypeStruct((t // tm, SUBLANES, dff), F32)] * 2
        scratch += [pltpu.VMEM((nf, SUBLANES, tf), F32)] * 2 + [pltpu.VMEM((SUBLANES + tm, tf), F32)] * 2
    return pl.pallas_call(
        functools.partial(_ffn_up_kernel, tm=tm, tf=tf, sample=sample, tiles_per_seq=tiles_per_seq),
        grid=(t // tm, nf),
        in_specs=in_specs,
        out_specs=[pl.BlockSpec((tm, tf), lambda i, f: (i, f))] + u_specs,
        out_shape=[jax.ShapeDtypeStruct((t, dff), BF16)] + u_shapes,
        scratch_shapes=scratch,
        compiler_params=_params("arbitrary", "arbitrary"),
        name="ffn_up_sample" if sample else "ffn_up_prompt",
    )(*args)


def _ffn_down_kernel(x_ref, act_ref, wd_ref, gf_ref, y_ref, *, tm, tn, final_norm):
    act = act_ref[...]
    for c in range(0, y_ref.shape[1], tn):
        y_ref[:, c:c + tn] = x_ref[:, c:c + tn] + _dot(act, wd_ref[:, c:c + tn])
    if final_norm:
        def store(sl, y):
            y_ref[sl, :] = y
        _rmsnorm_rows(lambda sl: y_ref[sl, :], gf_ref[...], store, tm)


def _ffn_down(x2d, act, w_down, gf, *, tm, final_norm):
    t, d = x2d.shape
    dff = w_down.shape[0]
    tn = 512
    assert d % tn == 0 and t % tm == 0
    resident = lambda shape: pl.BlockSpec(shape, lambda i: (0, 0), pipeline_mode=pl.Buffered(1))
    return pl.pallas_call(
        functools.partial(_ffn_down_kernel, tm=tm, tn=tn, final_norm=final_norm),
        grid=(t // tm,),
        in_specs=[pl.BlockSpec((tm, d), lambda i: (i, 0)), pl.BlockSpec((tm, dff), lambda i: (i, 0)),
                  resident((dff, d)), resident((1, d))],
        out_specs=pl.BlockSpec((tm, d), lambda i: (i, 0)),
        out_shape=jax.ShapeDtypeStruct((t, d), F32),
        compiler_params=_params("parallel"),
        name="ffn_down",
    )(x2d, act, w_down, gf)


def _layer_slice(x, l):
    return x.reshape(x.shape[1:]) if x.shape[0] == 1 else x[l]


def kernel(x_prompt, x_sample, cache_k, cache_v, page_table, state_gdn, state_gdn_conv, state_ffn_conv,
           rms_mix_w, w_in, gdn_conv_w, gdn_a_log, gdn_dt_bias, gdn_norm_w, w_out, rms_ffn_w, w_up,
           ffn_conv_w, ffn_conv_b, w_down, rel_bias, final_norm_w):
    b, s, d = x_prompt.shape
    db, t_new, _ = x_sample.shape
    depth = w_in.shape[0]
    n_att = cache_k.shape[3]
    w_att = n_att * HEAD_DIM
    n_gdn = gdn_a_log.shape[1]
    w_gdn = n_gdn * HEAD_DIM
    dff = w_down.shape[1]
    assert w_att == w_gdn == 1024 and w_in.shape[2] == 3 * w_att + 4 * w_gdn + 2 * n_gdn
    n_main = 3 * w_att + 4 * w_gdn
    n_pool = cache_k.shape[1]
    grp = 256 // t_new
    tm = 512

    bias_tiles = _bias_prompt(rel_bias, MOBA_BLOCK)
    blast, bown, bfar = _bias_sample(rel_bias, MOBA_BLOCK, t_new)

    xp = x_prompt.reshape(b * s, d)
    xs = x_sample.reshape(db * t_new, d)
    outs_p = [[] for _ in range(5)]
    outs_s = [[] for _ in range(5)]
    for l in range(depth):
        w_in_b = w_in[l].astype(BF16)
        w_main = w_in_b
        w_ab = jnp.pad(w_in_b[:, n_main:], ((0, 0), (0, LANES - 2 * n_gdn)))
        w_out_b = w_out[l].astype(BF16)
        w_up_b = w_up[l].astype(BF16)
        w_down_b = w_down[l].astype(BF16)
        g_mix = rms_mix_w[l].reshape(1, d)
        g_ffn = rms_ffn_w[l].reshape(1, d)
        g_fin = final_norm_w.reshape(1, d)
        norm_w = gdn_norm_w[l].reshape(1, HEAD_DIM)
        conv_b = ffn_conv_b[l].reshape(1, 2 * dff)
        last = l == depth - 1

        widths = (w_att, w_att, w_att, 3 * w_gdn, w_gdn)
        q, k, v, gq, z, ab = _inproj(xp, g_mix, w_main, w_ab, widths, tm=tm // 2, tn=512)
        att = _moba_prompt(q.reshape(b, s, w_att), k.reshape(b, s, w_att), v.reshape(b, s, w_att),
                           bias_tiles)
        gdn, s_new = _gdn(gq.reshape(b, s, 3 * w_gdn), z.reshape(b, s, w_gdn), ab.reshape(b, s, LANES),
                          gdn_conv_w[l], jnp.zeros((b, SUBLANES, 3 * w_gdn), F32), gdn_a_log[l], gdn_dt_bias[l],
                          norm_w, jnp.zeros((b, n_gdn, HEAD_DIM, HEAD_DIM), F32), sample=False, hp=8)
        x1 = _outproj(xp, att.reshape(b * s, w_att), gdn.reshape(b * s, w_gdn), w_out_b, tm=tm)
        act, ug, uv = _ffn_up(x1, g_ffn, w_up_b, ffn_conv_w[l], conv_b, None, tm=2 * tm, seq_rows=s, sample=False)
        y = _ffn_down(x1, act, w_down_b, g_fin, tm=tm, final_norm=last)
        outs_p[0].append(k.reshape(b, s, n_att, HEAD_DIM))
        outs_p[1].append(v.reshape(b, s, n_att, HEAD_DIM))
        outs_p[2].append(s_new)
        outs_p[3].append(gq.reshape(b, s, 3 * w_gdn)[:, s - (GDN_CONV - 1):])
        tps = s // (2 * tm)
        outs_p[4].append(jnp.concatenate([ug, uv], axis=-1)[tps - 1::tps, SUBLANES - (FFN_CONV - 1):])
        xp = y

        ts = db * t_new
        q, k, v, gq, z, ab = _inproj(xs, g_mix, w_main, w_ab, widths, tm=tm // 2, tn=512)
        att = _moba_sample(q, k, v, cache_k.reshape((depth * n_pool,) + cache_k.shape[2:]),
                           cache_v.reshape((depth * n_pool,) + cache_v.shape[2:]), page_table + l * n_pool,
                           blast, bown, bfar, t_new=t_new)
        prev_g = jnp.moveaxis(state_gdn_conv[l], 1, 0).reshape(GDN_CONV - 1, db // grp, grp, 3 * w_gdn)
        gdn, s_new = _gdn(gq.reshape(db // grp, 256, 3 * w_gdn), z.reshape(db // grp, 256, w_gdn),
                          ab.reshape(db // grp, 256, LANES), gdn_conv_w[l], prev_g, gdn_a_log[l], gdn_dt_bias[l],
                          norm_w, _layer_slice(state_gdn, l), sample=True, hp=2)
        x1 = _outproj(xs, att, gdn.reshape(ts, w_gdn), w_out_b, tm=tm)
        prev_f = tuple(state_ffn_conv[l, :, r] for r in range(FFN_CONV - 1))
        act, g0, g1, v0, v1 = _ffn_up(x1, g_ffn, w_up_b, ffn_conv_w[l], conv_b, prev_f, tm=tm, seq_rows=t_new,
                                      sample=True)
        y = _ffn_down(x1, act, w_down_b, g_fin, tm=tm, final_norm=last)
        outs_s[0].append(k.reshape(db, t_new, n_att, HEAD_DIM))
        outs_s[1].append(v.reshape(db, t_new, n_att, HEAD_DIM))
        outs_s[2].append(s_new)
        outs_s[3].append(gq.reshape(db, t_new, 3 * w_gdn)[:, t_new - (GDN_CONV - 1):])
        outs_s[4].append(jnp.stack([jnp.concatenate([g0, v0], axis=-1), jnp.concatenate([g1, v1], axis=-1)], axis=1))
        xs = y

    return (xp.reshape(b, s, d), xs.reshape(db, t_new, d),
            *[jnp.stack(o) for o in outs_p], *[jnp.stack(o) for o in outs_s])
```

```python
import functools
import math

import jax
import jax.numpy as jnp
import numpy as np
from jax import lax
from jax.experimental import pallas as pl
from jax.experimental.pallas import tpu as pltpu

F32 = jnp.float32
BF16 = jnp.bfloat16
HI = lax.Precision.HIGHEST

HEAD_DIM = 128
MOBA_BLOCK = 256
MOBA_TOP_K = 3
PAGE_SIZE = 128
GDN_CONV = 4
GDN_CHUNK = 64
FFN_CONV = 3
NUM_BUCKETS = 32
MAX_DISTANCE = 128
EPS = 1e-6
NEG = -1e30

LANES = 128
SUBLANES = 8
VMEM_LIMIT = 56 * 1024 * 1024


def _bucket_thresholds():
    n = np.arange(0, 4 * MAX_DISTANCE)
    max_exact = NUM_BUCKETS // 2
    nf = np.maximum(n, 1).astype(np.float32)
    large = max_exact + (np.log(nf / np.float32(max_exact)) / np.float32(math.log(MAX_DISTANCE / max_exact))
                         * np.float32(NUM_BUCKETS - max_exact)).astype(np.int32)
    large = np.minimum(large, NUM_BUCKETS - 1)
    b = np.where(n < max_exact, n, large)
    return [int(np.argmax(b >= k)) for k in range(1, NUM_BUCKETS)]


BUCKET_THR = _bucket_thresholds()


def _dot(a, b, precision=None):
    return jnp.dot(a, b, preferred_element_type=F32, precision=precision)


def _dot_nt(a, b, precision=None):
    return lax.dot_general(a, b, (((1,), (1,)), ((), ())), preferred_element_type=F32, precision=precision)


def _dot_tn(a, b, precision=None):
    return lax.dot_general(a, b, (((0,), (0,)), ((), ())), preferred_element_type=F32, precision=precision)


def _silu(x):
    return x * jax.nn.sigmoid(x)


def _softplus(x):
    return jnp.maximum(x, 0.0) + jnp.log1p(jnp.exp(-jnp.abs(x)))


def _params(*semantics):
    return pltpu.CompilerParams(dimension_semantics=semantics, vmem_limit_bytes=VMEM_LIMIT)


def _rmsnorm_rows(load, g, store, rows, chunk=256):
    def body(r, carry):
        sl = pl.ds(pl.multiple_of(r * chunk, chunk), chunk)
        x = load(sl)
        ms = jnp.mean(x * x, axis=-1, keepdims=True)
        store(sl, x * lax.rsqrt(ms + EPS) * g)
        return carry
    lax.fori_loop(0, rows // chunk, body, 0)


def _inproj_kernel(x_ref, g_ref, w_ref, wab_ref, *rest, tn, offsets):
    out_refs, ab_ref = rest[:-1], rest[-1]
    x = x_ref[...]
    ms = jnp.mean(x * x, axis=-1, keepdims=True)
    h = (x * lax.rsqrt(ms + EPS) * g_ref[...]).astype(BF16)
    ab_ref[...] = _dot(h, wab_ref[...])
    for off, ref in zip(offsets, out_refs):
        for c in range(0, ref.shape[1], tn):
            ref[:, c:c + tn] = _dot(h, w_ref[:, off + c:off + c + tn])


def _inproj(x2d, g, w_main, w_ab, widths, *, tm, tn):
    t, d = x2d.shape
    assert t % tm == 0 and all(w % tn == 0 for w in widths) and sum(widths) <= w_main.shape[1]
    offsets = [sum(widths[:n]) for n in range(len(widths))]
    resident = lambda shape: pl.BlockSpec(shape, lambda i: (0, 0), pipeline_mode=pl.Buffered(1))
    return pl.pallas_call(
        functools.partial(_inproj_kernel, tn=tn, offsets=tuple(offsets)),
        grid=(t // tm,),
        in_specs=[pl.BlockSpec((tm, d), lambda i: (i, 0)), resident((1, d)),
                  resident(w_main.shape), resident((d, LANES))],
        out_specs=[pl.BlockSpec((tm, w), lambda i: (i, 0)) for w in widths]
        + [pl.BlockSpec((tm, LANES), lambda i: (i, 0))],
        out_shape=[jax.ShapeDtypeStruct((t, w), F32) for w in widths] + [jax.ShapeDtypeStruct((t, LANES), F32)],
        compiler_params=_params("parallel"),
        name="inproj",
    )(x2d, g, w_main, w_ab)


def _bias_chain(dist, value_of_bucket):
    val = value_of_bucket(0)
    for k in range(1, NUM_BUCKETS):
        val = jnp.where(dist >= BUCKET_THR[k - 1], value_of_bucket(k), val)
    return val


def _bias_prompt_kernel(rb_ref, tiles_ref, *, blk):
    h = pl.program_id(0)
    qi = lax.broadcasted_iota(jnp.int32, (blk, blk), 0)
    kj = lax.broadcasted_iota(jnp.int32, (blk, blk), 1)
    far = rb_ref[NUM_BUCKETS - 1, h]
    for delta in range(2):
        dist = delta * blk + qi - kj
        val = _bias_chain(dist, lambda k: rb_ref[k, h]) - far
        if delta == 0:
            val = jnp.where(dist >= 0, val, NEG)
        tiles_ref[0, delta] = val


def _bias_prompt(rel_bias, blk):
    nh = rel_bias.shape[1]
    assert blk + 1 >= BUCKET_THR[-1]
    return pl.pallas_call(
        functools.partial(_bias_prompt_kernel, blk=blk),
        grid=(nh,),
        in_specs=[pl.BlockSpec(memory_space=pltpu.SMEM)],
        out_specs=pl.BlockSpec((1, 2, blk, blk), lambda h: (h, 0, 0, 0)),
        out_shape=jax.ShapeDtypeStruct((nh, 2, blk, blk), F32),
        compiler_params=_params("arbitrary"),
        name="bias_prompt",
    )(rel_bias)


def _bias_sample_kernel(rbx_ref, last_ref, own_ref, far_ref, *, blk, t_new):
    tok = lax.broadcasted_iota(jnp.int32, (blk, LANES), 1) & (t_new - 1)
    pos = lax.broadcasted_iota(jnp.int32, (blk, LANES), 0)
    row = lambda k: rbx_ref[k:k + 1, :]
    last_ref[...] = _bias_chain(blk + tok - pos, row)
    tok = lax.broadcasted_iota(jnp.int32, (LANES, LANES), 1) & (t_new - 1)
    pos = lax.broadcasted_iota(jnp.int32, (LANES, LANES), 0)
    dist = tok - pos
    own_ref[...] = jnp.where((dist >= 0) & (pos < t_new), _bias_chain(dist, row), NEG)
    far_ref[...] = jnp.broadcast_to(row(NUM_BUCKETS - 1), (SUBLANES, LANES))


def _bias_sample(rel_bias, blk, t_new):
    nh = rel_bias.shape[1]
    assert nh * t_new <= LANES and blk >= LANES
    rbx = jnp.pad(jnp.repeat(rel_bias, t_new, axis=1), ((0, 0), (0, LANES - nh * t_new)))
    return pl.pallas_call(
        functools.partial(_bias_sample_kernel, blk=blk, t_new=t_new),
        out_shape=[jax.ShapeDtypeStruct((blk, LANES), F32), jax.ShapeDtypeStruct((LANES, LANES), F32),
                   jax.ShapeDtypeStruct((SUBLANES, LANES), F32)],
        name="bias_sample",
    )(rbx)


def _moba_prompt_kernel(q_ref, k_ref, v_ref, bias_ref, o_ref, kb_ref, vb_ref, *, nb, blk, n_sel):
    scale = HEAD_DIM ** -0.5
    s = nb * blk
    k = k_ref[0]
    kb_ref[:, 0:HEAD_DIM] = k.astype(BF16)
    pos_blk = lax.broadcasted_iota(jnp.int32, (s, LANES), 0) >> int(math.log2(blk))
    kb_ref[:, HEAD_DIM:] = jnp.where(pos_blk == lax.broadcasted_iota(jnp.int32, (s, LANES), 1), 1.0, 0.0).astype(BF16)
    vb_ref[...] = v_ref[0].astype(BF16)
    means = jnp.sum(k.reshape(nb, blk, HEAD_DIM), axis=1) * (1.0 / blk)
    means_s = _split(jnp.concatenate([means, jnp.zeros((LANES - nb, HEAD_DIM), F32)], axis=0))
    bidx = lax.broadcasted_iota(jnp.int32, (nb, blk), 0)
    for c in range(nb):
        rows = slice(c * blk, (c + 1) * blk)
        q = q_ref[0, rows, :]
        qx = (q * scale).astype(BF16)
        width = HEAD_DIM
        if c > n_sel:
            sc = jnp.where(bidx < c, _dot3(means_s, _split(q), nt=True)[:nb], NEG)
            rank = jnp.zeros((nb, blk), jnp.int32)
            for i in range(c):
                ri = sc[i:i + 1, :]
                tie = (bidx > i).astype(jnp.int32)
                rank = rank + jnp.where(ri > sc, 1, jnp.where(ri == sc, tie, 0))
            drop = jnp.where(bidx < c, jnp.where(rank < n_sel, 0.0, NEG), 0.0)
            drop = jnp.concatenate([drop, jnp.zeros((LANES - nb, blk), F32)], axis=0).T
            qx = jnp.concatenate([qx, drop.astype(BF16)], axis=1)
            width = HEAD_DIM + LANES
        parts = []
        for j in range(c):
            sj = _dot_nt(qx, kb_ref[j * blk:(j + 1) * blk, 0:width])
            parts.append(sj + bias_ref[0, 1] if j == c - 1 else sj)
        parts.append(_dot_nt(qx, kb_ref[rows, 0:width]) + bias_ref[0, 0])
        mm = parts[0]
        for p in parts[1:]:
            mm = jnp.maximum(mm, p)
        m = jnp.max(mm, axis=1, keepdims=True)
        ps = [jnp.exp(p - m) for p in parts]
        tot = ps[0]
        for p in ps[1:]:
            tot = tot + p
        l = jnp.sum(tot, axis=1, keepdims=True)
        pcat = jnp.concatenate([p.astype(BF16) for p in ps], axis=1)
        o = _dot(pcat, vb_ref[0:(c + 1) * blk, :])
        o_ref[0, rows, :] = (o / l).astype(o_ref.dtype)


def _moba_prompt(q, k, v, bias_tiles):
    b, s, w = q.shape
    nh = w // HEAD_DIM
    blk = MOBA_BLOCK
    assert s % blk == 0
    nb = s // blk
    n_sel = min(MOBA_TOP_K, nb - 1)
    col = pl.BlockSpec((1, s, HEAD_DIM), lambda bi, h: (bi, 0, h))
    return pl.pallas_call(
        functools.partial(_moba_prompt_kernel, nb=nb, blk=blk, n_sel=n_sel),
        grid=(b, nh),
        in_specs=[col, col, col,
                  pl.BlockSpec((1, 2, blk, blk), lambda bi, h: (h, 0, 0, 0))],
        out_specs=col,
        out_shape=jax.ShapeDtypeStruct((b, s, w), BF16),
        scratch_shapes=[pltpu.VMEM((s, HEAD_DIM + LANES), BF16), pltpu.VMEM((s, HEAD_DIM), BF16)],
        compiler_params=_params("parallel", "parallel"),
        name="moba_prompt",
    )(q, k, v, bias_tiles)


def _moba_sample_kernel(pt_ref, q_ref, kn_ref, vn_ref, blast_ref, bown_ref, bfar_ref, *rest,
                        n_pages, ppb, n_sel):
    del pt_ref
    kps, vps, o_ref = rest[:n_pages], rest[n_pages:2 * n_pages], rest[2 * n_pages]
    scale = HEAD_DIM ** -0.5
    t_new, width = q_ref.shape
    nh = width // HEAD_DIM
    cur = n_pages // ppb
    blk = ppb * PAGE_SIZE
    rid = lax.broadcasted_iota(jnp.int32, (LANES, width), 0)
    cid = lax.broadcasted_iota(jnp.int32, (LANES, width), 1)
    q_rep = jnp.concatenate([q_ref[...]] * (LANES // t_new), axis=0)
    qt = jnp.where((rid >> int(math.log2(t_new))) == (cid >> int(math.log2(HEAD_DIM))), q_rep, 0.0)
    qt_s = (qt * scale).astype(BF16)

    def page_matrix(ref):
        return jnp.concatenate([ref[0, pl.ds(h, PAGE_SIZE, stride=nh), :] for h in range(nh)], axis=1)

    logits, ksums = [], []
    for j in range(n_pages):
        kj = page_matrix(kps[j])
        logits.append(_dot_nt(kj.astype(BF16), qt_s))
        ksums.append(jnp.sum(kj, axis=0, keepdims=True))
    means = []
    for jb in range(cur):
        tot = ksums[jb * ppb]
        for r in range(1, ppb):
            tot = tot + ksums[jb * ppb + r]
        means.append(tot * (1.0 / blk))
    sc = _dot_nt(jnp.concatenate(means, axis=0), qt, HI)
    bidx = lax.broadcasted_iota(jnp.int32, (cur, LANES), 0)
    rank = jnp.zeros((cur, LANES), jnp.int32)
    for i in range(cur):
        ri = sc[i:i + 1, :]
        tie = (bidx > i).astype(jnp.int32)
        rank = rank + jnp.where(ri > sc, 1, jnp.where(ri == sc, tie, 0))

    parts = []
    for jb in range(cur):
        sb = jnp.concatenate(logits[jb * ppb:(jb + 1) * ppb], axis=0)
        sb = sb + (blast_ref[...] if jb == cur - 1 else bfar_ref[0:1, :])
        parts.append(jnp.where(rank[jb:jb + 1, :] < n_sel, sb, NEG))
    pad = jnp.zeros((LANES - t_new, width), F32)
    kn = jnp.concatenate([kn_ref[...], pad], axis=0)
    vn = jnp.concatenate([vn_ref[...], pad], axis=0)
    parts.append(_dot_nt(kn.astype(BF16), qt_s) + bown_ref[...])
    mm = parts[0]
    for p in parts[1:-1]:
        mm = jnp.maximum(mm, p)
    m = jnp.maximum(jnp.max(mm, axis=0, keepdims=True), jnp.max(parts[-1], axis=0, keepdims=True))
    acc = jnp.zeros((LANES, width), F32)
    lsum = jnp.zeros((LANES, 1), F32)
    for jb in range(cur + 1):
        pt = jnp.exp(parts[jb] - m).T
        lsum = lsum + jnp.sum(pt, axis=1, keepdims=True)
        if jb < cur:
            vb = jnp.concatenate([page_matrix(vps[jb * ppb + r]).astype(BF16) for r in range(ppb)], axis=0)
        else:
            vb = vn.astype(BF16)
        acc = acc + _dot(pt.astype(BF16), vb)
    out = acc / lsum
    for h in range(nh):
        o_ref[:, h * HEAD_DIM:(h + 1) * HEAD_DIM] = out[h * t_new:(h + 1) * t_new, h * HEAD_DIM:(h + 1) * HEAD_DIM]


def _moba_sample(q, k_new, v_new, cache_k, cache_v, page_table, blast, bown, bfar, *, t_new):
    rows, width = q.shape
    db = rows // t_new
    n_pool = cache_k.shape[0]
    n_pages = page_table.shape[1]
    ppb = MOBA_BLOCK // PAGE_SIZE
    assert n_pages % ppb == 0, "new tokens must start a fresh MoBA block"
    assert t_new == SUBLANES
    cur = n_pages // ppb
    n_sel = min(MOBA_TOP_K, cur)
    assert n_sel > 0
    nh = width // HEAD_DIM
    ck = cache_k.reshape(n_pool, PAGE_SIZE * nh, HEAD_DIM)
    cv = cache_v.reshape(n_pool, PAGE_SIZE * nh, HEAD_DIM)
    new = pl.BlockSpec((t_new, width), lambda s, pt: (s, 0))
    const = lambda shape: pl.BlockSpec(shape, lambda s, pt: (0, 0))
    page = lambda j: pl.BlockSpec((1, PAGE_SIZE * nh, HEAD_DIM), lambda s, pt, j=j: (pt[s, j], 0, 0))
    grid_spec = pltpu.PrefetchScalarGridSpec(
        num_scalar_prefetch=1,
        grid=(db,),
        in_specs=[new, new, new, const(blast.shape), const(bown.shape), const(bfar.shape)]
        + [page(j) for j in range(n_pages)] * 2,
        out_specs=new,
    )
    return pl.pallas_call(
        functools.partial(_moba_sample_kernel, n_pages=n_pages, ppb=ppb, n_sel=n_sel),
        grid_spec=grid_spec,
        out_shape=jax.ShapeDtypeStruct((rows, width), F32),
        compiler_params=_params("parallel"),
        name="moba_sample",
    )(page_table, q, k_new, v_new, blast, bown, bfar, *([ck] * n_pages), *([cv] * n_pages))


def _split(x):
    hi = x.astype(BF16)
    return hi, (x - hi.astype(F32)).astype(BF16)


def _dot3(a, b, nt=False):
    lhs = jnp.concatenate([a[0], a[0], a[1]], axis=1)
    if nt:
        return _dot_nt(lhs, jnp.concatenate([b[0], b[1], b[0]], axis=1))
    return _dot(lhs, jnp.concatenate([b[0], b[1], b[0]], axis=0))


def _gdn_kernel(alog_ref, dtb_ref, q_ref, k_ref, v_ref, z_ref, ab_ref, cwq_ref, cwk_ref, cwv_ref,
                pq_ref, pk_ref, pv_ref, nw_ref, s0_ref, o_ref, so_ref, *scratch, tt, chunk, sample, n_heads, hp):
    assert hp == n_heads
    n_chunks = tt // chunk
    ins = [q_ref, k_ref, v_ref]
    cw_refs = [cwq_ref, cwk_ref, cwv_ref]
    prevs = [pq_ref, pk_ref, pv_ref]
    rid8 = lax.broadcasted_iota(jnp.int32, (SUBLANES, HEAD_DIM), 0)
    if sample:
        (shift_scr,) = scratch
    else:
        s_scr, tail_scr = scratch
        i = pl.program_id(2)

        @pl.when(i == 0)
        def _():
            s_scr[...] = s0_ref[0]
            for n in range(3):
                tail_scr[n] = prevs[n][0]

    ri = lax.broadcasted_iota(jnp.int32, (tt, tt), 0)
    ci = lax.broadcasted_iota(jnp.int32, (tt, tt), 1)
    shift_c = int(math.log2(chunk))
    same_block = lambda bits: (ri >> bits) == (ci >> bits)
    same = same_block(shift_c)
    incl = same & (ri >= ci)
    strict = same & (ri > ci)
    incl_t = same & (ri <= ci)
    eye = jnp.where(ri == ci, 1.0, 0.0)
    base_bits = int(math.log2(SUBLANES))
    base = same_block(base_bits)
    level_masks = [same_block(bits + 1) & (ri >> bits != ci >> bits) for bits in range(base_bits, shift_c)]
    sum_mat = jnp.concatenate([jnp.where(incl, 1.0, 0.0), jnp.where(same, 1.0, 0.0)], axis=0).astype(BF16)
    sum_mat3 = jnp.concatenate([sum_mat] * 3, axis=1)
    ab = ab_ref[0]
    heads = range(hp)
    cols = [slice(hh * HEAD_DIM, (hh + 1) * HEAD_DIM) for hh in heads]
    chunks = [slice(c * chunk, (c + 1) * chunk) for c in range(n_chunks)]


    xq, xk, xv = [], [], []
    for hh in heads:
        for n, dst in enumerate((xq, xk, xv)):
            x = ins[n][0, :, cols[hh]]
            w = cw_refs[n][:, cols[hh]]
            acc = None
            for shift in range(GDN_CONV - 1, 0, -1):
                rolled = pltpu.roll(x, shift, axis=0)
                if sample:
                    slot = (hh * 3 + n) * (GDN_CONV - 1) + shift - 1
                    shift_scr[slot] = rolled
                    for r in range(shift):
                        shift_scr[slot, pl.ds(r, n_chunks, stride=chunk), :] = (
                            prevs[n][GDN_CONV - 1 - shift + r, 0, :, cols[hh]])
                    xm = shift_scr[slot]
                else:
                    first = jnp.where(rid8 < shift, pltpu.roll(tail_scr[n, :, cols[hh]], shift, axis=0),
                                      rolled[:SUBLANES])
                    xm = jnp.concatenate([first, rolled[SUBLANES:]], axis=0)
                tap = GDN_CONV - 1 - shift
                term = xm * w[tap:tap + 1, :]
                acc = term if acc is None else acc + term
            acc = acc + x * w[GDN_CONV - 1:GDN_CONV, :]
            dst.append(_silu(acc))
            if not sample:
                tail_scr[n, :, cols[hh]] = x[tt - SUBLANES:]

    g_all = -jnp.exp(alog_ref[...]) * _softplus(ab + dtb_ref[...])
    beta_all = jax.nn.sigmoid(ab)
    g0 = g_all.astype(BF16)
    r1 = g_all - g0.astype(F32)
    g1 = r1.astype(BF16)
    g2 = (r1 - g1.astype(F32)).astype(BF16)
    sums = _dot(sum_mat3, jnp.concatenate([g0, g1, g2], axis=0))
    g_col = [g_all[:, hh:hh + 1] for hh in heads]
    beta = [beta_all[:, n_heads + hh:n_heads + hh + 1] for hh in heads]
    cg = [sums[:tt, hh:hh + 1] for hh in heads]
    gl = [sums[tt:, hh:hh + 1] for hh in heads]
    decay = []
    for hh in heads:
        cg_row = jnp.sum(jnp.where(incl_t, g_col[hh], 0.0), axis=0, keepdims=True)
        decay.append(jnp.exp(jnp.where(incl, cg[hh] - cg_row, NEG)))

    qn = [xq[hh] * lax.rsqrt(jnp.sum(xq[hh] * xq[hh], axis=-1, keepdims=True) + 1e-6) * (HEAD_DIM ** -0.5)
          for hh in heads]
    kn = [xk[hh] * lax.rsqrt(jnp.sum(xk[hh] * xk[hh], axis=-1, keepdims=True) + 1e-6) for hh in heads]
    kn_s = [_split(kn[hh]) for hh in heads]
    kk = [_dot3(kn_s[hh], kn_s[hh], nt=True) for hh in heads]
    lmat = [jnp.where(strict, beta[hh] * decay[hh] * kk[hh], 0.0) for hh in heads]

    xp = [-jnp.where(base, lmat[hh], 0.0) for hh in heads]
    tinv = [eye + xp[hh] for hh in heads]
    xp_s = [_split(xp[hh]) for hh in heads]
    for _ in range(base_bits - 1):
        xp_s = [_split(_dot3(xp_s[hh], xp_s[hh])) for hh in heads]
        tinv = [tinv[hh] + _dot3(_split(tinv[hh]), xp_s[hh]) for hh in heads]
    l_cat = [jnp.concatenate(_split(lmat[hh]), axis=1) for hh in heads] if level_masks else None
    for mask in level_masks:
        t_hi = [tinv[hh].astype(BF16) for hh in heads]
        lt = [_split(_dot(l_cat[hh], jnp.concatenate([t_hi[hh], t_hi[hh]], axis=0))) for hh in heads]
        tinv = [tinv[hh] - jnp.where(mask, _dot(jnp.concatenate([t_hi[hh], t_hi[hh]], axis=1),
                                                jnp.concatenate(lt[hh], axis=0)), 0.0) for hh in heads]

    eg = [jnp.exp(cg[hh]) for hh in heads]
    wu = [_dot3(_split(tinv[hh]),
                _split(jnp.concatenate([kn[hh] * (beta[hh] * eg[hh]), xv[hh] * beta[hh]], axis=1)))
          for hh in heads]
    u0 = [wu[hh][:, HEAD_DIM:] for hh in heads]
    amat = [jnp.where(incl, _dot_nt(qn[hh].astype(BF16), kn_s[hh][0]) * decay[hh], 0.0).astype(BF16)
            for hh in heads]
    wq = [[jnp.concatenate([wu[hh][cs, :HEAD_DIM], (qn[hh] * eg[hh])[cs]], axis=0).astype(BF16) for cs in chunks]
          for hh in heads]
    kd = [kn[hh] * jnp.exp(gl[hh] - cg[hh]) for hh in heads]
    egl = [jnp.exp(gl[hh]) for hh in heads]

    if sample:
        o = []
        for hh in heads:
            rs = [_dot(wq[hh][c], s0_ref[c, hh].astype(BF16)) for c in range(n_chunks)]
            u = u0[hh] - jnp.concatenate([r[:chunk] for r in rs], axis=0)
            o.append(jnp.concatenate([r[chunk:] for r in rs], axis=0) + _dot(amat[hh], u.astype(BF16)))
            for c, cs in enumerate(chunks):
                so_ref[c, hh] = (egl[hh][c * chunk:c * chunk + 1, :] * s0_ref[c, hh]
                                 + _dot_tn(kd[hh][cs], u[cs], HI))
    else:
        kd_t = [kd[hh].T.astype(BF16) for hh in heads]
        s = [s_scr[hh] for hh in heads]
        outs = [[] for _ in heads]
        for c, cs in enumerate(chunks):
            for hh in heads:
                r = _dot(wq[hh][c], s[hh].astype(BF16))
                u = (u0[hh][cs] - r[:chunk]).astype(BF16)
                pieces = [u]
                if c > 0:
                    pieces.insert(0, jnp.zeros((c * chunk, HEAD_DIM), BF16))
                if c < n_chunks - 1:
                    pieces.append(jnp.zeros((tt - (c + 1) * chunk, HEAD_DIM), BF16))
                u_full = jnp.concatenate(pieces, axis=0)
                outs[hh].append(r[chunk:] + _dot(amat[hh][cs, :], u_full))
                s[hh] = egl[hh][c * chunk:c * chunk + 1, :] * s[hh] + _dot(kd_t[hh], u_full)
        for hh in heads:
            s_scr[hh] = s[hh]
            so_ref[0, hh] = s[hh]
        o = [jnp.concatenate(outs[hh], axis=0) for hh in heads]

    for hh in heads:
        z = z_ref[0, :, cols[hh]]
        on = (o[hh] * lax.rsqrt(jnp.mean(o[hh] * o[hh], axis=-1, keepdims=True) + EPS) * nw_ref[...] * _silu(z))
        o_ref[0, :, cols[hh]] = on.astype(o_ref.dtype)


def _gdn(gqkv, z, ab, conv_w, prev, a_log, dt_bias, norm_w, s0, *, sample, tt):
    g, r, w3 = gqkv.shape
    w = w3 // 3
    nh = w // HEAD_DIM
    hp = nh
    assert r % tt == 0
    ng = 1
    wb = hp * HEAD_DIM
    smem = pl.BlockSpec((1, LANES), lambda *_: (0, 0))
    if sample:
        assert r == tt
        chunk = SUBLANES
        seqs = tt // chunk
        grid = (g, ng)
        col = lambda off: pl.BlockSpec((1, tt, wb), lambda gi, h, off=off: (gi, 0, off + h))
        prev_spec = lambda off: pl.BlockSpec((GDN_CONV - 1, 1, seqs, wb), lambda gi, h, off=off: (0, gi, 0, off + h))
        state = pl.BlockSpec((seqs, hp, HEAD_DIM, HEAD_DIM), lambda gi, h: (gi, h, 0, 0))
        cw = lambda off: pl.BlockSpec((GDN_CONV, wb), lambda gi, h, off=off: (0, off + h))
        row = pl.BlockSpec((1, HEAD_DIM), lambda gi, h: (0, 0))
        abs_ = pl.BlockSpec((1, tt, LANES), lambda gi, h: (gi, 0, 0))
        scratch = [pltpu.VMEM((hp * 3 * (GDN_CONV - 1), tt, HEAD_DIM), F32)]
        sem = ("parallel", "parallel")
    else:
        chunk = GDN_CHUNK
        grid = (g, ng, r // tt)
        col = lambda off: pl.BlockSpec((1, tt, wb), lambda gi, h, i, off=off: (gi, i, off + h))
        prev_spec = lambda off: pl.BlockSpec((1, SUBLANES, wb), lambda gi, h, i, off=off: (gi, 0, off + h))
        state = pl.BlockSpec((1, hp, HEAD_DIM, HEAD_DIM), lambda gi, h, i: (gi, h, 0, 0))
        cw = lambda off: pl.BlockSpec((GDN_CONV, wb), lambda gi, h, i, off=off: (0, off + h))
        row = pl.BlockSpec((1, HEAD_DIM), lambda gi, h, i: (0, 0))
        abs_ = pl.BlockSpec((1, tt, LANES), lambda gi, h, i: (gi, i, 0))
        scratch = [pltpu.VMEM((hp, HEAD_DIM, HEAD_DIM), F32), pltpu.VMEM((3, SUBLANES, wb), F32)]
        sem = ("parallel", "parallel", "arbitrary")
    return pl.pallas_call(
        functools.partial(_gdn_kernel, tt=tt, chunk=chunk, sample=sample, n_heads=nh, hp=hp),
        grid=grid,
        in_specs=[smem, smem, col(0), col(ng), col(2 * ng), col(0), abs_, cw(0), cw(ng), cw(2 * ng),
                  prev_spec(0), prev_spec(ng), prev_spec(2 * ng), row, state],
        out_specs=[col(0), state],
        out_shape=[jax.ShapeDtypeStruct((g, r, w), BF16), jax.ShapeDtypeStruct(s0.shape, F32)],
        scratch_shapes=scratch,
        compiler_params=_params(*sem),
        name="gdn_sample" if sample else "gdn_prompt",
    )(a_log, dt_bias, gqkv, gqkv, gqkv, z, ab, conv_w, conv_w, conv_w, prev, prev, prev, norm_w, s0)


def _outproj_kernel(x_ref, att_ref, gdn_ref, wa_ref, wb_ref, o_ref):
    o_ref[...] = (x_ref[...] + _dot(att_ref[...].astype(BF16), wa_ref[...])
                  + _dot(gdn_ref[...].astype(BF16), wb_ref[...]))


def _outproj(x2d, att, gdn, w_out, *, tm):
    t, d = x2d.shape
    wa = att.shape[1]
    assert gdn.shape[1] == wa and w_out.shape[0] == 2 * wa
    return pl.pallas_call(
        _outproj_kernel,
        grid=(t // tm,),
        in_specs=[pl.BlockSpec((tm, d), lambda i: (i, 0)), pl.BlockSpec((tm, wa), lambda i: (i, 0)),
                  pl.BlockSpec((tm, wa), lambda i: (i, 0)), pl.BlockSpec((wa, d), lambda i: (0, 0)),
                  pl.BlockSpec((wa, d), lambda i: (1, 0))],
        out_specs=pl.BlockSpec((tm, d), lambda i: (i, 0)),
        out_shape=jax.ShapeDtypeStruct((t, d), F32),
        compiler_params=_params("parallel"),
        name="outproj",
    )(x2d, att, gdn, w_out, w_out)


def _ffn_up_kernel(x_ref, g_ref, wg_ref, wv_ref, cwg_ref, cwv_ref, cbg_ref, cbv_ref, *rest,
                   tm, tf, sample, tiles_per_seq):
    if sample:
        (sg0_ref, sg1_ref, sv0_ref, sv1_ref, act_ref, og0_ref, og1_ref, ov0_ref, ov1_ref,
         h_ref, buf_g, buf_v) = rest
    else:
        act_ref, ug_ref, uv_ref, h_ref, carry_g, carry_v, buf_g, buf_v = rest
    i = pl.program_id(0)
    f = pl.program_id(1)

    @pl.when(f == 0)
    def _():
        def store(sl, y):
            h_ref[sl, :] = y.astype(BF16)
        _rmsnorm_rows(lambda sl: x_ref[sl, :], g_ref[...], store, tm)

    hb = h_ref[...]

    def conv(w_ref, cw_ref, cb_ref, state_refs, carry_ref, buf_ref, out_ref):
        cw = cw_ref[...]
        if sample:
            n_seq = tm // SUBLANES
            seq_row = lambda r: pl.ds(r, n_seq, stride=SUBLANES)
            u = _dot(hb, w_ref[...])
            n_col = tf // LANES
            for c in range(n_col):
                cs = slice(c * LANES, (c + 1) * LANES)
                buf_ref[0, c] = u[:, cs]
                for r, o_ref in enumerate(out_ref):
                    o_ref[:, cs] = buf_ref[0, c, seq_row(SUBLANES - (FFN_CONV - 1) + r), :]
                for shift in range(1, FFN_CONV):
                    buf_ref[shift, c] = pltpu.roll(u[:, cs], shift, axis=0)
                    for r in range(shift):
                        buf_ref[shift, c, seq_row(r), :] = state_refs[FFN_CONV - 1 - shift + r][:, cs]
            shifted = {shift: jnp.concatenate([buf_ref[shift, c] for c in range(n_col)], axis=1)
                       for shift in range(1, FFN_CONV)}
            y = shifted[2] * cw[0:1, :] + shifted[1] * cw[1:2, :] + u * cw[2:3, :]
        else:
            buf_ref[0:SUBLANES, :] = jnp.where(i % tiles_per_seq == 0, 0.0, carry_ref[f])
            buf_ref[SUBLANES:, :] = _dot(hb, w_ref[...])
            last = buf_ref[tm:, :]
            carry_ref[f] = last
            out_ref[0] = last
            y = (buf_ref[SUBLANES - 2:SUBLANES - 2 + tm, :] * cw[0:1, :]
                 + buf_ref[SUBLANES - 1:SUBLANES - 1 + tm, :] * cw[1:2, :]
                 + buf_ref[SUBLANES:, :] * cw[2:3, :])
        return y + cb_ref[...]

    if sample:
        yg = conv(wg_ref, cwg_ref, cbg_ref, (sg0_ref, sg1_ref), None, buf_g, (og0_ref, og1_ref))
        yv = conv(wv_ref, cwv_ref, cbv_ref, (sv0_ref, sv1_ref), None, buf_v, (ov0_ref, ov1_ref))
    else:
        yg = conv(wg_ref, cwg_ref, cbg_ref, None, carry_g, buf_g, ug_ref)
        yv = conv(wv_ref, cwv_ref, cbv_ref, None, carry_v, buf_v, uv_ref)
    act_ref[...] = (_silu(yg) * yv).astype(BF16)


def _ffn_up(x2d, g, w_up, conv_w, conv_b, prev, *, tm, seq_rows, sample):
    t, d = x2d.shape
    dff = w_up.shape[1] // 2
    tf = 512
    nf = dff // tf
    assert dff % tf == 0 and t % tm == 0
    gate = lambda shape: pl.BlockSpec(shape, lambda i, f: (0, f))
    val = lambda shape: pl.BlockSpec(shape, lambda i, f: (0, nf + f))
    in_specs = [pl.BlockSpec((tm, d), lambda i, f: (i, 0)), pl.BlockSpec((1, d), lambda i, f: (0, 0)),
                gate((d, tf)), val((d, tf)), gate((FFN_CONV, tf)), val((FFN_CONV, tf)),
                gate((1, tf)), val((1, tf))]
    args = [x2d, g, w_up, w_up, conv_w, conv_w, conv_b, conv_b]
    scratch = [pltpu.VMEM((tm, d), BF16)]
    if sample:
        assert seq_rows == SUBLANES and FFN_CONV == 3
        tiles_per_seq = 0
        n_seq = tm // SUBLANES
        in_specs += [pl.BlockSpec((n_seq, tf), lambda i, f: (i, f))] * 2
        in_specs += [pl.BlockSpec((n_seq, tf), lambda i, f: (i, nf + f))] * 2
        args += [prev[0], prev[1], prev[0], prev[1]]
        u_specs = [pl.BlockSpec((n_seq, tf), lambda i, f: (i, f))] * 4
        u_shapes = [jax.ShapeDtypeStruct((t // SUBLANES, dff), F32)] * 4
        scratch += [pltpu.VMEM((FFN_CONV, tf // LANES, tm, LANES), F32)] * 2
    else:
        assert seq_rows % tm == 0
        tiles_per_seq = seq_rows // tm
        u_specs = [pl.BlockSpec((1, SUBLANES, tf), lambda i, f: (i, 0, f))] * 2
        u_shapes = [jax.ShapeDtypeStruct((t // tm, SUBLANES, dff), F32)] * 2
        scratch += [pltpu.VMEM((nf, SUBLANES, tf), F32)] * 2 + [pltpu.VMEM((SUBLANES + tm, tf), F32)] * 2
    return pl.pallas_call(
        functools.partial(_ffn_up_kernel, tm=tm, tf=tf, sample=sample, tiles_per_seq=tiles_per_seq),
        grid=(t // tm, nf),
        in_specs=in_specs,
        out_specs=[pl.BlockSpec((tm, tf), lambda i, f: (i, f))] + u_specs,
        out_shape=[jax.ShapeDtypeStruct((t, dff), BF16)] + u_shapes,
        scratch_shapes=scratch,
        compiler_params=_params("arbitrary", "arbitrary"),
        name="ffn_up_sample" if sample else "ffn_up_prompt",
    )(*args)


def _ffn_down_kernel(x_ref, act_ref, wd_ref, gf_ref, y_ref, *, tm, tn, final_norm):
    act = act_ref[...]
    for c in range(0, y_ref.shape[1], tn):
        y_ref[:, c:c + tn] = x_ref[:, c:c + tn] + _dot(act, wd_ref[:, c:c + tn])
    if final_norm:
        def store(sl, y):
            y_ref[sl, :] = y
        _rmsnorm_rows(lambda sl: y_ref[sl, :], gf_ref[...], store, tm)


def _ffn_down(x2d, act, w_down, gf, *, tm, final_norm):
    t, d = x2d.shape
    dff = w_down.shape[0]
    tn = 512
    assert d % tn == 0 and t % tm == 0
    resident = lambda shape: pl.BlockSpec(shape, lambda i: (0, 0), pipeline_mode=pl.Buffered(1))
    return pl.pallas_call(
        functools.partial(_ffn_down_kernel, tm=tm, tn=tn, final_norm=final_norm),
        grid=(t // tm,),
        in_specs=[pl.BlockSpec((tm, d), lambda i: (i, 0)), pl.BlockSpec((tm, dff), lambda i: (i, 0)),
                  resident((dff, d)), resident((1, d))],
        out_specs=pl.BlockSpec((tm, d), lambda i: (i, 0)),
        out_shape=jax.ShapeDtypeStruct((t, d), F32),
        compiler_params=_params("parallel"),
        name="ffn_down",
    )(x2d, act, w_down, gf)


def _layer_slice(x, l):
    return x.reshape(x.shape[1:]) if x.shape[0] == 1 else x[l]


def kernel(x_prompt, x_sample, cache_k, cache_v, page_table, state_gdn, state_gdn_conv, state_ffn_conv,
           rms_mix_w, w_in, gdn_conv_w, gdn_a_log, gdn_dt_bias, gdn_norm_w, w_out, rms_ffn_w, w_up,
           ffn_conv_w, ffn_conv_b, w_down, rel_bias, final_norm_w):
    b, s, d = x_prompt.shape
    db, t_new, _ = x_sample.shape
    depth = w_in.shape[0]
    n_att = cache_k.shape[3]
    w_att = n_att * HEAD_DIM
    n_gdn = gdn_a_log.shape[1]
    w_gdn = n_gdn * HEAD_DIM
    dff = w_down.shape[1]
    assert w_att == w_gdn == 1024 and w_in.shape[2] == 3 * w_att + 4 * w_gdn + 2 * n_gdn
    n_main = 3 * w_att + 4 * w_gdn
    n_pool = cache_k.shape[1]
    tt_s = 128
    grp = tt_s // t_new
    tm = 512

    bias_tiles = _bias_prompt(rel_bias, MOBA_BLOCK)
    blast, bown, bfar = _bias_sample(rel_bias, MOBA_BLOCK, t_new)

    xp = x_prompt.reshape(b * s, d)
    xs = x_sample.reshape(db * t_new, d)
    outs_p = [[] for _ in range(5)]
    outs_s = [[] for _ in range(5)]
    for l in range(depth):
        w_in_b = w_in[l].astype(BF16)
        w_main = w_in_b
        w_ab = jnp.pad(w_in_b[:, n_main:], ((0, 0), (0, LANES - 2 * n_gdn)))
        w_out_b = w_out[l].astype(BF16)
        w_up_b = w_up[l].astype(BF16)
        w_down_b = w_down[l].astype(BF16)
        g_mix = rms_mix_w[l].reshape(1, d)
        g_ffn = rms_ffn_w[l].reshape(1, d)
        g_fin = final_norm_w.reshape(1, d)
        norm_w = gdn_norm_w[l].reshape(1, HEAD_DIM)
        a_log_row = jnp.pad(gdn_a_log[l].reshape(1, n_gdn), ((0, 0), (0, LANES - n_gdn)))
        dt_bias_row = jnp.pad(gdn_dt_bias[l].reshape(1, n_gdn), ((0, 0), (0, LANES - n_gdn)))
        conv_b = ffn_conv_b[l].reshape(1, 2 * dff)
        last = l == depth - 1

        widths = (w_att, w_att, w_att, 3 * w_gdn, w_gdn)
        q, k, v, gq, z, ab = _inproj(xp, g_mix, w_main, w_ab, widths, tm=tm // 2, tn=512)
        att = _moba_prompt(q.reshape(b, s, w_att), k.reshape(b, s, w_att), v.reshape(b, s, w_att),
                           bias_tiles)
        gdn, s_new = _gdn(gq.reshape(b, s, 3 * w_gdn), z.reshape(b, s, w_gdn), ab.reshape(b, s, LANES),
                          gdn_conv_w[l], jnp.zeros((b, SUBLANES, 3 * w_gdn), F32), a_log_row, dt_bias_row,
                          norm_w, jnp.zeros((b, n_gdn, HEAD_DIM, HEAD_DIM), F32), sample=False, tt=256)
        x1 = _outproj(xp, att.reshape(b * s, w_att), gdn.reshape(b * s, w_gdn), w_out_b, tm=tm)
        act, ug, uv = _ffn_up(x1, g_ffn, w_up_b, ffn_conv_w[l], conv_b, None, tm=2 * tm, seq_rows=s, sample=False)
        y = _ffn_down(x1, act, w_down_b, g_fin, tm=tm, final_norm=last)
        outs_p[0].append(k.reshape(b, s, n_att, HEAD_DIM))
        outs_p[1].append(v.reshape(b, s, n_att, HEAD_DIM))
        outs_p[2].append(s_new)
        outs_p[3].append(gq.reshape(b, s, 3 * w_gdn)[:, s - (GDN_CONV - 1):])
        tps = s // (2 * tm)
        outs_p[4].append(jnp.concatenate([ug, uv], axis=-1)[tps - 1::tps, SUBLANES - (FFN_CONV - 1):])
        xp = y

        ts = db * t_new
        q, k, v, gq, z, ab = _inproj(xs, g_mix, w_main, w_ab, widths, tm=tm // 2, tn=512)
        att = _moba_sample(q, k, v, cache_k.reshape((depth * n_pool,) + cache_k.shape[2:]),
                           cache_v.reshape((depth * n_pool,) + cache_v.shape[2:]), page_table + l * n_pool,
                           blast, bown, bfar, t_new=t_new)
        prev_g = jnp.moveaxis(state_gdn_conv[l], 1, 0).reshape(GDN_CONV - 1, db // grp, grp, 3 * w_gdn)
        gdn, s_new = _gdn(gq.reshape(db // grp, tt_s, 3 * w_gdn), z.reshape(db // grp, tt_s, w_gdn),
                          ab.reshape(db // grp, tt_s, LANES), gdn_conv_w[l], prev_g, a_log_row, dt_bias_row,
                          norm_w, _layer_slice(state_gdn, l), sample=True, tt=tt_s)
        x1 = _outproj(xs, att, gdn.reshape(ts, w_gdn), w_out_b, tm=tm)
        prev_f = tuple(state_ffn_conv[l, :, r] for r in range(FFN_CONV - 1))
        act, g0, g1, v0, v1 = _ffn_up(x1, g_ffn, w_up_b, ffn_conv_w[l], conv_b, prev_f, tm=tm, seq_rows=t_new,
                                      sample=True)
        y = _ffn_down(x1, act, w_down_b, g_fin, tm=tm, final_norm=last)
        outs_s[0].append(k.reshape(db, t_new, n_att, HEAD_DIM))
        outs_s[1].append(v.reshape(db, t_new, n_att, HEAD_DIM))
        outs_s[2].append(s_new)
        outs_s[3].append(gq.reshape(db, t_new, 3 * w_gdn)[:, t_new - (GDN_CONV - 1):])
        outs_s[4].append(jnp.stack([jnp.concatenate([g0, v0], axis=-1), jnp.concatenate([g1, v1], axis=-1)], axis=1))
        xs = y

    return (xp.reshape(b, s, d), xs.reshape(db, t_new, d),
            *[jnp.stack(o) for o in outs_p], *[jnp.stack(o) for o in outs_s])
```

```python
import functools
import math

import jax
import jax.numpy as jnp
import numpy as np
from jax import lax
from jax.experimental import pallas as pl
from jax.experimental.pallas import tpu as pltpu

F32 = jnp.float32
BF16 = jnp.bfloat16
HI = lax.Precision.HIGHEST

HEAD_DIM = 128
MOBA_BLOCK = 256
MOBA_TOP_K = 3
PAGE_SIZE = 128
GDN_CONV = 4
GDN_CHUNK = 64
FFN_CONV = 3
NUM_BUCKETS = 32
MAX_DISTANCE = 128
EPS = 1e-6
NEG = -1e30

LANES = 128
SUBLANES = 8
VMEM_LIMIT = 56 * 1024 * 1024


def _bucket_thresholds():
    n = np.arange(0, 4 * MAX_DISTANCE)
    max_exact = NUM_BUCKETS // 2
    nf = np.maximum(n, 1).astype(np.float32)
    large = max_exact + (np.log(nf / np.float32(max_exact)) / np.float32(math.log(MAX_DISTANCE / max_exact))
                         * np.float32(NUM_BUCKETS - max_exact)).astype(np.int32)
    large = np.minimum(large, NUM_BUCKETS - 1)
    b = np.where(n < max_exact, n, large)
    return [int(np.argmax(b >= k)) for k in range(1, NUM_BUCKETS)]


BUCKET_THR = _bucket_thresholds()


def _dot(a, b, precision=None):
    return jnp.dot(a, b, preferred_element_type=F32, precision=precision)


def _dot_nt(a, b, precision=None):
    return lax.dot_general(a, b, (((1,), (1,)), ((), ())), preferred_element_type=F32, precision=precision)


def _dot_tn(a, b, precision=None):
    return lax.dot_general(a, b, (((0,), (0,)), ((), ())), preferred_element_type=F32, precision=precision)


def _silu(x):
    return x * jax.nn.sigmoid(x)


def _softplus(x):
    return jnp.maximum(x, 0.0) + jnp.log1p(jnp.exp(-jnp.abs(x)))


def _params(*semantics):
    return pltpu.CompilerParams(dimension_semantics=semantics, vmem_limit_bytes=VMEM_LIMIT)


def _rmsnorm_rows(load, g, store, rows, chunk=256):
    def body(r, carry):
        sl = pl.ds(pl.multiple_of(r * chunk, chunk), chunk)
        x = load(sl)
        ms = jnp.mean(x * x, axis=-1, keepdims=True)
        store(sl, x * lax.rsqrt(ms + EPS) * g)
        return carry
    lax.fori_loop(0, rows // chunk, body, 0)


def _inproj_kernel(x_ref, g_ref, w_ref, wab_ref, *rest, tn, offsets):
    out_refs, ab_ref = rest[:-1], rest[-1]
    x = x_ref[...]
    ms = jnp.mean(x * x, axis=-1, keepdims=True)
    h = (x * lax.rsqrt(ms + EPS) * g_ref[...]).astype(BF16)
    ab_ref[...] = _dot(h, wab_ref[...])
    for off, ref in zip(offsets, out_refs):
        for c in range(0, ref.shape[1], tn):
            ref[:, c:c + tn] = _dot(h, w_ref[:, off + c:off + c + tn])


def _inproj(x2d, g, w_main, w_ab, widths, *, tm, tn):
    t, d = x2d.shape
    assert t % tm == 0 and all(w % tn == 0 for w in widths) and sum(widths) <= w_main.shape[1]
    offsets = [sum(widths[:n]) for n in range(len(widths))]
    resident = lambda shape: pl.BlockSpec(shape, lambda i: (0, 0), pipeline_mode=pl.Buffered(1))
    return pl.pallas_call(
        functools.partial(_inproj_kernel, tn=tn, offsets=tuple(offsets)),
        grid=(t // tm,),
        in_specs=[pl.BlockSpec((tm, d), lambda i: (i, 0)), resident((1, d)),
                  resident(w_main.shape), resident((d, LANES))],
        out_specs=[pl.BlockSpec((tm, w), lambda i: (i, 0)) for w in widths]
        + [pl.BlockSpec((tm, LANES), lambda i: (i, 0))],
        out_shape=[jax.ShapeDtypeStruct((t, w), F32) for w in widths] + [jax.ShapeDtypeStruct((t, LANES), F32)],
        compiler_params=_params("parallel"),
        name="inproj",
    )(x2d, g, w_main, w_ab)


def _bias_chain(dist, value_of_bucket):
    val = value_of_bucket(0)
    for k in range(1, NUM_BUCKETS):
        val = jnp.where(dist >= BUCKET_THR[k - 1], value_of_bucket(k), val)
    return val


def _bias_prompt_kernel(rb_ref, tiles_ref, *, blk):
    h = pl.program_id(0)
    qi = lax.broadcasted_iota(jnp.int32, (blk, blk), 0)
    kj = lax.broadcasted_iota(jnp.int32, (blk, blk), 1)
    far = rb_ref[NUM_BUCKETS - 1, h]
    for delta in range(2):
        dist = delta * blk + qi - kj
        val = _bias_chain(dist, lambda k: rb_ref[k, h]) - far
        if delta == 0:
            val = jnp.where(dist >= 0, val, NEG)
        tiles_ref[0, delta] = val


def _bias_prompt(rel_bias, blk):
    nh = rel_bias.shape[1]
    assert blk + 1 >= BUCKET_THR[-1]
    return pl.pallas_call(
        functools.partial(_bias_prompt_kernel, blk=blk),
        grid=(nh,),
        in_specs=[pl.BlockSpec(memory_space=pltpu.SMEM)],
        out_specs=pl.BlockSpec((1, 2, blk, blk), lambda h: (h, 0, 0, 0)),
        out_shape=jax.ShapeDtypeStruct((nh, 2, blk, blk), F32),
        compiler_params=_params("arbitrary"),
        name="bias_prompt",
    )(rel_bias)


def _bias_sample_kernel(rbx_ref, last_ref, own_ref, far_ref, *, blk, t_new):
    tok = lax.broadcasted_iota(jnp.int32, (blk, LANES), 1) & (t_new - 1)
    pos = lax.broadcasted_iota(jnp.int32, (blk, LANES), 0)
    row = lambda k: rbx_ref[k:k + 1, :]
    last_ref[...] = _bias_chain(blk + tok - pos, row)
    tok = lax.broadcasted_iota(jnp.int32, (LANES, LANES), 1) & (t_new - 1)
    pos = lax.broadcasted_iota(jnp.int32, (LANES, LANES), 0)
    dist = tok - pos
    own_ref[...] = jnp.where((dist >= 0) & (pos < t_new), _bias_chain(dist, row), NEG)
    far_ref[...] = jnp.broadcast_to(row(NUM_BUCKETS - 1), (SUBLANES, LANES))


def _bias_sample(rel_bias, blk, t_new):
    nh = rel_bias.shape[1]
    assert nh * t_new <= LANES and blk >= LANES
    rbx = jnp.pad(jnp.repeat(rel_bias, t_new, axis=1), ((0, 0), (0, LANES - nh * t_new)))
    return pl.pallas_call(
        functools.partial(_bias_sample_kernel, blk=blk, t_new=t_new),
        out_shape=[jax.ShapeDtypeStruct((blk, LANES), F32), jax.ShapeDtypeStruct((LANES, LANES), F32),
                   jax.ShapeDtypeStruct((SUBLANES, LANES), F32)],
        name="bias_sample",
    )(rbx)


def _moba_prompt_kernel(q_ref, k_ref, v_ref, bias_ref, o_ref, kb_ref, vb_ref, *, nb, blk, n_sel):
    scale = HEAD_DIM ** -0.5
    s = nb * blk
    k = k_ref[0]
    kb_ref[:, 0:HEAD_DIM] = k.astype(BF16)
    pos_blk = lax.broadcasted_iota(jnp.int32, (s, LANES), 0) >> int(math.log2(blk))
    kb_ref[:, HEAD_DIM:] = jnp.where(pos_blk == lax.broadcasted_iota(jnp.int32, (s, LANES), 1), 1.0, 0.0).astype(BF16)
    vb_ref[...] = v_ref[0].astype(BF16)
    means = jnp.sum(k.reshape(nb, blk, HEAD_DIM), axis=1) * (1.0 / blk)
    means_s = _split(jnp.concatenate([means, jnp.zeros((LANES - nb, HEAD_DIM), F32)], axis=0))
    bidx = lax.broadcasted_iota(jnp.int32, (nb, blk), 0)
    blocks = range(nb)
    rows = [slice(c * blk, (c + 1) * blk) for c in blocks]
    qx = []
    for c in blocks:
        q = q_ref[0, rows[c], :]
        qc = (q * scale).astype(BF16)
        if c > n_sel:
            sc = jnp.where(bidx < c, _dot3(means_s, _split(q), nt=True)[:nb], NEG)
            rank = jnp.zeros((nb, blk), jnp.int32)
            for i in range(c):
                ri = sc[i:i + 1, :]
                tie = (bidx > i).astype(jnp.int32)
                rank = rank + jnp.where(ri > sc, 1, jnp.where(ri == sc, tie, 0))
            drop = jnp.where(bidx < c, jnp.where(rank < n_sel, 0.0, NEG), 0.0)
            drop = jnp.concatenate([drop, jnp.zeros((LANES - nb, blk), F32)], axis=0).T
            qc = jnp.concatenate([qc, drop.astype(BF16)], axis=1)
        qx.append(qc)
    parts = []
    for c in blocks:
        width = qx[c].shape[1]
        pc = []
        for j in range(c):
            sj = _dot_nt(qx[c], kb_ref[j * blk:(j + 1) * blk, 0:width])
            pc.append(sj + bias_ref[0, 1] if j == c - 1 else sj)
        pc.append(_dot_nt(qx[c], kb_ref[rows[c], 0:width]) + bias_ref[0, 0])
        parts.append(pc)
    m = []
    for c in blocks:
        mm = parts[c][0]
        for p in parts[c][1:]:
            mm = jnp.maximum(mm, p)
        m.append(jnp.max(mm, axis=1, keepdims=True))
    ps = [[jnp.exp(p - m[c]) for p in parts[c]] for c in blocks]
    l = []
    for c in blocks:
        tot = ps[c][0]
        for p in ps[c][1:]:
            tot = tot + p
        l.append(jnp.sum(tot, axis=1, keepdims=True))
    o = [_dot(jnp.concatenate([p.astype(BF16) for p in ps[c]], axis=1), vb_ref[0:(c + 1) * blk, :])
         for c in blocks]
    for c in blocks:
        o_ref[0, rows[c], :] = (o[c] / l[c]).astype(o_ref.dtype)


def _moba_prompt(q, k, v, bias_tiles):
    b, s, w = q.shape
    nh = w // HEAD_DIM
    blk = MOBA_BLOCK
    assert s % blk == 0
    nb = s // blk
    n_sel = min(MOBA_TOP_K, nb - 1)
    col = pl.BlockSpec((1, s, HEAD_DIM), lambda bi, h: (bi, 0, h))
    return pl.pallas_call(
        functools.partial(_moba_prompt_kernel, nb=nb, blk=blk, n_sel=n_sel),
        grid=(b, nh),
        in_specs=[col, col, col,
                  pl.BlockSpec((1, 2, blk, blk), lambda bi, h: (h, 0, 0, 0))],
        out_specs=col,
        out_shape=jax.ShapeDtypeStruct((b, s, w), BF16),
        scratch_shapes=[pltpu.VMEM((s, HEAD_DIM + LANES), BF16), pltpu.VMEM((s, HEAD_DIM), BF16)],
        compiler_params=_params("parallel", "parallel"),
        name="moba_prompt",
    )(q, k, v, bias_tiles)


def _moba_sample_kernel(pt_ref, q_ref, kn_ref, vn_ref, blast_ref, bown_ref, bfar_ref, *rest,
                        n_pages, ppb, n_sel):
    del pt_ref
    kps, vps, o_ref = rest[:n_pages], rest[n_pages:2 * n_pages], rest[2 * n_pages]
    scale = HEAD_DIM ** -0.5
    t_new, width = q_ref.shape
    nh = width // HEAD_DIM
    cur = n_pages // ppb
    blk = ppb * PAGE_SIZE
    rid = lax.broadcasted_iota(jnp.int32, (LANES, width), 0)
    cid = lax.broadcasted_iota(jnp.int32, (LANES, width), 1)
    q_rep = jnp.concatenate([q_ref[...]] * (LANES // t_new), axis=0)
    qt = jnp.where((rid >> int(math.log2(t_new))) == (cid >> int(math.log2(HEAD_DIM))), q_rep, 0.0)
    qt_s = (qt * scale).astype(BF16)

    def page_matrix(ref):
        return jnp.concatenate([ref[0, pl.ds(h, PAGE_SIZE, stride=nh), :] for h in range(nh)], axis=1)

    logits, ksums = [], []
    for j in range(n_pages):
        kj = page_matrix(kps[j])
        logits.append(_dot_nt(kj.astype(BF16), qt_s))
        ksums.append(jnp.sum(kj, axis=0, keepdims=True))
    means = []
    for jb in range(cur):
        tot = ksums[jb * ppb]
        for r in range(1, ppb):
            tot = tot + ksums[jb * ppb + r]
        means.append(tot * (1.0 / blk))
    sc = _dot_nt(jnp.concatenate(means, axis=0), qt, HI)
    bidx = lax.broadcasted_iota(jnp.int32, (cur, LANES), 0)
    rank = jnp.zeros((cur, LANES), jnp.int32)
    for i in range(cur):
        ri = sc[i:i + 1, :]
        tie = (bidx > i).astype(jnp.int32)
        rank = rank + jnp.where(ri > sc, 1, jnp.where(ri == sc, tie, 0))

    parts = []
    for jb in range(cur):
        sb = jnp.concatenate(logits[jb * ppb:(jb + 1) * ppb], axis=0)
        sb = sb + (blast_ref[...] if jb == cur - 1 else bfar_ref[0:1, :])
        parts.append(jnp.where(rank[jb:jb + 1, :] < n_sel, sb, NEG))
    pad = jnp.zeros((LANES - t_new, width), F32)
    kn = jnp.concatenate([kn_ref[...], pad], axis=0)
    vn = jnp.concatenate([vn_ref[...], pad], axis=0)
    parts.append(_dot_nt(kn.astype(BF16), qt_s) + bown_ref[...])
    mm = parts[0]
    for p in parts[1:-1]:
        mm = jnp.maximum(mm, p)
    m = jnp.maximum(jnp.max(mm, axis=0, keepdims=True), jnp.max(parts[-1], axis=0, keepdims=True))
    acc = jnp.zeros((LANES, width), F32)
    lsum = jnp.zeros((LANES, 1), F32)
    for jb in range(cur + 1):
        pt = jnp.exp(parts[jb] - m).T
        lsum = lsum + jnp.sum(pt, axis=1, keepdims=True)
        if jb < cur:
            vb = jnp.concatenate([page_matrix(vps[jb * ppb + r]).astype(BF16) for r in range(ppb)], axis=0)
        else:
            vb = vn.astype(BF16)
        acc = acc + _dot(pt.astype(BF16), vb)
    out = acc / lsum
    for h in range(nh):
        o_ref[:, h * HEAD_DIM:(h + 1) * HEAD_DIM] = out[h * t_new:(h + 1) * t_new, h * HEAD_DIM:(h + 1) * HEAD_DIM]


def _moba_sample(q, k_new, v_new, cache_k, cache_v, page_table, blast, bown, bfar, *, t_new):
    rows, width = q.shape
    db = rows // t_new
    n_pool = cache_k.shape[0]
    n_pages = page_table.shape[1]
    ppb = MOBA_BLOCK // PAGE_SIZE
    assert n_pages % ppb == 0, "new tokens must start a fresh MoBA block"
    assert t_new == SUBLANES
    cur = n_pages // ppb
    n_sel = min(MOBA_TOP_K, cur)
    assert n_sel > 0
    nh = width // HEAD_DIM
    ck = cache_k.reshape(n_pool, PAGE_SIZE * nh, HEAD_DIM)
    cv = cache_v.reshape(n_pool, PAGE_SIZE * nh, HEAD_DIM)
    new = pl.BlockSpec((t_new, width), lambda s, pt: (s, 0))
    const = lambda shape: pl.BlockSpec(shape, lambda s, pt: (0, 0))
    page = lambda j: pl.BlockSpec((1, PAGE_SIZE * nh, HEAD_DIM), lambda s, pt, j=j: (pt[s, j], 0, 0))
    grid_spec = pltpu.PrefetchScalarGridSpec(
        num_scalar_prefetch=1,
        grid=(db,),
        in_specs=[new, new, new, const(blast.shape), const(bown.shape), const(bfar.shape)]
        + [page(j) for j in range(n_pages)] * 2,
        out_specs=new,
    )
    return pl.pallas_call(
        functools.partial(_moba_sample_kernel, n_pages=n_pages, ppb=ppb, n_sel=n_sel),
        grid_spec=grid_spec,
        out_shape=jax.ShapeDtypeStruct((rows, width), F32),
        compiler_params=_params("parallel"),
        name="moba_sample",
    )(page_table, q, k_new, v_new, blast, bown, bfar, *([ck] * n_pages), *([cv] * n_pages))


def _split(x):
    hi = x.astype(BF16)
    return hi, (x - hi.astype(F32)).astype(BF16)


def _dot3(a, b, nt=False):
    lhs = jnp.concatenate([a[0], a[0], a[1]], axis=1)
    if nt:
        return _dot_nt(lhs, jnp.concatenate([b[0], b[1], b[0]], axis=1))
    return _dot(lhs, jnp.concatenate([b[0], b[1], b[0]], axis=0))


def _gdn_kernel(alog_ref, dtb_ref, q_ref, k_ref, v_ref, z_ref, ab_ref, cwq_ref, cwk_ref, cwv_ref,
                pq_ref, pk_ref, pv_ref, nw_ref, s0_ref, o_ref, so_ref, *scratch, tt, chunk, sample, n_heads, hp):
    assert hp == n_heads
    n_chunks = tt // chunk
    ins = [q_ref, k_ref, v_ref]
    cw_refs = [cwq_ref, cwk_ref, cwv_ref]
    prevs = [pq_ref, pk_ref, pv_ref]
    rid8 = lax.broadcasted_iota(jnp.int32, (SUBLANES, HEAD_DIM), 0)
    if sample:
        (shift_scr,) = scratch
    else:
        s_scr, tail_scr = scratch
        i = pl.program_id(2)

        @pl.when(i == 0)
        def _():
            s_scr[...] = s0_ref[0]
            for n in range(3):
                tail_scr[n] = prevs[n][0]

    ri = lax.broadcasted_iota(jnp.int32, (tt, tt), 0)
    ci = lax.broadcasted_iota(jnp.int32, (tt, tt), 1)
    shift_c = int(math.log2(chunk))
    same_block = lambda bits: (ri >> bits) == (ci >> bits)
    same = same_block(shift_c)
    incl = same & (ri >= ci)
    strict = same & (ri > ci)
    incl_t = same & (ri <= ci)
    eye = jnp.where(ri == ci, 1.0, 0.0)
    base_bits = int(math.log2(SUBLANES))
    base = same_block(base_bits)
    level_masks = [same_block(bits + 1) & (ri >> bits != ci >> bits) for bits in range(base_bits, shift_c)]
    sum_mat = jnp.concatenate([jnp.where(incl, 1.0, 0.0), jnp.where(same, 1.0, 0.0)], axis=0).astype(BF16)
    sum_mat3 = jnp.concatenate([sum_mat] * 3, axis=1)
    ab = ab_ref[0]
    heads = range(hp)
    cols = [slice(hh * HEAD_DIM, (hh + 1) * HEAD_DIM) for hh in heads]
    chunks = [slice(c * chunk, (c + 1) * chunk) for c in range(n_chunks)]


    xq, xk, xv = [], [], []
    for hh in heads:
        for n, dst in enumerate((xq, xk, xv)):
            x = ins[n][0, :, cols[hh]]
            w = cw_refs[n][:, cols[hh]]
            acc = None
            for shift in range(GDN_CONV - 1, 0, -1):
                rolled = pltpu.roll(x, shift, axis=0)
                if sample:
                    slot = (hh * 3 + n) * (GDN_CONV - 1) + shift - 1
                    shift_scr[slot] = rolled
                    for r in range(shift):
                        shift_scr[slot, pl.ds(r, n_chunks, stride=chunk), :] = (
                            prevs[n][GDN_CONV - 1 - shift + r, 0, :, cols[hh]])
                    xm = shift_scr[slot]
                else:
                    first = jnp.where(rid8 < shift, pltpu.roll(tail_scr[n, :, cols[hh]], shift, axis=0),
                                      rolled[:SUBLANES])
                    xm = jnp.concatenate([first, rolled[SUBLANES:]], axis=0)
                tap = GDN_CONV - 1 - shift
                term = xm * w[tap:tap + 1, :]
                acc = term if acc is None else acc + term
            acc = acc + x * w[GDN_CONV - 1:GDN_CONV, :]
            dst.append(_silu(acc))
            if not sample:
                tail_scr[n, :, cols[hh]] = x[tt - SUBLANES:]

    g_all = -jnp.exp(alog_ref[...]) * _softplus(ab + dtb_ref[...])
    beta_all = jax.nn.sigmoid(ab)
    g0 = g_all.astype(BF16)
    r1 = g_all - g0.astype(F32)
    g1 = r1.astype(BF16)
    g2 = (r1 - g1.astype(F32)).astype(BF16)
    sums = _dot(sum_mat3, jnp.concatenate([g0, g1, g2], axis=0))
    g_col = [g_all[:, hh:hh + 1] for hh in heads]
    beta = [beta_all[:, n_heads + hh:n_heads + hh + 1] for hh in heads]
    cg = [sums[:tt, hh:hh + 1] for hh in heads]
    gl = [sums[tt:, hh:hh + 1] for hh in heads]
    decay = []
    for hh in heads:
        cg_row = jnp.sum(jnp.where(incl_t, g_col[hh], 0.0), axis=0, keepdims=True)
        decay.append(jnp.exp(jnp.where(incl, cg[hh] - cg_row, NEG)))

    qn = [xq[hh] * lax.rsqrt(jnp.sum(xq[hh] * xq[hh], axis=-1, keepdims=True) + 1e-6) * (HEAD_DIM ** -0.5)
          for hh in heads]
    kn = [xk[hh] * lax.rsqrt(jnp.sum(xk[hh] * xk[hh], axis=-1, keepdims=True) + 1e-6) for hh in heads]
    kn_s = [_split(kn[hh]) for hh in heads]
    kk = [_dot3(kn_s[hh], kn_s[hh], nt=True) for hh in heads]
    lmat = [jnp.where(strict, beta[hh] * decay[hh] * kk[hh], 0.0) for hh in heads]

    xp = [-jnp.where(base, lmat[hh], 0.0) for hh in heads]
    tinv = [eye + xp[hh] for hh in heads]
    xp_s = [_split(xp[hh]) for hh in heads]
    for _ in range(base_bits - 1):
        xp_s = [_split(_dot3(xp_s[hh], xp_s[hh])) for hh in heads]
        tinv = [tinv[hh] + _dot3(_split(tinv[hh]), xp_s[hh]) for hh in heads]
    l_cat = [jnp.concatenate(_split(lmat[hh]), axis=1) for hh in heads] if level_masks else None
    for mask in level_masks:
        t_hi = [tinv[hh].astype(BF16) for hh in heads]
        lt = [_split(_dot(l_cat[hh], jnp.concatenate([t_hi[hh], t_hi[hh]], axis=0))) for hh in heads]
        tinv = [tinv[hh] - jnp.where(mask, _dot(jnp.concatenate([t_hi[hh], t_hi[hh]], axis=1),
                                                jnp.concatenate(lt[hh], axis=0)), 0.0) for hh in heads]

    eg = [jnp.exp(cg[hh]) for hh in heads]
    wu = [_dot3(_split(tinv[hh]),
                _split(jnp.concatenate([kn[hh] * (beta[hh] * eg[hh]), xv[hh] * beta[hh]], axis=1)))
          for hh in heads]
    u0 = [wu[hh][:, HEAD_DIM:] for hh in heads]
    amat = [jnp.where(incl, _dot_nt(qn[hh].astype(BF16), kn_s[hh][0]) * decay[hh], 0.0).astype(BF16)
            for hh in heads]
    wq = [[jnp.concatenate([wu[hh][cs, :HEAD_DIM], (qn[hh] * eg[hh])[cs]], axis=0).astype(BF16) for cs in chunks]
          for hh in heads]
    kd = [kn[hh] * jnp.exp(gl[hh] - cg[hh]) for hh in heads]
    egl = [jnp.exp(gl[hh]) for hh in heads]

    if sample:
        o = []
        for hh in heads:
            rs = [_dot(wq[hh][c], s0_ref[c, hh].astype(BF16)) for c in range(n_chunks)]
            u = u0[hh] - jnp.concatenate([r[:chunk] for r in rs], axis=0)
            o.append(jnp.concatenate([r[chunk:] for r in rs], axis=0) + _dot(amat[hh], u.astype(BF16)))
            for c, cs in enumerate(chunks):
                so_ref[c, hh] = (egl[hh][c * chunk:c * chunk + 1, :] * s0_ref[c, hh]
                                 + _dot_tn(kd[hh][cs], u[cs], HI))
    else:
        kd_t = [kd[hh].T.astype(BF16) for hh in heads]
        s = [s_scr[hh] for hh in heads]
        outs = [[] for _ in heads]
        for c, cs in enumerate(chunks):
            for hh in heads:
                r = _dot(wq[hh][c], s[hh].astype(BF16))
                u = (u0[hh][cs] - r[:chunk]).astype(BF16)
                pieces = [u]
                if c > 0:
                    pieces.insert(0, jnp.zeros((c * chunk, HEAD_DIM), BF16))
                if c < n_chunks - 1:
                    pieces.append(jnp.zeros((tt - (c + 1) * chunk, HEAD_DIM), BF16))
                u_full = jnp.concatenate(pieces, axis=0)
                outs[hh].append(r[chunk:] + _dot(amat[hh][cs, :], u_full))
                s[hh] = egl[hh][c * chunk:c * chunk + 1, :] * s[hh] + _dot(kd_t[hh], u_full)
        for hh in heads:
            s_scr[hh] = s[hh]
            so_ref[0, hh] = s[hh]
        o = [jnp.concatenate(outs[hh], axis=0) for hh in heads]

    for hh in heads:
        z = z_ref[0, :, cols[hh]]
        on = (o[hh] * lax.rsqrt(jnp.mean(o[hh] * o[hh], axis=-1, keepdims=True) + EPS) * nw_ref[...] * _silu(z))
        o_ref[0, :, cols[hh]] = on.astype(o_ref.dtype)


def _gdn(gqkv, z, ab, conv_w, prev, a_log, dt_bias, norm_w, s0, *, sample, tt):
    g, r, w3 = gqkv.shape
    w = w3 // 3
    nh = w // HEAD_DIM
    hp = nh
    assert r % tt == 0
    ng = 1
    wb = hp * HEAD_DIM
    smem = pl.BlockSpec((1, LANES), lambda *_: (0, 0))
    if sample:
        assert r == tt
        chunk = SUBLANES
        seqs = tt // chunk
        grid = (g, ng)
        col = lambda off: pl.BlockSpec((1, tt, wb), lambda gi, h, off=off: (gi, 0, off + h))
        prev_spec = lambda off: pl.BlockSpec((GDN_CONV - 1, 1, seqs, wb), lambda gi, h, off=off: (0, gi, 0, off + h))
        state = pl.BlockSpec((seqs, hp, HEAD_DIM, HEAD_DIM), lambda gi, h: (gi, h, 0, 0))
        cw = lambda off: pl.BlockSpec((GDN_CONV, wb), lambda gi, h, off=off: (0, off + h))
        row = pl.BlockSpec((1, HEAD_DIM), lambda gi, h: (0, 0))
        abs_ = pl.BlockSpec((1, tt, LANES), lambda gi, h: (gi, 0, 0))
        scratch = [pltpu.VMEM((hp * 3 * (GDN_CONV - 1), tt, HEAD_DIM), F32)]
        sem = ("parallel", "parallel")
    else:
        chunk = GDN_CHUNK
        grid = (g, ng, r // tt)
        col = lambda off: pl.BlockSpec((1, tt, wb), lambda gi, h, i, off=off: (gi, i, off + h))
        prev_spec = lambda off: pl.BlockSpec((1, SUBLANES, wb), lambda gi, h, i, off=off: (gi, 0, off + h))
        state = pl.BlockSpec((1, hp, HEAD_DIM, HEAD_DIM), lambda gi, h, i: (gi, h, 0, 0))
        cw = lambda off: pl.BlockSpec((GDN_CONV, wb), lambda gi, h, i, off=off: (0, off + h))
        row = pl.BlockSpec((1, HEAD_DIM), lambda gi, h, i: (0, 0))
        abs_ = pl.BlockSpec((1, tt, LANES), lambda gi, h, i: (gi, i, 0))
        scratch = [pltpu.VMEM((hp, HEAD_DIM, HEAD_DIM), F32), pltpu.VMEM((3, SUBLANES, wb), F32)]
        sem = ("parallel", "parallel", "arbitrary")
    return pl.pallas_call(
        functools.partial(_gdn_kernel, tt=tt, chunk=chunk, sample=sample, n_heads=nh, hp=hp),
        grid=grid,
        in_specs=[smem, smem, col(0), col(ng), col(2 * ng), col(0), abs_, cw(0), cw(ng), cw(2 * ng),
                  prev_spec(0), prev_spec(ng), prev_spec(2 * ng), row, state],
        out_specs=[col(0), state],
        out_shape=[jax.ShapeDtypeStruct((g, r, w), BF16), jax.ShapeDtypeStruct(s0.shape, F32)],
        scratch_shapes=scratch,
        compiler_params=_params(*sem),
        name="gdn_sample" if sample else "gdn_prompt",
    )(a_log, dt_bias, gqkv, gqkv, gqkv, z, ab, conv_w, conv_w, conv_w, prev, prev, prev, norm_w, s0)


def _outproj_kernel(x_ref, att_ref, gdn_ref, wa_ref, wb_ref, o_ref):
    o_ref[...] = (x_ref[...] + _dot(att_ref[...].astype(BF16), wa_ref[...])
                  + _dot(gdn_ref[...].astype(BF16), wb_ref[...]))


def _outproj(x2d, att, gdn, w_out, *, tm):
    t, d = x2d.shape
    wa = att.shape[1]
    assert gdn.shape[1] == wa and w_out.shape[0] == 2 * wa
    return pl.pallas_call(
        _outproj_kernel,
        grid=(t // tm,),
        in_specs=[pl.BlockSpec((tm, d), lambda i: (i, 0)), pl.BlockSpec((tm, wa), lambda i: (i, 0)),
                  pl.BlockSpec((tm, wa), lambda i: (i, 0)), pl.BlockSpec((wa, d), lambda i: (0, 0)),
                  pl.BlockSpec((wa, d), lambda i: (1, 0))],
        out_specs=pl.BlockSpec((tm, d), lambda i: (i, 0)),
        out_shape=jax.ShapeDtypeStruct((t, d), F32),
        compiler_params=_params("parallel"),
        name="outproj",
    )(x2d, att, gdn, w_out, w_out)


def _ffn_up_kernel(x_ref, g_ref, wg_ref, wv_ref, cwg_ref, cwv_ref, cbg_ref, cbv_ref, *rest,
                   tm, tf, sample, tiles_per_seq, row_parts):
    if sample:
        (sg0_ref, sg1_ref, sv0_ref, sv1_ref, act_ref, og0_ref, og1_ref, ov0_ref, ov1_ref,
         h_ref, buf_g, buf_v) = rest
    else:
        act_ref, ug_ref, uv_ref, h_ref, carry_g, carry_v, buf_g, buf_v = rest
    i = pl.program_id(0)
    f = pl.program_id(1)

    @pl.when(f == 0)
    def _():
        def store(sl, y):
            h_ref[sl, :] = y.astype(BF16)
        _rmsnorm_rows(lambda sl: x_ref[sl, :], g_ref[...], store, tm)

    def sample_conv(w_ref, cw_ref, cb_ref, state_refs, buf_ref, out_refs):
        cw = cw_ref[...]
        n_seq = tm // SUBLANES
        seq_row = lambda r: pl.ds(r, n_seq, stride=SUBLANES)
        u = _dot(h_ref[...], w_ref[...])
        n_col = tf // LANES
        for c in range(n_col):
            lanes = slice(c * LANES, (c + 1) * LANES)
            buf_ref[0, c] = u[:, lanes]
            for r, o_ref in enumerate(out_refs):
                o_ref[:, lanes] = buf_ref[0, c, seq_row(SUBLANES - (FFN_CONV - 1) + r), :]
            for shift in range(1, FFN_CONV):
                buf_ref[shift, c] = pltpu.roll(u[:, lanes], shift, axis=0)
                for r in range(shift):
                    buf_ref[shift, c, seq_row(r), :] = state_refs[FFN_CONV - 1 - shift + r][:, lanes]
        shifted = {shift: jnp.concatenate([buf_ref[shift, c] for c in range(n_col)], axis=1)
                   for shift in range(1, FFN_CONV)}
        return shifted[2] * cw[0:1, :] + shifted[1] * cw[1:2, :] + u * cw[2:3, :] + cb_ref[...]

    if sample:
        yg = sample_conv(wg_ref, cwg_ref, cbg_ref, (sg0_ref, sg1_ref), buf_g, (og0_ref, og1_ref))
        yv = sample_conv(wv_ref, cwv_ref, cbv_ref, (sv0_ref, sv1_ref), buf_v, (ov0_ref, ov1_ref))
        act_ref[...] = (_silu(yg) * yv).astype(BF16)
    else:
        gate = (wg_ref, cwg_ref, cbg_ref, buf_g, carry_g, ug_ref)
        val = (wv_ref, cwv_ref, cbv_ref, buf_v, carry_v, uv_ref)
        for w_ref, cw_ref, cb_ref, buf_ref, carry_ref, out_ref in (gate, val):
            buf_ref[0:SUBLANES, :] = jnp.where(i % tiles_per_seq == 0, 0.0, carry_ref[f])
        part = tm // row_parts
        for r0 in range(0, tm, part):
            ys = []
            for w_ref, cw_ref, cb_ref, buf_ref, carry_ref, out_ref in (gate, val):
                buf_ref[SUBLANES + r0:SUBLANES + r0 + part, :] = _dot(h_ref[r0:r0 + part, :], w_ref[...])
                cw = cw_ref[...]
                taps = [buf_ref[SUBLANES - shift + r0:SUBLANES - shift + r0 + part, :]
                        for shift in range(FFN_CONV - 1, -1, -1)]
                ys.append(taps[0] * cw[0:1, :] + taps[1] * cw[1:2, :] + taps[2] * cw[2:3, :] + cb_ref[...])
            act_ref[r0:r0 + part, :] = (_silu(ys[0]) * ys[1]).astype(BF16)
        for w_ref, cw_ref, cb_ref, buf_ref, carry_ref, out_ref in (gate, val):
            last = buf_ref[tm:, :]
            carry_ref[f] = last
            out_ref[0] = last


def _ffn_up(x2d, g, w_up, conv_w, conv_b, prev, *, tm, seq_rows, sample):
    t, d = x2d.shape
    dff = w_up.shape[1] // 2
    tf = 512
    nf = dff // tf
    assert dff % tf == 0 and t % tm == 0
    gate = lambda shape: pl.BlockSpec(shape, lambda i, f: (0, f))
    val = lambda shape: pl.BlockSpec(shape, lambda i, f: (0, nf + f))
    in_specs = [pl.BlockSpec((tm, d), lambda i, f: (i, 0)), pl.BlockSpec((1, d), lambda i, f: (0, 0)),
                gate((d, tf)), val((d, tf)), gate((FFN_CONV, tf)), val((FFN_CONV, tf)),
                gate((1, tf)), val((1, tf))]
    args = [x2d, g, w_up, w_up, conv_w, conv_w, conv_b, conv_b]
    scratch = [pltpu.VMEM((tm, d), BF16)]
    if sample:
        assert seq_rows == SUBLANES and FFN_CONV == 3
        tiles_per_seq = 0
        n_seq = tm // SUBLANES
        in_specs += [pl.BlockSpec((n_seq, tf), lambda i, f: (i, f))] * 2
        in_specs += [pl.BlockSpec((n_seq, tf), lambda i, f: (i, nf + f))] * 2
        args += [prev[0], prev[1], prev[0], prev[1]]
        u_specs = [pl.BlockSpec((n_seq, tf), lambda i, f: (i, f))] * 4
        u_shapes = [jax.ShapeDtypeStruct((t // SUBLANES, dff), F32)] * 4
        scratch += [pltpu.VMEM((FFN_CONV, tf // LANES, tm, LANES), F32)] * 2
    else:
        assert seq_rows % tm == 0
        tiles_per_seq = seq_rows // tm
        u_specs = [pl.BlockSpec((1, SUBLANES, tf), lambda i, f: (i, 0, f))] * 2
        u_shapes = [jax.ShapeDtypeStruct((t // tm, SUBLANES, dff), F32)] * 2
        scratch += [pltpu.VMEM((nf, SUBLANES, tf), F32)] * 2 + [pltpu.VMEM((SUBLANES + tm, tf), F32)] * 2
    return pl.pallas_call(
        functools.partial(_ffn_up_kernel, tm=tm, tf=tf, sample=sample, tiles_per_seq=tiles_per_seq, row_parts=1),
        grid=(t // tm, nf),
        in_specs=in_specs,
        out_specs=[pl.BlockSpec((tm, tf), lambda i, f: (i, f))] + u_specs,
        out_shape=[jax.ShapeDtypeStruct((t, dff), BF16)] + u_shapes,
        scratch_shapes=scratch,
        compiler_params=_params("arbitrary", "arbitrary"),
        name="ffn_up_sample" if sample else "ffn_up_prompt",
    )(*args)


def _ffn_down_kernel(x_ref, act_ref, wd_ref, gf_ref, y_ref, *, tm, tn, final_norm):
    act = act_ref[...]
    for c in range(0, y_ref.shape[1], tn):
        y_ref[:, c:c + tn] = x_ref[:, c:c + tn] + _dot(act, wd_ref[:, c:c + tn])
    if final_norm:
        def store(sl, y):
            y_ref[sl, :] = y
        _rmsnorm_rows(lambda sl: y_ref[sl, :], gf_ref[...], store, tm)


def _ffn_down(x2d, act, w_down, gf, *, tm, final_norm):
    t, d = x2d.shape
    dff = w_down.shape[0]
    tn = 512
    assert d % tn == 0 and t % tm == 0
    resident = lambda shape: pl.BlockSpec(shape, lambda i: (0, 0), pipeline_mode=pl.Buffered(1))
    return pl.pallas_call(
        functools.partial(_ffn_down_kernel, tm=tm, tn=tn, final_norm=final_norm),
        grid=(t // tm,),
        in_specs=[pl.BlockSpec((tm, d), lambda i: (i, 0)), pl.BlockSpec((tm, dff), lambda i: (i, 0)),
                  resident((dff, d)), resident((1, d))],
        out_specs=pl.BlockSpec((tm, d), lambda i: (i, 0)),
        out_shape=jax.ShapeDtypeStruct((t, d), F32),
        compiler_params=_params("parallel"),
        name="ffn_down",
    )(x2d, act, w_down, gf)


def _layer_slice(x, l):
    return x.reshape(x.shape[1:]) if x.shape[0] == 1 else x[l]


def kernel(x_prompt, x_sample, cache_k, cache_v, page_table, state_gdn, state_gdn_conv, state_ffn_conv,
           rms_mix_w, w_in, gdn_conv_w, gdn_a_log, gdn_dt_bias, gdn_norm_w, w_out, rms_ffn_w, w_up,
           ffn_conv_w, ffn_conv_b, w_down, rel_bias, final_norm_w):
    b, s, d = x_prompt.shape
    db, t_new, _ = x_sample.shape
    depth = w_in.shape[0]
    n_att = cache_k.shape[3]
    w_att = n_att * HEAD_DIM
    n_gdn = gdn_a_log.shape[1]
    w_gdn = n_gdn * HEAD_DIM
    dff = w_down.shape[1]
    assert w_att == w_gdn == 1024 and w_in.shape[2] == 3 * w_att + 4 * w_gdn + 2 * n_gdn
    n_main = 3 * w_att + 4 * w_gdn
    n_pool = cache_k.shape[1]
    tt_s = 128
    grp = tt_s // t_new
    tm = 512

    bias_tiles = _bias_prompt(rel_bias, MOBA_BLOCK)
    blast, bown, bfar = _bias_sample(rel_bias, MOBA_BLOCK, t_new)

    xp = x_prompt.reshape(b * s, d)
    xs = x_sample.reshape(db * t_new, d)
    outs_p = [[] for _ in range(5)]
    outs_s = [[] for _ in range(5)]
    for l in range(depth):
        w_in_b = w_in[l].astype(BF16)
        w_main = w_in_b
        w_ab = jnp.pad(w_in_b[:, n_main:], ((0, 0), (0, LANES - 2 * n_gdn)))
        w_out_b = w_out[l].astype(BF16)
        w_up_b = w_up[l].astype(BF16)
        w_down_b = w_down[l].astype(BF16)
        g_mix = rms_mix_w[l].reshape(1, d)
        g_ffn = rms_ffn_w[l].reshape(1, d)
        g_fin = final_norm_w.reshape(1, d)
        norm_w = gdn_norm_w[l].reshape(1, HEAD_DIM)
        a_log_row = jnp.pad(gdn_a_log[l].reshape(1, n_gdn), ((0, 0), (0, LANES - n_gdn)))
        dt_bias_row = jnp.pad(gdn_dt_bias[l].reshape(1, n_gdn), ((0, 0), (0, LANES - n_gdn)))
        conv_b = ffn_conv_b[l].reshape(1, 2 * dff)
        last = l == depth - 1

        widths = (w_att, w_att, w_att, 3 * w_gdn, w_gdn)
        q, k, v, gq, z, ab = _inproj(xp, g_mix, w_main, w_ab, widths, tm=tm // 2, tn=512)
        att = _moba_prompt(q.reshape(b, s, w_att), k.reshape(b, s, w_att), v.reshape(b, s, w_att),
                           bias_tiles)
        gdn, s_new = _gdn(gq.reshape(b, s, 3 * w_gdn), z.reshape(b, s, w_gdn), ab.reshape(b, s, LANES),
                          gdn_conv_w[l], jnp.zeros((b, SUBLANES, 3 * w_gdn), F32), a_log_row, dt_bias_row,
                          norm_w, jnp.zeros((b, n_gdn, HEAD_DIM, HEAD_DIM), F32), sample=False, tt=256)
        x1 = _outproj(xp, att.reshape(b * s, w_att), gdn.reshape(b * s, w_gdn), w_out_b, tm=tm)
        act, ug, uv = _ffn_up(x1, g_ffn, w_up_b, ffn_conv_w[l], conv_b, None, tm=2 * tm, seq_rows=s, sample=False)
        y = _ffn_down(x1, act, w_down_b, g_fin, tm=tm, final_norm=last)
        outs_p[0].append(k.reshape(b, s, n_att, HEAD_DIM))
        outs_p[1].append(v.reshape(b, s, n_att, HEAD_DIM))
        outs_p[2].append(s_new)
        outs_p[3].append(gq.reshape(b, s, 3 * w_gdn)[:, s - (GDN_CONV - 1):])
        tps = s // (2 * tm)
        outs_p[4].append(jnp.concatenate([ug, uv], axis=-1)[tps - 1::tps, SUBLANES - (FFN_CONV - 1):])
        xp = y

        ts = db * t_new
        q, k, v, gq, z, ab = _inproj(xs, g_mix, w_main, w_ab, widths, tm=tm // 2, tn=512)
        att = _moba_sample(q, k, v, cache_k.reshape((depth * n_pool,) + cache_k.shape[2:]),
                           cache_v.reshape((depth * n_pool,) + cache_v.shape[2:]), page_table + l * n_pool,
                           blast, bown, bfar, t_new=t_new)
        prev_g = jnp.moveaxis(state_gdn_conv[l], 1, 0).reshape(GDN_CONV - 1, db // grp, grp, 3 * w_gdn)
        gdn, s_new = _gdn(gq.reshape(db // grp, tt_s, 3 * w_gdn), z.reshape(db // grp, tt_s, w_gdn),
                          ab.reshape(db // grp, tt_s, LANES), gdn_conv_w[l], prev_g, a_log_row, dt_bias_row,
                          norm_w, _layer_slice(state_gdn, l), sample=True, tt=tt_s)
        x1 = _outproj(xs, att, gdn.reshape(ts, w_gdn), w_out_b, tm=tm)
        prev_f = tuple(state_ffn_conv[l, :, r] for r in range(FFN_CONV - 1))
        act, g0, g1, v0, v1 = _ffn_up(x1, g_ffn, w_up_b, ffn_conv_w[l], conv_b, prev_f, tm=tm, seq_rows=t_new,
                                      sample=True)
        y = _ffn_down(x1, act, w_down_b, g_fin, tm=tm, final_norm=last)
        outs_s[0].append(k.reshape(db, t_new, n_att, HEAD_DIM))
        outs_s[1].append(v.reshape(db, t_new, n_att, HEAD_DIM))
        outs_s[2].append(s_new)
        outs_s[3].append(gq.reshape(db, t_new, 3 * w_gdn)[:, t_new - (GDN_CONV - 1):])
        outs_s[4].append(jnp.stack([jnp.concatenate([g0, v0], axis=-1), jnp.concatenate([g1, v1], axis=-1)], axis=1))
        xs = y

    return (xp.reshape(b, s, d), xs.reshape(db, t_new, d),
            *[jnp.stack(o) for o in outs_p], *[jnp.stack(o) for o in outs_s])
```

```python
import functools
import math

import jax
import jax.numpy as jnp
import numpy as np
from jax import lax
from jax.experimental import pallas as pl
from jax.experimental.pallas import tpu as pltpu

F32 = jnp.float32
BF16 = jnp.bfloat16
HI = lax.Precision.HIGHEST

HEAD_DIM = 128
MOBA_BLOCK = 256
MOBA_TOP_K = 3
PAGE_SIZE = 128
GDN_CONV = 4
GDN_CHUNK = 64
FFN_CONV = 3
NUM_BUCKETS = 32
MAX_DISTANCE = 128
EPS = 1e-6
NEG = -1e30

LANES = 128
SUBLANES = 8
VMEM_LIMIT = 56 * 1024 * 1024


def _bucket_thresholds():
    n = np.arange(0, 4 * MAX_DISTANCE)
    max_exact = NUM_BUCKETS // 2
    nf = np.maximum(n, 1).astype(np.float32)
    large = max_exact + (np.log(nf / np.float32(max_exact)) / np.float32(math.log(MAX_DISTANCE / max_exact))
                         * np.float32(NUM_BUCKETS - max_exact)).astype(np.int32)
    large = np.minimum(large, NUM_BUCKETS - 1)
    b = np.where(n < max_exact, n, large)
    return [int(np.argmax(b >= k)) for k in range(1, NUM_BUCKETS)]


BUCKET_THR = _bucket_thresholds()


def _dot(a, b, precision=None):
    return jnp.dot(a, b, preferred_element_type=F32, precision=precision)


def _dot_nt(a, b, precision=None):
    return lax.dot_general(a, b, (((1,), (1,)), ((), ())), preferred_element_type=F32, precision=precision)


def _dot_tn(a, b, precision=None):
    return lax.dot_general(a, b, (((0,), (0,)), ((), ())), preferred_element_type=F32, precision=precision)


def _silu(x):
    return x * jax.nn.sigmoid(x)


def _softplus(x):
    return jnp.maximum(x, 0.0) + jnp.log1p(jnp.exp(-jnp.abs(x)))


def _params(*semantics):
    return pltpu.CompilerParams(dimension_semantics=semantics, vmem_limit_bytes=VMEM_LIMIT)


def _rmsnorm_rows(load, g, store, rows, chunk=256):
    def body(r, carry):
        sl = pl.ds(pl.multiple_of(r * chunk, chunk), chunk)
        x = load(sl)
        ms = jnp.mean(x * x, axis=-1, keepdims=True)
        store(sl, x * lax.rsqrt(ms + EPS) * g)
        return carry
    lax.fori_loop(0, rows // chunk, body, 0)


def _inproj_kernel(x_ref, g_ref, w_ref, wab_ref, *rest, tn, offsets):
    out_refs, ab_ref = rest[:-1], rest[-1]
    x = x_ref[...]
    ms = jnp.mean(x * x, axis=-1, keepdims=True)
    h = (x * lax.rsqrt(ms + EPS) * g_ref[...]).astype(BF16)
    ab_ref[...] = _dot(h, wab_ref[...])
    for off, ref in zip(offsets, out_refs):
        for c in range(0, ref.shape[1], tn):
            ref[:, c:c + tn] = _dot(h, w_ref[:, off + c:off + c + tn])


def _inproj(x2d, g, w_main, w_ab, widths, *, tm, tn):
    t, d = x2d.shape
    assert t % tm == 0 and all(w % tn == 0 for w in widths) and sum(widths) <= w_main.shape[1]
    offsets = [sum(widths[:n]) for n in range(len(widths))]
    resident = lambda shape: pl.BlockSpec(shape, lambda i: (0, 0), pipeline_mode=pl.Buffered(1))
    return pl.pallas_call(
        functools.partial(_inproj_kernel, tn=tn, offsets=tuple(offsets)),
        grid=(t // tm,),
        in_specs=[pl.BlockSpec((tm, d), lambda i: (i, 0)), resident((1, d)),
                  resident(w_main.shape), resident((d, LANES))],
        out_specs=[pl.BlockSpec((tm, w), lambda i: (i, 0)) for w in widths]
        + [pl.BlockSpec((tm, LANES), lambda i: (i, 0))],
        out_shape=[jax.ShapeDtypeStruct((t, w), F32) for w in widths] + [jax.ShapeDtypeStruct((t, LANES), F32)],
        compiler_params=_params("parallel"),
        name="inproj",
    )(x2d, g, w_main, w_ab)


def _bias_chain(dist, value_of_bucket):
    val = value_of_bucket(0)
    for k in range(1, NUM_BUCKETS):
        val = jnp.where(dist >= BUCKET_THR[k - 1], value_of_bucket(k), val)
    return val


def _bias_prompt_kernel(rb_ref, tiles_ref, *, blk):
    h = pl.program_id(0)
    qi = lax.broadcasted_iota(jnp.int32, (blk, blk), 0)
    kj = lax.broadcasted_iota(jnp.int32, (blk, blk), 1)
    far = rb_ref[NUM_BUCKETS - 1, h]
    for delta in range(2):
        dist = delta * blk + qi - kj
        val = _bias_chain(dist, lambda k: rb_ref[k, h]) - far
        if delta == 0:
            val = jnp.where(dist >= 0, val, NEG)
        tiles_ref[0, delta] = val


def _bias_prompt(rel_bias, blk):
    nh = rel_bias.shape[1]
    assert blk + 1 >= BUCKET_THR[-1]
    return pl.pallas_call(
        functools.partial(_bias_prompt_kernel, blk=blk),
        grid=(nh,),
        in_specs=[pl.BlockSpec(memory_space=pltpu.SMEM)],
        out_specs=pl.BlockSpec((1, 2, blk, blk), lambda h: (h, 0, 0, 0)),
        out_shape=jax.ShapeDtypeStruct((nh, 2, blk, blk), F32),
        compiler_params=_params("arbitrary"),
        name="bias_prompt",
    )(rel_bias)


def _bias_sample_kernel(rbx_ref, last_ref, own_ref, far_ref, *, blk, t_new):
    tok = lax.broadcasted_iota(jnp.int32, (blk, LANES), 1) & (t_new - 1)
    pos = lax.broadcasted_iota(jnp.int32, (blk, LANES), 0)
    row = lambda k: rbx_ref[k:k + 1, :]
    last_ref[...] = _bias_chain(blk + tok - pos, row)
    tok = lax.broadcasted_iota(jnp.int32, (LANES, LANES), 1) & (t_new - 1)
    pos = lax.broadcasted_iota(jnp.int32, (LANES, LANES), 0)
    dist = tok - pos
    own_ref[...] = jnp.where((dist >= 0) & (pos < t_new), _bias_chain(dist, row), NEG)
    far_ref[...] = jnp.broadcast_to(row(NUM_BUCKETS - 1), (SUBLANES, LANES))


def _bias_sample(rel_bias, blk, t_new):
    nh = rel_bias.shape[1]
    assert nh * t_new <= LANES and blk >= LANES
    rbx = jnp.pad(jnp.repeat(rel_bias, t_new, axis=1), ((0, 0), (0, LANES - nh * t_new)))
    return pl.pallas_call(
        functools.partial(_bias_sample_kernel, blk=blk, t_new=t_new),
        out_shape=[jax.ShapeDtypeStruct((blk, LANES), F32), jax.ShapeDtypeStruct((LANES, LANES), F32),
                   jax.ShapeDtypeStruct((SUBLANES, LANES), F32)],
        name="bias_sample",
    )(rbx)


def _moba_prompt_kernel(q_ref, k_ref, v_ref, bias_ref, o_ref, kb_ref, vb_ref, *, nb, blk, n_sel):
    scale = HEAD_DIM ** -0.5
    s = nb * blk
    k = k_ref[0]
    kb_ref[:, 0:HEAD_DIM] = k.astype(BF16)
    pos_blk = lax.broadcasted_iota(jnp.int32, (s, LANES), 0) >> int(math.log2(blk))
    kb_ref[:, HEAD_DIM:] = jnp.where(pos_blk == lax.broadcasted_iota(jnp.int32, (s, LANES), 1), 1.0, 0.0).astype(BF16)
    vb_ref[...] = v_ref[0].astype(BF16)
    means = jnp.sum(k.reshape(nb, blk, HEAD_DIM), axis=1) * (1.0 / blk)
    means_s = _split(jnp.concatenate([means, jnp.zeros((LANES - nb, HEAD_DIM), F32)], axis=0))
    bidx = lax.broadcasted_iota(jnp.int32, (nb, blk), 0)
    blocks = range(nb)
    rows = [slice(c * blk, (c + 1) * blk) for c in blocks]
    qx = []
    for c in blocks:
        q = q_ref[0, rows[c], :]
        qc = (q * scale).astype(BF16)
        if c > n_sel:
            sc = jnp.where(bidx < c, _dot3(means_s, _split(q), nt=True)[:nb], NEG)
            rank = jnp.zeros((nb, blk), jnp.int32)
            for i in range(c):
                ri = sc[i:i + 1, :]
                tie = (bidx > i).astype(jnp.int32)
                rank = rank + jnp.where(ri > sc, 1, jnp.where(ri == sc, tie, 0))
            drop = jnp.where(bidx < c, jnp.where(rank < n_sel, 0.0, NEG), 0.0)
            drop = jnp.concatenate([drop, jnp.zeros((LANES - nb, blk), F32)], axis=0).T
            qc = jnp.concatenate([qc, drop.astype(BF16)], axis=1)
        qx.append(qc)
    parts = []
    for c in blocks:
        width = qx[c].shape[1]
        pc = []
        for j in range(c):
            sj = _dot_nt(qx[c], kb_ref[j * blk:(j + 1) * blk, 0:width])
            pc.append(sj + bias_ref[0, 1] if j == c - 1 else sj)
        pc.append(_dot_nt(qx[c], kb_ref[rows[c], 0:width]) + bias_ref[0, 0])
        parts.append(pc)
    m = []
    for c in blocks:
        mm = parts[c][0]
        for p in parts[c][1:]:
            mm = jnp.maximum(mm, p)
        m.append(jnp.max(mm, axis=1, keepdims=True))
    ps = [[jnp.exp(p - m[c]) for p in parts[c]] for c in blocks]
    l = []
    for c in blocks:
        tot = ps[c][0]
        for p in ps[c][1:]:
            tot = tot + p
        l.append(jnp.sum(tot, axis=1, keepdims=True))
    o = [_dot(jnp.concatenate([p.astype(BF16) for p in ps[c]], axis=1), vb_ref[0:(c + 1) * blk, :])
         for c in blocks]
    for c in blocks:
        o_ref[0, rows[c], :] = (o[c] / l[c]).astype(o_ref.dtype)


def _moba_prompt(q, k, v, bias_tiles):
    b, s, w = q.shape
    nh = w // HEAD_DIM
    blk = MOBA_BLOCK
    assert s % blk == 0
    nb = s // blk
    n_sel = min(MOBA_TOP_K, nb - 1)
    col = pl.BlockSpec((1, s, HEAD_DIM), lambda bi, h: (bi, 0, h))
    return pl.pallas_call(
        functools.partial(_moba_prompt_kernel, nb=nb, blk=blk, n_sel=n_sel),
        grid=(b, nh),
        in_specs=[col, col, col,
                  pl.BlockSpec((1, 2, blk, blk), lambda bi, h: (h, 0, 0, 0))],
        out_specs=col,
        out_shape=jax.ShapeDtypeStruct((b, s, w), BF16),
        scratch_shapes=[pltpu.VMEM((s, HEAD_DIM + LANES), BF16), pltpu.VMEM((s, HEAD_DIM), BF16)],
        compiler_params=_params("parallel", "parallel"),
        name="moba_prompt",
    )(q, k, v, bias_tiles)


def _moba_sample_kernel(pt_ref, q_ref, kn_ref, vn_ref, blast_ref, bown_ref, bfar_ref, *rest,
                        n_pages, ppb, n_sel):
    del pt_ref
    kps, vps, o_ref = rest[:n_pages], rest[n_pages:2 * n_pages], rest[2 * n_pages]
    scale = HEAD_DIM ** -0.5
    t_new, width = q_ref.shape
    nh = width // HEAD_DIM
    cur = n_pages // ppb
    blk = ppb * PAGE_SIZE
    rid = lax.broadcasted_iota(jnp.int32, (LANES, width), 0)
    cid = lax.broadcasted_iota(jnp.int32, (LANES, width), 1)
    q_rep = jnp.concatenate([q_ref[...]] * (LANES // t_new), axis=0)
    qt = jnp.where((rid >> int(math.log2(t_new))) == (cid >> int(math.log2(HEAD_DIM))), q_rep, 0.0)
    qt_s = (qt * scale).astype(BF16)

    def page_matrix(ref):
        return jnp.concatenate([ref[0, pl.ds(h, PAGE_SIZE, stride=nh), :] for h in range(nh)], axis=1)

    logits, ksums = [], []
    for j in range(n_pages):
        kj = page_matrix(kps[j])
        logits.append(_dot_nt(kj.astype(BF16), qt_s))
        ksums.append(jnp.sum(kj, axis=0, keepdims=True))
    means = []
    for jb in range(cur):
        tot = ksums[jb * ppb]
        for r in range(1, ppb):
            tot = tot + ksums[jb * ppb + r]
        means.append(tot * (1.0 / blk))
    sc = _dot_nt(jnp.concatenate(means, axis=0), qt, HI)
    bidx = lax.broadcasted_iota(jnp.int32, (cur, LANES), 0)
    rank = jnp.zeros((cur, LANES), jnp.int32)
    for i in range(cur):
        ri = sc[i:i + 1, :]
        tie = (bidx > i).astype(jnp.int32)
        rank = rank + jnp.where(ri > sc, 1, jnp.where(ri == sc, tie, 0))

    parts = []
    for jb in range(cur):
        sb = jnp.concatenate(logits[jb * ppb:(jb + 1) * ppb], axis=0)
        sb = sb + (blast_ref[...] if jb == cur - 1 else bfar_ref[0:1, :])
        parts.append(jnp.where(rank[jb:jb + 1, :] < n_sel, sb, NEG))
    pad = jnp.zeros((LANES - t_new, width), F32)
    kn = jnp.concatenate([kn_ref[...], pad], axis=0)
    vn = jnp.concatenate([vn_ref[...], pad], axis=0)
    parts.append(_dot_nt(kn.astype(BF16), qt_s) + bown_ref[...])
    mm = parts[0]
    for p in parts[1:-1]:
        mm = jnp.maximum(mm, p)
    m = jnp.maximum(jnp.max(mm, axis=0, keepdims=True), jnp.max(parts[-1], axis=0, keepdims=True))
    acc = jnp.zeros((LANES, width), F32)
    lsum = jnp.zeros((LANES, 1), F32)
    for jb in range(cur + 1):
        pt = jnp.exp(parts[jb] - m).T
        lsum = lsum + jnp.sum(pt, axis=1, keepdims=True)
        if jb < cur:
            vb = jnp.concatenate([page_matrix(vps[jb * ppb + r]).astype(BF16) for r in range(ppb)], axis=0)
        else:
            vb = vn.astype(BF16)
        acc = acc + _dot(pt.astype(BF16), vb)
    out = acc / lsum
    for h in range(nh):
        o_ref[:, h * HEAD_DIM:(h + 1) * HEAD_DIM] = out[h * t_new:(h + 1) * t_new, h * HEAD_DIM:(h + 1) * HEAD_DIM]


def _moba_sample(q, k_new, v_new, cache_k, cache_v, page_table, blast, bown, bfar, *, t_new):
    rows, width = q.shape
    db = rows // t_new
    n_pool = cache_k.shape[0]
    n_pages = page_table.shape[1]
    ppb = MOBA_BLOCK // PAGE_SIZE
    assert n_pages % ppb == 0, "new tokens must start a fresh MoBA block"
    assert t_new == SUBLANES
    cur = n_pages // ppb
    n_sel = min(MOBA_TOP_K, cur)
    assert n_sel > 0
    nh = width // HEAD_DIM
    ck = cache_k.reshape(n_pool, PAGE_SIZE * nh, HEAD_DIM)
    cv = cache_v.reshape(n_pool, PAGE_SIZE * nh, HEAD_DIM)
    new = pl.BlockSpec((t_new, width), lambda s, pt: (s, 0))
    const = lambda shape: pl.BlockSpec(shape, lambda s, pt: (0, 0))
    page = lambda j: pl.BlockSpec((1, PAGE_SIZE * nh, HEAD_DIM), lambda s, pt, j=j: (pt[s, j], 0, 0))
    grid_spec = pltpu.PrefetchScalarGridSpec(
        num_scalar_prefetch=1,
        grid=(db,),
        in_specs=[new, new, new, const(blast.shape), const(bown.shape), const(bfar.shape)]
        + [page(j) for j in range(n_pages)] * 2,
        out_specs=new,
    )
    return pl.pallas_call(
        functools.partial(_moba_sample_kernel, n_pages=n_pages, ppb=ppb, n_sel=n_sel),
        grid_spec=grid_spec,
        out_shape=jax.ShapeDtypeStruct((rows, width), F32),
        compiler_params=_params("parallel"),
        name="moba_sample",
    )(page_table, q, k_new, v_new, blast, bown, bfar, *([ck] * n_pages), *([cv] * n_pages))


def _split(x):
    hi = x.astype(BF16)
    return hi, (x - hi.astype(F32)).astype(BF16)


def _dot3(a, b, nt=False):
    lhs = jnp.concatenate([a[0], a[0], a[1]], axis=1)
    if nt:
        return _dot_nt(lhs, jnp.concatenate([b[0], b[1], b[0]], axis=1))
    return _dot(lhs, jnp.concatenate([b[0], b[1], b[0]], axis=0))


def _gdn_kernel(alog_ref, dtb_ref, q_ref, k_ref, v_ref, z_ref, ab_ref, cwq_ref, cwk_ref, cwv_ref,
                pq_ref, pk_ref, pv_ref, nw_ref, s0_ref, o_ref, so_ref, *scratch, tt, chunk, sample, n_heads, hp):
    assert hp == n_heads
    n_chunks = tt // chunk
    ins = [q_ref, k_ref, v_ref]
    cw_refs = [cwq_ref, cwk_ref, cwv_ref]
    prevs = [pq_ref, pk_ref, pv_ref]
    rid8 = lax.broadcasted_iota(jnp.int32, (SUBLANES, HEAD_DIM), 0)
    if sample:
        (shift_scr,) = scratch
    else:
        s_scr, tail_scr = scratch
        i = pl.program_id(2)

        @pl.when(i == 0)
        def _():
            s_scr[...] = s0_ref[0]
            for n in range(3):
                tail_scr[n] = prevs[n][0]

    ri = lax.broadcasted_iota(jnp.int32, (tt, tt), 0)
    ci = lax.broadcasted_iota(jnp.int32, (tt, tt), 1)
    shift_c = int(math.log2(chunk))
    same_block = lambda bits: (ri >> bits) == (ci >> bits)
    same = same_block(shift_c)
    incl = same & (ri >= ci)
    strict = same & (ri > ci)
    incl_t = same & (ri <= ci)
    eye = jnp.where(ri == ci, 1.0, 0.0)
    base_bits = int(math.log2(SUBLANES))
    base = same_block(base_bits)
    level_masks = [same_block(bits + 1) & (ri >> bits != ci >> bits) for bits in range(base_bits, shift_c)]
    sum_mat = jnp.concatenate([jnp.where(incl, 1.0, 0.0), jnp.where(same, 1.0, 0.0)], axis=0).astype(BF16)
    sum_mat3 = jnp.concatenate([sum_mat] * 3, axis=1)
    ab = ab_ref[0]
    heads = range(hp)
    cols = [slice(hh * HEAD_DIM, (hh + 1) * HEAD_DIM) for hh in heads]
    chunks = [slice(c * chunk, (c + 1) * chunk) for c in range(n_chunks)]


    xq, xk, xv = [], [], []
    for hh in heads:
        for n, dst in enumerate((xq, xk, xv)):
            x = ins[n][0, :, cols[hh]]
            w = cw_refs[n][:, cols[hh]]
            acc = None
            for shift in range(GDN_CONV - 1, 0, -1):
                rolled = pltpu.roll(x, shift, axis=0)
                if sample:
                    slot = (hh * 3 + n) * (GDN_CONV - 1) + shift - 1
                    shift_scr[slot] = rolled
                    for r in range(shift):
                        shift_scr[slot, pl.ds(r, n_chunks, stride=chunk), :] = (
                            prevs[n][GDN_CONV - 1 - shift + r, 0, :, cols[hh]])
                    xm = shift_scr[slot]
                else:
                    first = jnp.where(rid8 < shift, pltpu.roll(tail_scr[n, :, cols[hh]], shift, axis=0),
                                      rolled[:SUBLANES])
                    xm = jnp.concatenate([first, rolled[SUBLANES:]], axis=0)
                tap = GDN_CONV - 1 - shift
                term = xm * w[tap:tap + 1, :]
                acc = term if acc is None else acc + term
            acc = acc + x * w[GDN_CONV - 1:GDN_CONV, :]
            dst.append(_silu(acc))
            if not sample:
                tail_scr[n, :, cols[hh]] = x[tt - SUBLANES:]

    g_all = -jnp.exp(alog_ref[...]) * _softplus(ab + dtb_ref[...])
    beta_all = jax.nn.sigmoid(ab)
    g0 = g_all.astype(BF16)
    r1 = g_all - g0.astype(F32)
    g1 = r1.astype(BF16)
    g2 = (r1 - g1.astype(F32)).astype(BF16)
    sums = _dot(sum_mat3, jnp.concatenate([g0, g1, g2], axis=0))
    g_col = [g_all[:, hh:hh + 1] for hh in heads]
    beta = [beta_all[:, n_heads + hh:n_heads + hh + 1] for hh in heads]
    cg = [sums[:tt, hh:hh + 1] for hh in heads]
    gl = [sums[tt:, hh:hh + 1] for hh in heads]
    decay = []
    for hh in heads:
        cg_row = jnp.sum(jnp.where(incl_t, g_col[hh], 0.0), axis=0, keepdims=True)
        decay.append(jnp.exp(jnp.where(incl, cg[hh] - cg_row, NEG)))

    qn = [xq[hh] * lax.rsqrt(jnp.sum(xq[hh] * xq[hh], axis=-1, keepdims=True) + 1e-6) * (HEAD_DIM ** -0.5)
          for hh in heads]
    kn = [xk[hh] * lax.rsqrt(jnp.sum(xk[hh] * xk[hh], axis=-1, keepdims=True) + 1e-6) for hh in heads]
    kn_s = [_split(kn[hh]) for hh in heads]
    kk = [_dot3(kn_s[hh], kn_s[hh], nt=True) for hh in heads]
    lmat = [jnp.where(strict, beta[hh] * decay[hh] * kk[hh], 0.0) for hh in heads]

    xp = [-jnp.where(base, lmat[hh], 0.0) for hh in heads]
    tinv = [eye + xp[hh] for hh in heads]
    xp_s = [_split(xp[hh]) for hh in heads]
    for _ in range(base_bits - 1):
        xp_s = [_split(_dot3(xp_s[hh], xp_s[hh])) for hh in heads]
        tinv = [tinv[hh] + _dot3(_split(tinv[hh]), xp_s[hh]) for hh in heads]
    l_cat = [jnp.concatenate(_split(lmat[hh]), axis=1) for hh in heads] if level_masks else None
    for mask in level_masks:
        t_hi = [tinv[hh].astype(BF16) for hh in heads]
        lt = [_split(_dot(l_cat[hh], jnp.concatenate([t_hi[hh], t_hi[hh]], axis=0))) for hh in heads]
        tinv = [tinv[hh] - jnp.where(mask, _dot(jnp.concatenate([t_hi[hh], t_hi[hh]], axis=1),
                                                jnp.concatenate(lt[hh], axis=0)), 0.0) for hh in heads]

    eg = [jnp.exp(cg[hh]) for hh in heads]
    wu = [_dot3(_split(tinv[hh]),
                _split(jnp.concatenate([kn[hh] * (beta[hh] * eg[hh]), xv[hh] * beta[hh]], axis=1)))
          for hh in heads]
    u0 = [wu[hh][:, HEAD_DIM:] for hh in heads]
    amat = [jnp.where(incl, _dot_nt(qn[hh].astype(BF16), kn_s[hh][0]) * decay[hh], 0.0).astype(BF16)
            for hh in heads]
    wq = [[jnp.concatenate([wu[hh][cs, :HEAD_DIM], (qn[hh] * eg[hh])[cs]], axis=0).astype(BF16) for cs in chunks]
          for hh in heads]
    kd = [kn[hh] * jnp.exp(gl[hh] - cg[hh]) for hh in heads]
    egl = [jnp.exp(gl[hh]) for hh in heads]

    if sample:
        o = []
        for hh in heads:
            rs = [_dot(wq[hh][c], s0_ref[c, hh].astype(BF16)) for c in range(n_chunks)]
            u = u0[hh] - jnp.concatenate([r[:chunk] for r in rs], axis=0)
            o.append(jnp.concatenate([r[chunk:] for r in rs], axis=0) + _dot(amat[hh], u.astype(BF16)))
            for c, cs in enumerate(chunks):
                so_ref[c, hh] = (egl[hh][c * chunk:c * chunk + 1, :] * s0_ref[c, hh]
                                 + _dot_tn(kd[hh][cs], u[cs], HI))
    else:
        kd_t = [kd[hh].T.astype(BF16) for hh in heads]
        s = [s_scr[hh] for hh in heads]
        outs = [[] for _ in heads]
        for c, cs in enumerate(chunks):
            for hh in heads:
                r = _dot(wq[hh][c], s[hh].astype(BF16))
                u = (u0[hh][cs] - r[:chunk]).astype(BF16)
                pieces = [u]
                if c > 0:
                    pieces.insert(0, jnp.zeros((c * chunk, HEAD_DIM), BF16))
                if c < n_chunks - 1:
                    pieces.append(jnp.zeros((tt - (c + 1) * chunk, HEAD_DIM), BF16))
                u_full = jnp.concatenate(pieces, axis=0)
                outs[hh].append(r[chunk:] + _dot(amat[hh][cs, :], u_full))
                s[hh] = egl[hh][c * chunk:c * chunk + 1, :] * s[hh] + _dot(kd_t[hh], u_full)
        for hh in heads:
            s_scr[hh] = s[hh]
            so_ref[0, hh] = s[hh]
        o = [jnp.concatenate(outs[hh], axis=0) for hh in heads]

    for hh in heads:
        z = z_ref[0, :, cols[hh]]
        on = (o[hh] * lax.rsqrt(jnp.mean(o[hh] * o[hh], axis=-1, keepdims=True) + EPS) * nw_ref[...] * _silu(z))
        o_ref[0, :, cols[hh]] = on.astype(o_ref.dtype)


def _gdn(gqkv, z, ab, conv_w, prev, a_log, dt_bias, norm_w, s0, *, sample, tt):
    g, r, w3 = gqkv.shape
    w = w3 // 3
    nh = w // HEAD_DIM
    hp = nh
    assert r % tt == 0
    ng = 1
    wb = hp * HEAD_DIM
    smem = pl.BlockSpec((1, LANES), lambda *_: (0, 0))
    if sample:
        assert r == tt
        chunk = SUBLANES
        seqs = tt // chunk
        grid = (g, ng)
        col = lambda off: pl.BlockSpec((1, tt, wb), lambda gi, h, off=off: (gi, 0, off + h))
        prev_spec = lambda off: pl.BlockSpec((GDN_CONV - 1, 1, seqs, wb), lambda gi, h, off=off: (0, gi, 0, off + h))
        state = pl.BlockSpec((seqs, hp, HEAD_DIM, HEAD_DIM), lambda gi, h: (gi, h, 0, 0))
        cw = lambda off: pl.BlockSpec((GDN_CONV, wb), lambda gi, h, off=off: (0, off + h))
        row = pl.BlockSpec((1, HEAD_DIM), lambda gi, h: (0, 0))
        abs_ = pl.BlockSpec((1, tt, LANES), lambda gi, h: (gi, 0, 0))
        scratch = [pltpu.VMEM((hp * 3 * (GDN_CONV - 1), tt, HEAD_DIM), F32)]
        sem = ("parallel", "parallel")
    else:
        chunk = GDN_CHUNK
        grid = (g, ng, r // tt)
        col = lambda off: pl.BlockSpec((1, tt, wb), lambda gi, h, i, off=off: (gi, i, off + h))
        prev_spec = lambda off: pl.BlockSpec((1, SUBLANES, wb), lambda gi, h, i, off=off: (gi, 0, off + h))
        state = pl.BlockSpec((1, hp, HEAD_DIM, HEAD_DIM), lambda gi, h, i: (gi, h, 0, 0))
        cw = lambda off: pl.BlockSpec((GDN_CONV, wb), lambda gi, h, i, off=off: (0, off + h))
        row = pl.BlockSpec((1, HEAD_DIM), lambda gi, h, i: (0, 0))
        abs_ = pl.BlockSpec((1, tt, LANES), lambda gi, h, i: (gi, i, 0))
        scratch = [pltpu.VMEM((hp, HEAD_DIM, HEAD_DIM), F32), pltpu.VMEM((3, SUBLANES, wb), F32)]
        sem = ("parallel", "parallel", "arbitrary")
    return pl.pallas_call(
        functools.partial(_gdn_kernel, tt=tt, chunk=chunk, sample=sample, n_heads=nh, hp=hp),
        grid=grid,
        in_specs=[smem, smem, col(0), col(ng), col(2 * ng), col(0), abs_, cw(0), cw(ng), cw(2 * ng),
                  prev_spec(0), prev_spec(ng), prev_spec(2 * ng), row, state],
        out_specs=[col(0), state],
        out_shape=[jax.ShapeDtypeStruct((g, r, w), BF16), jax.ShapeDtypeStruct(s0.shape, F32)],
        scratch_shapes=scratch,
        compiler_params=_params(*sem),
        name="gdn_sample" if sample else "gdn_prompt",
    )(a_log, dt_bias, gqkv, gqkv, gqkv, z, ab, conv_w, conv_w, conv_w, prev, prev, prev, norm_w, s0)


def _outproj_kernel(x_ref, att_ref, gdn_ref, wa_ref, wb_ref, o_ref):
    o_ref[...] = (x_ref[...] + _dot(att_ref[...].astype(BF16), wa_ref[...])
                  + _dot(gdn_ref[...].astype(BF16), wb_ref[...]))


def _outproj(x2d, att, gdn, w_out, *, tm):
    t, d = x2d.shape
    wa = att.shape[1]
    assert gdn.shape[1] == wa and w_out.shape[0] == 2 * wa
    return pl.pallas_call(
        _outproj_kernel,
        grid=(t // tm,),
        in_specs=[pl.BlockSpec((tm, d), lambda i: (i, 0)), pl.BlockSpec((tm, wa), lambda i: (i, 0)),
                  pl.BlockSpec((tm, wa), lambda i: (i, 0)), pl.BlockSpec((wa, d), lambda i: (0, 0)),
                  pl.BlockSpec((wa, d), lambda i: (1, 0))],
        out_specs=pl.BlockSpec((tm, d), lambda i: (i, 0)),
        out_shape=jax.ShapeDtypeStruct((t, d), F32),
        compiler_params=_params("parallel"),
        name="outproj",
    )(x2d, att, gdn, w_out, w_out)


def _ffn_up_kernel(x_ref, g_ref, wg_ref, wv_ref, cwg_ref, cwv_ref, cbg_ref, cbv_ref, *rest,
                   tm, tf, sample, tiles_per_seq):
    if sample:
        (sg0_ref, sg1_ref, sv0_ref, sv1_ref, act_ref, og0_ref, og1_ref, ov0_ref, ov1_ref,
         h_ref, buf_g, buf_v) = rest
    else:
        act_ref, ug_ref, uv_ref, h_ref, carry_g, carry_v, buf_g, buf_v = rest
    i = pl.program_id(0)
    f = pl.program_id(1)

    @pl.when(f == 0)
    def _():
        def store(sl, y):
            h_ref[sl, :] = y.astype(BF16)
        _rmsnorm_rows(lambda sl: x_ref[sl, :], g_ref[...], store, tm)

    def sample_conv(w_ref, cw_ref, cb_ref, state_refs, buf_ref, out_refs):
        cw = cw_ref[...]
        n_seq = tm // SUBLANES
        seq_row = lambda r: pl.ds(r, n_seq, stride=SUBLANES)
        u = _dot(h_ref[...], w_ref[...])
        n_col = tf // LANES
        for c in range(n_col):
            lanes = slice(c * LANES, (c + 1) * LANES)
            buf_ref[0, c] = u[:, lanes]
            for r, o_ref in enumerate(out_refs):
                o_ref[:, lanes] = buf_ref[0, c, seq_row(SUBLANES - (FFN_CONV - 1) + r), :]
            for shift in range(1, FFN_CONV):
                buf_ref[shift, c] = pltpu.roll(u[:, lanes], shift, axis=0)
                for r in range(shift):
                    buf_ref[shift, c, seq_row(r), :] = state_refs[FFN_CONV - 1 - shift + r][:, lanes]
        shifted = {shift: jnp.concatenate([buf_ref[shift, c] for c in range(n_col)], axis=1)
                   for shift in range(1, FFN_CONV)}
        return shifted[2] * cw[0:1, :] + shifted[1] * cw[1:2, :] + u * cw[2:3, :] + cb_ref[...]

    if sample:
        yg = sample_conv(wg_ref, cwg_ref, cbg_ref, (sg0_ref, sg1_ref), buf_g, (og0_ref, og1_ref))
        yv = sample_conv(wv_ref, cwv_ref, cbv_ref, (sv0_ref, sv1_ref), buf_v, (ov0_ref, ov1_ref))
        act_ref[...] = (_silu(yg) * yv).astype(BF16)
    else:
        ys = []
        for w_ref, cw_ref, cb_ref, buf_ref, carry_ref, out_ref in ((wg_ref, cwg_ref, cbg_ref, buf_g, carry_g, ug_ref),
                                                                   (wv_ref, cwv_ref, cbv_ref, buf_v, carry_v, uv_ref)):
            buf_ref[0:SUBLANES, :] = jnp.where(i % tiles_per_seq == 0, 0.0, carry_ref[f])
            buf_ref[SUBLANES:, :] = _dot(h_ref[...], w_ref[...])
            last = buf_ref[tm:, :]
            carry_ref[f] = last
            out_ref[0] = last
            cw = cw_ref[...]
            taps = [buf_ref[SUBLANES - shift:SUBLANES - shift + tm, :] for shift in range(FFN_CONV - 1, -1, -1)]
            ys.append(taps[0] * cw[0:1, :] + taps[1] * cw[1:2, :] + taps[2] * cw[2:3, :] + cb_ref[...])
        act_ref[...] = (_silu(ys[0]) * ys[1]).astype(BF16)


def _ffn_up(x2d, g, w_up, conv_w, conv_b, prev, *, tm, seq_rows, sample):
    t, d = x2d.shape
    dff = w_up.shape[1] // 2
    tf = 512
    nf = dff // tf
    assert dff % tf == 0 and t % tm == 0
    gate = lambda shape: pl.BlockSpec(shape, lambda i, f: (0, f))
    val = lambda shape: pl.BlockSpec(shape, lambda i, f: (0, nf + f))
    in_specs = [pl.BlockSpec((tm, d), lambda i, f: (i, 0)), pl.BlockSpec((1, d), lambda i, f: (0, 0)),
                gate((d, tf)), val((d, tf)), gate((FFN_CONV, tf)), val((FFN_CONV, tf)),
                gate((1, tf)), val((1, tf))]
    args = [x2d, g, w_up, w_up, conv_w, conv_w, conv_b, conv_b]
    scratch = [pltpu.VMEM((tm, d), BF16)]
    if sample:
        assert seq_rows == SUBLANES and FFN_CONV == 3
        tiles_per_seq = 0
        n_seq = tm // SUBLANES
        in_specs += [pl.BlockSpec((n_seq, tf), lambda i, f: (i, f))] * 2
        in_specs += [pl.BlockSpec((n_seq, tf), lambda i, f: (i, nf + f))] * 2
        args += [prev[0], prev[1], prev[0], prev[1]]
        u_specs = [pl.BlockSpec((n_seq, tf), lambda i, f: (i, f))] * 4
        u_shapes = [jax.ShapeDtypeStruct((t // SUBLANES, dff), F32)] * 4
        scratch += [pltpu.VMEM((FFN_CONV, tf // LANES, tm, LANES), F32)] * 2
    else:
        assert seq_rows % tm == 0
        tiles_per_seq = seq_rows // tm
        u_specs = [pl.BlockSpec((1, SUBLANES, tf), lambda i, f: (i, 0, f))] * 2
        u_shapes = [jax.ShapeDtypeStruct((t // tm, SUBLANES, dff), F32)] * 2
        scratch += [pltpu.VMEM((nf, SUBLANES, tf), F32)] * 2 + [pltpu.VMEM((SUBLANES + tm, tf), F32)] * 2
    return pl.pallas_call(
        functools.partial(_ffn_up_kernel, tm=tm, tf=tf, sample=sample, tiles_per_seq=tiles_per_seq),
        grid=(t // tm, nf),
        in_specs=in_specs,
        out_specs=[pl.BlockSpec((tm, tf), lambda i, f: (i, f))] + u_specs,
        out_shape=[jax.ShapeDtypeStruct((t, dff), BF16)] + u_shapes,
        scratch_shapes=scratch,
        compiler_params=_params("arbitrary", "arbitrary"),
        name="ffn_up_sample" if sample else "ffn_up_prompt",
    )(*args)


def _ffn_down_kernel(x_ref, act_ref, wd_ref, gf_ref, y_ref, *, tm, tn, final_norm):
    act = act_ref[...]
    for c in range(0, y_ref.shape[1], tn):
        y_ref[:, c:c + tn] = x_ref[:, c:c + tn] + _dot(act, wd_ref[:, c:c + tn])
    if final_norm:
        def store(sl, y):
            y_ref[sl, :] = y
        _rmsnorm_rows(lambda sl: y_ref[sl, :], gf_ref[...], store, tm)


def _ffn_down(x2d, act, w_down, gf, *, tm, final_norm):
    t, d = x2d.shape
    dff = w_down.shape[0]
    tn = 512
    assert d % tn == 0 and t % tm == 0
    resident = lambda shape: pl.BlockSpec(shape, lambda i: (0, 0), pipeline_mode=pl.Buffered(1))
    return pl.pallas_call(
        functools.partial(_ffn_down_kernel, tm=tm, tn=tn, final_norm=final_norm),
        grid=(t // tm,),
        in_specs=[pl.BlockSpec((tm, d), lambda i: (i, 0)), pl.BlockSpec((tm, dff), lambda i: (i, 0)),
                  resident((dff, d)), resident((1, d))],
        out_specs=pl.BlockSpec((tm, d), lambda i: (i, 0)),
        out_shape=jax.ShapeDtypeStruct((t, d), F32),
        compiler_params=_params("parallel"),
        name="ffn_down",
    )(x2d, act, w_down, gf)


def _layer_slice(x, l):
    return x.reshape(x.shape[1:]) if x.shape[0] == 1 else x[l]


def kernel(x_prompt, x_sample, cache_k, cache_v, page_table, state_gdn, state_gdn_conv, state_ffn_conv,
           rms_mix_w, w_in, gdn_conv_w, gdn_a_log, gdn_dt_bias, gdn_norm_w, w_out, rms_ffn_w, w_up,
           ffn_conv_w, ffn_conv_b, w_down, rel_bias, final_norm_w):
    b, s, d = x_prompt.shape
    db, t_new, _ = x_sample.shape
    depth = w_in.shape[0]
    n_att = cache_k.shape[3]
    w_att = n_att * HEAD_DIM
    n_gdn = gdn_a_log.shape[1]
    w_gdn = n_gdn * HEAD_DIM
    dff = w_down.shape[1]
    assert w_att == w_gdn == 1024 and w_in.shape[2] == 3 * w_att + 4 * w_gdn + 2 * n_gdn
    n_main = 3 * w_att + 4 * w_gdn
    n_pool = cache_k.shape[1]
    tt_s = 128
    grp = tt_s // t_new
    tm = 512

    bias_tiles = _bias_prompt(rel_bias, MOBA_BLOCK)
    blast, bown, bfar = _bias_sample(rel_bias, MOBA_BLOCK, t_new)

    xp = x_prompt.reshape(b * s, d)
    xs = x_sample.reshape(db * t_new, d)
    outs_p = [[] for _ in range(5)]
    outs_s = [[] for _ in range(5)]
    for l in range(depth):
        w_in_b = w_in[l].astype(BF16)
        w_main = w_in_b
        w_ab = jnp.pad(w_in_b[:, n_main:], ((0, 0), (0, LANES - 2 * n_gdn)))
        w_out_b = w_out[l].astype(BF16)
        w_up_b = w_up[l].astype(BF16)
        w_down_b = w_down[l].astype(BF16)
        g_mix = rms_mix_w[l].reshape(1, d)
        g_ffn = rms_ffn_w[l].reshape(1, d)
        g_fin = final_norm_w.reshape(1, d)
        norm_w = gdn_norm_w[l].reshape(1, HEAD_DIM)
        a_log_row = jnp.pad(gdn_a_log[l].reshape(1, n_gdn), ((0, 0), (0, LANES - n_gdn)))
        dt_bias_row = jnp.pad(gdn_dt_bias[l].reshape(1, n_gdn), ((0, 0), (0, LANES - n_gdn)))
        conv_b = ffn_conv_b[l].reshape(1, 2 * dff)
        last = l == depth - 1

        widths = (w_att, w_att, w_att, 3 * w_gdn, w_gdn)
        q, k, v, gq, z, ab = _inproj(xp, g_mix, w_main, w_ab, widths, tm=tm // 2, tn=512)
        att = _moba_prompt(q.reshape(b, s, w_att), k.reshape(b, s, w_att), v.reshape(b, s, w_att),
                           bias_tiles)
        gdn, s_new = _gdn(gq.reshape(b, s, 3 * w_gdn), z.reshape(b, s, w_gdn), ab.reshape(b, s, LANES),
                          gdn_conv_w[l], jnp.zeros((b, SUBLANES, 3 * w_gdn), F32), a_log_row, dt_bias_row,
                          norm_w, jnp.zeros((b, n_gdn, HEAD_DIM, HEAD_DIM), F32), sample=False, tt=256)
        x1 = _outproj(xp, att.reshape(b * s, w_att), gdn.reshape(b * s, w_gdn), w_out_b, tm=tm)
        act, ug, uv = _ffn_up(x1, g_ffn, w_up_b, ffn_conv_w[l], conv_b, None, tm=2 * tm, seq_rows=s, sample=False)
        y = _ffn_down(x1, act, w_down_b, g_fin, tm=tm, final_norm=last)
        outs_p[0].append(k.reshape(b, s, n_att, HEAD_DIM))
        outs_p[1].append(v.reshape(b, s, n_att, HEAD_DIM))
        outs_p[2].append(s_new)
        outs_p[3].append(gq.reshape(b, s, 3 * w_gdn)[:, s - (GDN_CONV - 1):])
        tps = s // (2 * tm)
        outs_p[4].append(jnp.concatenate([ug, uv], axis=-1)[tps - 1::tps, SUBLANES - (FFN_CONV - 1):])
        xp = y

        ts = db * t_new
        q, k, v, gq, z, ab = _inproj(xs, g_mix, w_main, w_ab, widths, tm=tm // 2, tn=512)
        att = _moba_sample(q, k, v, cache_k.reshape((depth * n_pool,) + cache_k.shape[2:]),
                           cache_v.reshape((depth * n_pool,) + cache_v.shape[2:]), page_table + l * n_pool,
                           blast, bown, bfar, t_new=t_new)
        prev_g = jnp.moveaxis(state_gdn_conv[l], 1, 0).reshape(GDN_CONV - 1, db // grp, grp, 3 * w_gdn)
        gdn, s_new = _gdn(gq.reshape(db // grp, tt_s, 3 * w_gdn), z.reshape(db // grp, tt_s, w_gdn),
                          ab.reshape(db // grp, tt_s, LANES), gdn_conv_w[l], prev_g, a_log_row, dt_bias_row,
                          norm_w, _layer_slice(state_gdn, l), sample=True, tt=tt_s)
        x1 = _outproj(xs, att, gdn.reshape(ts, w_gdn), w_out_b, tm=tm)
        prev_f = tuple(state_ffn_conv[l, :, r] for r in range(FFN_CONV - 1))
        act, g0, g1, v0, v1 = _ffn_up(x1, g_ffn, w_up_b, ffn_conv_w[l], conv_b, prev_f, tm=2 * tm, seq_rows=t_new,
                                      sample=True)
        y = _ffn_down(x1, act, w_down_b, g_fin, tm=tm, final_norm=last)
        outs_s[0].append(k.reshape(db, t_new, n_att, HEAD_DIM))
        outs_s[1].append(v.reshape(db, t_new, n_att, HEAD_DIM))
        outs_s[2].append(s_new)
        outs_s[3].append(gq.reshape(db, t_new, 3 * w_gdn)[:, t_new - (GDN_CONV - 1):])
        outs_s[4].append(jnp.stack([jnp.concatenate([g0, v0], axis=-1), jnp.concatenate([g1, v1], axis=-1)], axis=1))
        xs = y

    return (xp.reshape(b, s, d), xs.reshape(db, t_new, d),
            *[jnp.stack(o) for o in outs_p], *[jnp.stack(o) for o in outs_s])
```

```python
import functools
import math

import jax
import jax.numpy as jnp
import numpy as np
from jax import lax
from jax.experimental import pallas as pl
from jax.experimental.pallas import tpu as pltpu

F32 = jnp.float32
BF16 = jnp.bfloat16
HI = lax.Precision.HIGHEST

HEAD_DIM = 128
MOBA_BLOCK = 256
MOBA_TOP_K = 3
PAGE_SIZE = 128
GDN_CONV = 4
GDN_CHUNK = 64
FFN_CONV = 3
NUM_BUCKETS = 32
MAX_DISTANCE = 128
EPS = 1e-6
NEG = -1e30

LANES = 128
SUBLANES = 8
VMEM_LIMIT = 56 * 1024 * 1024


def _bucket_thresholds():
    n = np.arange(0, 4 * MAX_DISTANCE)
    max_exact = NUM_BUCKETS // 2
    nf = np.maximum(n, 1).astype(np.float32)
    large = max_exact + (np.log(nf / np.float32(max_exact)) / np.float32(math.log(MAX_DISTANCE / max_exact))
                         * np.float32(NUM_BUCKETS - max_exact)).astype(np.int32)
    large = np.minimum(large, NUM_BUCKETS - 1)
    b = np.where(n < max_exact, n, large)
    return [int(np.argmax(b >= k)) for k in range(1, NUM_BUCKETS)]


BUCKET_THR = _bucket_thresholds()


def _dot(a, b, precision=None):
    return jnp.dot(a, b, preferred_element_type=F32, precision=precision)


def _dot_nt(a, b, precision=None):
    return lax.dot_general(a, b, (((1,), (1,)), ((), ())), preferred_element_type=F32, precision=precision)


def _dot_tn(a, b, precision=None):
    return lax.dot_general(a, b, (((0,), (0,)), ((), ())), preferred_element_type=F32, precision=precision)


def _silu(x):
    return x * jax.nn.sigmoid(x)


def _softplus(x):
    return jnp.maximum(x, 0.0) + jnp.log1p(jnp.exp(-jnp.abs(x)))


def _params(*semantics):
    return pltpu.CompilerParams(dimension_semantics=semantics, vmem_limit_bytes=VMEM_LIMIT)


def _rmsnorm_rows(load, g, store, rows, chunk=256):
    def body(r, carry):
        sl = pl.ds(pl.multiple_of(r * chunk, chunk), chunk)
        x = load(sl)
        ms = jnp.mean(x * x, axis=-1, keepdims=True)
        store(sl, x * lax.rsqrt(ms + EPS) * g)
        return carry
    lax.fori_loop(0, rows // chunk, body, 0)


def _inproj_kernel(x_ref, g_ref, w_ref, wab_ref, *rest, tn, offsets):
    out_refs, ab_ref = rest[:-1], rest[-1]
    x = x_ref[...]
    ms = jnp.mean(x * x, axis=-1, keepdims=True)
    h = (x * lax.rsqrt(ms + EPS) * g_ref[...]).astype(BF16)
    ab_ref[...] = _dot(h, wab_ref[...])
    for off, ref in zip(offsets, out_refs):
        for c in range(0, ref.shape[1], tn):
            ref[:, c:c + tn] = _dot(h, w_ref[:, off + c:off + c + tn])


def _inproj(x2d, g, w_main, w_ab, widths, *, tm, tn):
    t, d = x2d.shape
    assert t % tm == 0 and all(w % tn == 0 for w in widths) and sum(widths) <= w_main.shape[1]
    offsets = [sum(widths[:n]) for n in range(len(widths))]
    resident = lambda shape: pl.BlockSpec(shape, lambda i: (0, 0), pipeline_mode=pl.Buffered(1))
    return pl.pallas_call(
        functools.partial(_inproj_kernel, tn=tn, offsets=tuple(offsets)),
        grid=(t // tm,),
        in_specs=[pl.BlockSpec((tm, d), lambda i: (i, 0)), resident((1, d)),
                  resident(w_main.shape), resident((d, LANES))],
        out_specs=[pl.BlockSpec((tm, w), lambda i: (i, 0)) for w in widths]
        + [pl.BlockSpec((tm, LANES), lambda i: (i, 0))],
        out_shape=[jax.ShapeDtypeStruct((t, w), F32) for w in widths] + [jax.ShapeDtypeStruct((t, LANES), F32)],
        compiler_params=_params("parallel"),
        name="inproj",
    )(x2d, g, w_main, w_ab)


def _bias_chain(dist, value_of_bucket):
    val = value_of_bucket(0)
    for k in range(1, NUM_BUCKETS):
        val = jnp.where(dist >= BUCKET_THR[k - 1], value_of_bucket(k), val)
    return val


def _bias_prompt_kernel(rb_ref, tiles_ref, *, blk):
    h = pl.program_id(0)
    qi = lax.broadcasted_iota(jnp.int32, (blk, blk), 0)
    kj = lax.broadcasted_iota(jnp.int32, (blk, blk), 1)
    far = rb_ref[NUM_BUCKETS - 1, h]
    for delta in range(2):
        dist = delta * blk + qi - kj
        val = _bias_chain(dist, lambda k: rb_ref[k, h]) - far
        if delta == 0:
            val = jnp.where(dist >= 0, val, NEG)
        tiles_ref[0, delta] = val


def _bias_prompt(rel_bias, blk):
    nh = rel_bias.shape[1]
    assert blk + 1 >= BUCKET_THR[-1]
    return pl.pallas_call(
        functools.partial(_bias_prompt_kernel, blk=blk),
        grid=(nh,),
        in_specs=[pl.BlockSpec(memory_space=pltpu.SMEM)],
        out_specs=pl.BlockSpec((1, 2, blk, blk), lambda h: (h, 0, 0, 0)),
        out_shape=jax.ShapeDtypeStruct((nh, 2, blk, blk), F32),
        compiler_params=_params("arbitrary"),
        name="bias_prompt",
    )(rel_bias)


def _bias_sample_kernel(rbx_ref, last_ref, own_ref, far_ref, *, blk, t_new):
    tok = lax.broadcasted_iota(jnp.int32, (blk, LANES), 1) & (t_new - 1)
    pos = lax.broadcasted_iota(jnp.int32, (blk, LANES), 0)
    row = lambda k: rbx_ref[k:k + 1, :]
    last_ref[...] = _bias_chain(blk + tok - pos, row)
    tok = lax.broadcasted_iota(jnp.int32, (LANES, LANES), 1) & (t_new - 1)
    pos = lax.broadcasted_iota(jnp.int32, (LANES, LANES), 0)
    dist = tok - pos
    own_ref[...] = jnp.where((dist >= 0) & (pos < t_new), _bias_chain(dist, row), NEG)
    far_ref[...] = jnp.broadcast_to(row(NUM_BUCKETS - 1), (SUBLANES, LANES))


def _bias_sample(rel_bias, blk, t_new):
    nh = rel_bias.shape[1]
    assert nh * t_new <= LANES and blk >= LANES
    rbx = jnp.pad(jnp.repeat(rel_bias, t_new, axis=1), ((0, 0), (0, LANES - nh * t_new)))
    return pl.pallas_call(
        functools.partial(_bias_sample_kernel, blk=blk, t_new=t_new),
        out_shape=[jax.ShapeDtypeStruct((blk, LANES), F32), jax.ShapeDtypeStruct((LANES, LANES), F32),
                   jax.ShapeDtypeStruct((SUBLANES, LANES), F32)],
        name="bias_sample",
    )(rbx)


def _moba_prompt_kernel(q_ref, k_ref, v_ref, bias_ref, o_ref, kb_ref, vb_ref, *, nb, blk, n_sel):
    scale = HEAD_DIM ** -0.5
    s = nb * blk
    k = k_ref[0]
    kb_ref[:, 0:HEAD_DIM] = k.astype(BF16)
    pos_blk = lax.broadcasted_iota(jnp.int32, (s, LANES), 0) >> int(math.log2(blk))
    kb_ref[:, HEAD_DIM:] = jnp.where(pos_blk == lax.broadcasted_iota(jnp.int32, (s, LANES), 1), 1.0, 0.0).astype(BF16)
    vb_ref[...] = v_ref[0].astype(BF16)
    means = jnp.sum(k.reshape(nb, blk, HEAD_DIM), axis=1) * (1.0 / blk)
    means_s = _split(jnp.concatenate([means, jnp.zeros((LANES - nb, HEAD_DIM), F32)], axis=0))
    bidx = lax.broadcasted_iota(jnp.int32, (nb, blk), 0)
    blocks = range(nb)
    rows = [slice(c * blk, (c + 1) * blk) for c in blocks]
    qx = []
    for c in blocks:
        q = q_ref[0, rows[c], :]
        qc = (q * scale).astype(BF16)
        if c > n_sel:
            sc = jnp.where(bidx < c, _dot3(means_s, _split(q), nt=True)[:nb], NEG)
            rank = jnp.zeros((nb, blk), jnp.int32)
            for i in range(c):
                ri = sc[i:i + 1, :]
                tie = (bidx > i).astype(jnp.int32)
                rank = rank + jnp.where(ri > sc, 1, jnp.where(ri == sc, tie, 0))
            drop = jnp.where(bidx < c, jnp.where(rank < n_sel, 0.0, NEG), 0.0)
            drop = jnp.concatenate([drop, jnp.zeros((LANES - nb, blk), F32)], axis=0).T
            qc = jnp.concatenate([qc, drop.astype(BF16)], axis=1)
        qx.append(qc)
    parts = []
    for c in blocks:
        width = qx[c].shape[1]
        pc = []
        for j in range(c):
            sj = _dot_nt(qx[c], kb_ref[j * blk:(j + 1) * blk, 0:width])
            pc.append(sj + bias_ref[0, 1] if j == c - 1 else sj)
        pc.append(_dot_nt(qx[c], kb_ref[rows[c], 0:width]) + bias_ref[0, 0])
        parts.append(pc)
    m = []
    for c in blocks:
        mm = parts[c][0]
        for p in parts[c][1:]:
            mm = jnp.maximum(mm, p)
        m.append(jnp.max(mm, axis=1, keepdims=True))
    ps = [[jnp.exp(p - m[c]) for p in parts[c]] for c in blocks]
    l = []
    for c in blocks:
        tot = ps[c][0]
        for p in ps[c][1:]:
            tot = tot + p
        l.append(jnp.sum(tot, axis=1, keepdims=True))
    o = [_dot(jnp.concatenate([p.astype(BF16) for p in ps[c]], axis=1), vb_ref[0:(c + 1) * blk, :])
         for c in blocks]
    for c in blocks:
        o_ref[0, rows[c], :] = (o[c] / l[c]).astype(o_ref.dtype)


def _moba_prompt(q, k, v, bias_tiles):
    b, s, w = q.shape
    nh = w // HEAD_DIM
    blk = MOBA_BLOCK
    assert s % blk == 0
    nb = s // blk
    n_sel = min(MOBA_TOP_K, nb - 1)
    col = pl.BlockSpec((1, s, HEAD_DIM), lambda bi, h: (bi, 0, h))
    return pl.pallas_call(
        functools.partial(_moba_prompt_kernel, nb=nb, blk=blk, n_sel=n_sel),
        grid=(b, nh),
        in_specs=[col, col, col,
                  pl.BlockSpec((1, 2, blk, blk), lambda bi, h: (h, 0, 0, 0))],
        out_specs=col,
        out_shape=jax.ShapeDtypeStruct((b, s, w), BF16),
        scratch_shapes=[pltpu.VMEM((s, HEAD_DIM + LANES), BF16), pltpu.VMEM((s, HEAD_DIM), BF16)],
        compiler_params=_params("parallel", "parallel"),
        name="moba_prompt",
    )(q, k, v, bias_tiles)


def _moba_sample_kernel(pt_ref, q_ref, kn_ref, vn_ref, blast_ref, bown_ref, bfar_ref, *rest,
                        n_pages, ppb, n_sel):
    del pt_ref
    kps, vps, o_ref = rest[:n_pages], rest[n_pages:2 * n_pages], rest[2 * n_pages]
    scale = HEAD_DIM ** -0.5
    t_new, width = q_ref.shape
    nh = width // HEAD_DIM
    cur = n_pages // ppb
    blk = ppb * PAGE_SIZE
    rid = lax.broadcasted_iota(jnp.int32, (LANES, width), 0)
    cid = lax.broadcasted_iota(jnp.int32, (LANES, width), 1)
    q_rep = jnp.concatenate([q_ref[...]] * (LANES // t_new), axis=0)
    qt = jnp.where((rid >> int(math.log2(t_new))) == (cid >> int(math.log2(HEAD_DIM))), q_rep, 0.0)
    qt_s = (qt * scale).astype(BF16)

    def page_matrix(ref):
        return jnp.concatenate([ref[0, pl.ds(h, PAGE_SIZE, stride=nh), :] for h in range(nh)], axis=1)

    logits, ksums = [], []
    for j in range(n_pages):
        kj = page_matrix(kps[j])
        logits.append(_dot_nt(kj.astype(BF16), qt_s))
        ksums.append(jnp.sum(kj, axis=0, keepdims=True))
    means = []
    for jb in range(cur):
        tot = ksums[jb * ppb]
        for r in range(1, ppb):
            tot = tot + ksums[jb * ppb + r]
        means.append(tot * (1.0 / blk))
    sc = _dot_nt(jnp.concatenate(means, axis=0), qt, HI)
    bidx = lax.broadcasted_iota(jnp.int32, (cur, LANES), 0)
    rank = jnp.zeros((cur, LANES), jnp.int32)
    for i in range(cur):
        ri = sc[i:i + 1, :]
        tie = (bidx > i).astype(jnp.int32)
        rank = rank + jnp.where(ri > sc, 1, jnp.where(ri == sc, tie, 0))

    parts = []
    for jb in range(cur):
        sb = jnp.concatenate(logits[jb * ppb:(jb + 1) * ppb], axis=0)
        sb = sb + (blast_ref[...] if jb == cur - 1 else bfar_ref[0:1, :])
        parts.append(jnp.where(rank[jb:jb + 1, :] < n_sel, sb, NEG))
    pad = jnp.zeros((LANES - t_new, width), F32)
    kn = jnp.concatenate([kn_ref[...], pad], axis=0)
    vn = jnp.concatenate([vn_ref[...], pad], axis=0)
    parts.append(_dot_nt(kn.astype(BF16), qt_s) + bown_ref[...])
    mm = parts[0]
    for p in parts[1:-1]:
        mm = jnp.maximum(mm, p)
    m = jnp.maximum(jnp.max(mm, axis=0, keepdims=True), jnp.max(parts[-1], axis=0, keepdims=True))
    acc = jnp.zeros((LANES, width), F32)
    lsum = jnp.zeros((LANES, 1), F32)
    for jb in range(cur + 1):
        pt = jnp.exp(parts[jb] - m).T
        lsum = lsum + jnp.sum(pt, axis=1, keepdims=True)
        if jb < cur:
            vb = jnp.concatenate([page_matrix(vps[jb * ppb + r]).astype(BF16) for r in range(ppb)], axis=0)
        else:
            vb = vn.astype(BF16)
        acc = acc + _dot(pt.astype(BF16), vb)
    out = acc / lsum
    for h in range(nh):
        o_ref[:, h * HEAD_DIM:(h + 1) * HEAD_DIM] = out[h * t_new:(h + 1) * t_new, h * HEAD_DIM:(h + 1) * HEAD_DIM]


def _moba_sample(q, k_new, v_new, cache_k, cache_v, page_table, blast, bown, bfar, *, t_new):
    rows, width = q.shape
    db = rows // t_new
    n_pool = cache_k.shape[0]
    n_pages = page_table.shape[1]
    ppb = MOBA_BLOCK // PAGE_SIZE
    assert n_pages % ppb == 0, "new tokens must start a fresh MoBA block"
    assert t_new == SUBLANES
    cur = n_pages // ppb
    n_sel = min(MOBA_TOP_K, cur)
    assert n_sel > 0
    nh = width // HEAD_DIM
    ck = cache_k.reshape(n_pool, PAGE_SIZE * nh, HEAD_DIM)
    cv = cache_v.reshape(n_pool, PAGE_SIZE * nh, HEAD_DIM)
    new = pl.BlockSpec((t_new, width), lambda s, pt: (s, 0))
    const = lambda shape: pl.BlockSpec(shape, lambda s, pt: (0, 0))
    page = lambda j: pl.BlockSpec((1, PAGE_SIZE * nh, HEAD_DIM), lambda s, pt, j=j: (pt[s, j], 0, 0))
    grid_spec = pltpu.PrefetchScalarGridSpec(
        num_scalar_prefetch=1,
        grid=(db,),
        in_specs=[new, new, new, const(blast.shape), const(bown.shape), const(bfar.shape)]
        + [page(j) for j in range(n_pages)] * 2,
        out_specs=new,
    )
    return pl.pallas_call(
        functools.partial(_moba_sample_kernel, n_pages=n_pages, ppb=ppb, n_sel=n_sel),
        grid_spec=grid_spec,
        out_shape=jax.ShapeDtypeStruct((rows, width), F32),
        compiler_params=_params("parallel"),
        name="moba_sample",
    )(page_table, q, k_new, v_new, blast, bown, bfar, *([ck] * n_pages), *([cv] * n_pages))


def _split(x):
    hi = x.astype(BF16)
    return hi, (x - hi.astype(F32)).astype(BF16)


def _dot3(a, b, nt=False):
    lhs = jnp.concatenate([a[0], a[0], a[1]], axis=1)
    if nt:
        return _dot_nt(lhs, jnp.concatenate([b[0], b[1], b[0]], axis=1))
    return _dot(lhs, jnp.concatenate([b[0], b[1], b[0]], axis=0))


def _gdn_kernel(alog_ref, dtb_ref, q_ref, k_ref, v_ref, z_ref, ab_ref, cwq_ref, cwk_ref, cwv_ref,
                pq_ref, pk_ref, pv_ref, nw_ref, s0_ref, o_ref, so_ref, *scratch, tt, chunk, sample, n_heads, hp):
    assert hp == n_heads
    n_chunks = tt // chunk
    ins = [q_ref, k_ref, v_ref]
    cw_refs = [cwq_ref, cwk_ref, cwv_ref]
    prevs = [pq_ref, pk_ref, pv_ref]
    rid8 = lax.broadcasted_iota(jnp.int32, (SUBLANES, HEAD_DIM), 0)
    if sample:
        (shift_scr,) = scratch
    else:
        s_scr, tail_scr = scratch
        i = pl.program_id(2)

        @pl.when(i == 0)
        def _():
            s_scr[...] = s0_ref[0]
            for n in range(3):
                tail_scr[n] = prevs[n][0]

    ri = lax.broadcasted_iota(jnp.int32, (tt, tt), 0)
    ci = lax.broadcasted_iota(jnp.int32, (tt, tt), 1)
    shift_c = int(math.log2(chunk))
    same_block = lambda bits: (ri >> bits) == (ci >> bits)
    same = same_block(shift_c)
    incl = same & (ri >= ci)
    strict = same & (ri > ci)
    incl_t = same & (ri <= ci)
    eye = jnp.where(ri == ci, 1.0, 0.0)
    base_bits = 1
    base = same_block(base_bits)
    level_masks = [same_block(bits + 1) & (ri >> bits != ci >> bits) for bits in range(base_bits, shift_c)]
    sum_mat = jnp.concatenate([jnp.where(incl, 1.0, 0.0), jnp.where(same, 1.0, 0.0)], axis=0).astype(BF16)
    sum_mat3 = jnp.concatenate([sum_mat] * 3, axis=1)
    ab = ab_ref[0]
    heads = range(hp)
    cols = [slice(hh * HEAD_DIM, (hh + 1) * HEAD_DIM) for hh in heads]
    chunks = [slice(c * chunk, (c + 1) * chunk) for c in range(n_chunks)]


    xq, xk, xv = [], [], []
    for hh in heads:
        for n, dst in enumerate((xq, xk, xv)):
            x = ins[n][0, :, cols[hh]]
            w = cw_refs[n][:, cols[hh]]
            acc = None
            for shift in range(GDN_CONV - 1, 0, -1):
                rolled = pltpu.roll(x, shift, axis=0)
                if sample:
                    slot = (hh * 3 + n) * (GDN_CONV - 1) + shift - 1
                    shift_scr[slot] = rolled
                    for r in range(shift):
                        shift_scr[slot, pl.ds(r, n_chunks, stride=chunk), :] = (
                            prevs[n][GDN_CONV - 1 - shift + r, 0, :, cols[hh]])
                    xm = shift_scr[slot]
                else:
                    first = jnp.where(rid8 < shift, pltpu.roll(tail_scr[n, :, cols[hh]], shift, axis=0),
                                      rolled[:SUBLANES])
                    xm = jnp.concatenate([first, rolled[SUBLANES:]], axis=0)
                tap = GDN_CONV - 1 - shift
                term = xm * w[tap:tap + 1, :]
                acc = term if acc is None else acc + term
            acc = acc + x * w[GDN_CONV - 1:GDN_CONV, :]
            dst.append(_silu(acc))
            if not sample:
                tail_scr[n, :, cols[hh]] = x[tt - SUBLANES:]

    g_all = -jnp.exp(alog_ref[...]) * _softplus(ab + dtb_ref[...])
    beta_all = jax.nn.sigmoid(ab)
    g0 = g_all.astype(BF16)
    r1 = g_all - g0.astype(F32)
    g1 = r1.astype(BF16)
    g2 = (r1 - g1.astype(F32)).astype(BF16)
    sums = _dot(sum_mat3, jnp.concatenate([g0, g1, g2], axis=0))
    g_col = [g_all[:, hh:hh + 1] for hh in heads]
    beta = [beta_all[:, n_heads + hh:n_heads + hh + 1] for hh in heads]
    cg = [sums[:tt, hh:hh + 1] for hh in heads]
    gl = [sums[tt:, hh:hh + 1] for hh in heads]
    decay = []
    for hh in heads:
        cg_row = jnp.sum(jnp.where(incl_t, g_col[hh], 0.0), axis=0, keepdims=True)
        decay.append(jnp.exp(jnp.where(incl, cg[hh] - cg_row, NEG)))

    qn = [xq[hh] * lax.rsqrt(jnp.sum(xq[hh] * xq[hh], axis=-1, keepdims=True) + 1e-6) * (HEAD_DIM ** -0.5)
          for hh in heads]
    kn = [xk[hh] * lax.rsqrt(jnp.sum(xk[hh] * xk[hh], axis=-1, keepdims=True) + 1e-6) for hh in heads]
    kn_s = [_split(kn[hh]) for hh in heads]
    kk = [_dot3(kn_s[hh], kn_s[hh], nt=True) for hh in heads]
    lmat = [jnp.where(strict, beta[hh] * decay[hh] * kk[hh], 0.0) for hh in heads]

    tinv = [eye - jnp.where(base, lmat[hh], 0.0) for hh in heads]
    l_cat = [jnp.concatenate(_split(lmat[hh]), axis=1) for hh in heads]
    for mask in level_masks:
        t_hi = [tinv[hh].astype(BF16) for hh in heads]
        lt = [_split(_dot(l_cat[hh], jnp.concatenate([t_hi[hh], t_hi[hh]], axis=0))) for hh in heads]
        tinv = [tinv[hh] - jnp.where(mask, _dot(jnp.concatenate([t_hi[hh], t_hi[hh]], axis=1),
                                                jnp.concatenate(lt[hh], axis=0)), 0.0) for hh in heads]

    eg = [jnp.exp(cg[hh]) for hh in heads]
    wu = [_dot3(_split(tinv[hh]),
                _split(jnp.concatenate([kn[hh] * (beta[hh] * eg[hh]), xv[hh] * beta[hh]], axis=1)))
          for hh in heads]
    u0 = [wu[hh][:, HEAD_DIM:] for hh in heads]
    amat = [(_dot_nt(qn[hh].astype(BF16), kn_s[hh][0]) * decay[hh]).astype(BF16) for hh in heads]
    wq = [[jnp.concatenate([wu[hh][cs, :HEAD_DIM], (qn[hh] * eg[hh])[cs]], axis=0).astype(BF16) for cs in chunks]
          for hh in heads]
    kd = [kn[hh] * jnp.exp(gl[hh] - cg[hh]) for hh in heads]
    egl = [jnp.exp(gl[hh]) for hh in heads]

    if sample:
        o = []
        for hh in heads:
            rs = [_dot(wq[hh][c], s0_ref[c, hh].astype(BF16)) for c in range(n_chunks)]
            u = u0[hh] - jnp.concatenate([r[:chunk] for r in rs], axis=0)
            o.append(jnp.concatenate([r[chunk:] for r in rs], axis=0) + _dot(amat[hh], u.astype(BF16)))
            for c, cs in enumerate(chunks):
                so_ref[c, hh] = (egl[hh][c * chunk:c * chunk + 1, :] * s0_ref[c, hh]
                                 + _dot_tn(kd[hh][cs], u[cs], HI))
    else:
        kd_t = [kd[hh].T.astype(BF16) for hh in heads]
        s = [s_scr[hh] for hh in heads]
        outs = [[] for _ in heads]
        for c, cs in enumerate(chunks):
            for hh in heads:
                r = _dot(wq[hh][c], s[hh].astype(BF16))
                u = (u0[hh][cs] - r[:chunk]).astype(BF16)
                pieces = [u]
                if c > 0:
                    pieces.insert(0, jnp.zeros((c * chunk, HEAD_DIM), BF16))
                if c < n_chunks - 1:
                    pieces.append(jnp.zeros((tt - (c + 1) * chunk, HEAD_DIM), BF16))
                u_full = jnp.concatenate(pieces, axis=0)
                outs[hh].append(r[chunk:] + _dot(amat[hh][cs, :], u_full))
                s[hh] = egl[hh][c * chunk:c * chunk + 1, :] * s[hh] + _dot(kd_t[hh], u_full)
        for hh in heads:
            s_scr[hh] = s[hh]
            so_ref[0, hh] = s[hh]
        o = [jnp.concatenate(outs[hh], axis=0) for hh in heads]

    for hh in heads:
        z = z_ref[0, :, cols[hh]]
        on = (o[hh] * lax.rsqrt(jnp.mean(o[hh] * o[hh], axis=-1, keepdims=True) + EPS) * nw_ref[...] * _silu(z))
        o_ref[0, :, cols[hh]] = on.astype(o_ref.dtype)


def _gdn(gqkv, z, ab, conv_w, prev, a_log, dt_bias, norm_w, s0, *, sample, tt):
    g, r, w3 = gqkv.shape
    w = w3 // 3
    nh = w // HEAD_DIM
    hp = nh
    assert r % tt == 0
    ng = 1
    wb = hp * HEAD_DIM
    smem = pl.BlockSpec((1, LANES), lambda *_: (0, 0))
    if sample:
        assert r == tt
        chunk = SUBLANES
        seqs = tt // chunk
        grid = (g, ng)
        col = lambda off: pl.BlockSpec((1, tt, wb), lambda gi, h, off=off: (gi, 0, off + h))
        prev_spec = lambda off: pl.BlockSpec((GDN_CONV - 1, 1, seqs, wb), lambda gi, h, off=off: (0, gi, 0, off + h))
        state = pl.BlockSpec((seqs, hp, HEAD_DIM, HEAD_DIM), lambda gi, h: (gi, h, 0, 0))
        cw = lambda off: pl.BlockSpec((GDN_CONV, wb), lambda gi, h, off=off: (0, off + h))
        row = pl.BlockSpec((1, HEAD_DIM), lambda gi, h: (0, 0))
        abs_ = pl.BlockSpec((1, tt, LANES), lambda gi, h: (gi, 0, 0))
        scratch = [pltpu.VMEM((hp * 3 * (GDN_CONV - 1), tt, HEAD_DIM), F32)]
        sem = ("parallel", "parallel")
    else:
        chunk = GDN_CHUNK
        grid = (g, ng, r // tt)
        col = lambda off: pl.BlockSpec((1, tt, wb), lambda gi, h, i, off=off: (gi, i, off + h))
        prev_spec = lambda off: pl.BlockSpec((1, SUBLANES, wb), lambda gi, h, i, off=off: (gi, 0, off + h))
        state = pl.BlockSpec((1, hp, HEAD_DIM, HEAD_DIM), lambda gi, h, i: (gi, h, 0, 0))
        cw = lambda off: pl.BlockSpec((GDN_CONV, wb), lambda gi, h, i, off=off: (0, off + h))
        row = pl.BlockSpec((1, HEAD_DIM), lambda gi, h, i: (0, 0))
        abs_ = pl.BlockSpec((1, tt, LANES), lambda gi, h, i: (gi, i, 0))
        scratch = [pltpu.VMEM((hp, HEAD_DIM, HEAD_DIM), F32), pltpu.VMEM((3, SUBLANES, wb), F32)]
        sem = ("parallel", "parallel", "arbitrary")
    return pl.pallas_call(
        functools.partial(_gdn_kernel, tt=tt, chunk=chunk, sample=sample, n_heads=nh, hp=hp),
        grid=grid,
        in_specs=[smem, smem, col(0), col(ng), col(2 * ng), col(0), abs_, cw(0), cw(ng), cw(2 * ng),
                  prev_spec(0), prev_spec(ng), prev_spec(2 * ng), row, state],
        out_specs=[col(0), state],
        out_shape=[jax.ShapeDtypeStruct((g, r, w), BF16), jax.ShapeDtypeStruct(s0.shape, F32)],
        scratch_shapes=scratch,
        compiler_params=_params(*sem),
        name="gdn_sample" if sample else "gdn_prompt",
    )(a_log, dt_bias, gqkv, gqkv, gqkv, z, ab, conv_w, conv_w, conv_w, prev, prev, prev, norm_w, s0)


def _outproj_kernel(x_ref, att_ref, gdn_ref, wa_ref, wb_ref, o_ref):
    o_ref[...] = (x_ref[...] + _dot(att_ref[...].astype(BF16), wa_ref[...])
                  + _dot(gdn_ref[...].astype(BF16), wb_ref[...]))


def _outproj(x2d, att, gdn, w_out, *, tm):
    t, d = x2d.shape
    wa = att.shape[1]
    assert gdn.shape[1] == wa and w_out.shape[0] == 2 * wa
    return pl.pallas_call(
        _outproj_kernel,
        grid=(t // tm,),
        in_specs=[pl.BlockSpec((tm, d), lambda i: (i, 0)), pl.BlockSpec((tm, wa), lambda i: (i, 0)),
                  pl.BlockSpec((tm, wa), lambda i: (i, 0)), pl.BlockSpec((wa, d), lambda i: (0, 0)),
                  pl.BlockSpec((wa, d), lambda i: (1, 0))],
        out_specs=pl.BlockSpec((tm, d), lambda i: (i, 0)),
        out_shape=jax.ShapeDtypeStruct((t, d), F32),
        compiler_params=_params("parallel"),
        name="outproj",
    )(x2d, att, gdn, w_out, w_out)


def _ffn_up_kernel(x_ref, g_ref, wg_ref, wv_ref, cwg_ref, cwv_ref, cbg_ref, cbv_ref, *rest,
                   tm, tf, sample, tiles_per_seq):
    if sample:
        (sg0_ref, sg1_ref, sv0_ref, sv1_ref, act_ref, og0_ref, og1_ref, ov0_ref, ov1_ref,
         h_ref, buf_g, buf_v) = rest
    else:
        act_ref, ug_ref, uv_ref, h_ref, carry_g, carry_v, buf_g, buf_v = rest
    i = pl.program_id(0)
    f = pl.program_id(1)

    @pl.when(f == 0)
    def _():
        def store(sl, y):
            h_ref[sl, :] = y.astype(BF16)
        _rmsnorm_rows(lambda sl: x_ref[sl, :], g_ref[...], store, tm)

    def sample_conv(w_ref, cw_ref, cb_ref, state_refs, buf_ref, out_refs):
        cw = cw_ref[...]
        n_seq = tm // SUBLANES
        seq_row = lambda r: pl.ds(r, n_seq, stride=SUBLANES)
        u = _dot(h_ref[...], w_ref[...])
        n_col = tf // LANES
        for c in range(n_col):
            lanes = slice(c * LANES, (c + 1) * LANES)
            buf_ref[0, c] = u[:, lanes]
            for r, o_ref in enumerate(out_refs):
                o_ref[:, lanes] = buf_ref[0, c, seq_row(SUBLANES - (FFN_CONV - 1) + r), :]
            for shift in range(1, FFN_CONV):
                buf_ref[shift, c] = pltpu.roll(u[:, lanes], shift, axis=0)
                for r in range(shift):
                    buf_ref[shift, c, seq_row(r), :] = state_refs[FFN_CONV - 1 - shift + r][:, lanes]
        shifted = {shift: jnp.concatenate([buf_ref[shift, c] for c in range(n_col)], axis=1)
                   for shift in range(1, FFN_CONV)}
        return shifted[2] * cw[0:1, :] + shifted[1] * cw[1:2, :] + u * cw[2:3, :] + cb_ref[...]

    if sample:
        yg = sample_conv(wg_ref, cwg_ref, cbg_ref, (sg0_ref, sg1_ref), buf_g, (og0_ref, og1_ref))
        yv = sample_conv(wv_ref, cwv_ref, cbv_ref, (sv0_ref, sv1_ref), buf_v, (ov0_ref, ov1_ref))
        act_ref[...] = (_silu(yg) * yv).astype(BF16)
    else:
        ys = []
        for w_ref, cw_ref, cb_ref, buf_ref, carry_ref, out_ref in ((wg_ref, cwg_ref, cbg_ref, buf_g, carry_g, ug_ref),
                                                                   (wv_ref, cwv_ref, cbv_ref, buf_v, carry_v, uv_ref)):
            buf_ref[0:SUBLANES, :] = jnp.where(i % tiles_per_seq == 0, 0.0, carry_ref[f])
            buf_ref[SUBLANES:, :] = _dot(h_ref[...], w_ref[...])
            last = buf_ref[tm:, :]
            carry_ref[f] = last
            out_ref[0] = last
            cw = cw_ref[...]
            taps = [buf_ref[SUBLANES - shift:SUBLANES - shift + tm, :] for shift in range(FFN_CONV - 1, -1, -1)]
            ys.append(taps[0] * cw[0:1, :] + taps[1] * cw[1:2, :] + taps[2] * cw[2:3, :] + cb_ref[...])
        act_ref[...] = (_silu(ys[0]) * ys[1]).astype(BF16)


def _ffn_up(x2d, g, w_up, conv_w, conv_b, prev, *, tm, seq_rows, sample):
    t, d = x2d.shape
    dff = w_up.shape[1] // 2
    tf = 512
    nf = dff // tf
    assert dff % tf == 0 and t % tm == 0
    gate = lambda shape: pl.BlockSpec(shape, lambda i, f: (0, f))
    val = lambda shape: pl.BlockSpec(shape, lambda i, f: (0, nf + f))
    in_specs = [pl.BlockSpec((tm, d), lambda i, f: (i, 0)), pl.BlockSpec((1, d), lambda i, f: (0, 0)),
                gate((d, tf)), val((d, tf)), gate((FFN_CONV, tf)), val((FFN_CONV, tf)),
                gate((1, tf)), val((1, tf))]
    args = [x2d, g, w_up, w_up, conv_w, conv_w, conv_b, conv_b]
    scratch = [pltpu.VMEM((tm, d), BF16)]
    if sample:
        assert seq_rows == SUBLANES and FFN_CONV == 3
        tiles_per_seq = 0
        n_seq = tm // SUBLANES
        in_specs += [pl.BlockSpec((n_seq, tf), lambda i, f: (i, f))] * 2
        in_specs += [pl.BlockSpec((n_seq, tf), lambda i, f: (i, nf + f))] * 2
        args += [prev[0], prev[1], prev[0], prev[1]]
        u_specs = [pl.BlockSpec((n_seq, tf), lambda i, f: (i, f))] * 4
        u_shapes = [jax.ShapeDtypeStruct((t // SUBLANES, dff), F32)] * 4
        scratch += [pltpu.VMEM((FFN_CONV, tf // LANES, tm, LANES), F32)] * 2
    else:
        assert seq_rows % tm == 0
        tiles_per_seq = seq_rows // tm
        u_specs = [pl.BlockSpec((1, SUBLANES, tf), lambda i, f: (i, 0, f))] * 2
        u_shapes = [jax.ShapeDtypeStruct((t // tm, SUBLANES, dff), F32)] * 2
        scratch += [pltpu.VMEM((nf, SUBLANES, tf), F32)] * 2 + [pltpu.VMEM((SUBLANES + tm, tf), F32)] * 2
    return pl.pallas_call(
        functools.partial(_ffn_up_kernel, tm=tm, tf=tf, sample=sample, tiles_per_seq=tiles_per_seq),
        grid=(t // tm, nf),
        in_specs=in_specs,
        out_specs=[pl.BlockSpec((tm, tf), lambda i, f: (i, f))] + u_specs,
        out_shape=[jax.ShapeDtypeStruct((t, dff), BF16)] + u_shapes,
        scratch_shapes=scratch,
        compiler_params=_params("arbitrary", "arbitrary"),
        name="ffn_up_sample" if sample else "ffn_up_prompt",
    )(*args)


def _ffn_down_kernel(x_ref, act_ref, wd_ref, gf_ref, y_ref, *, tm, tn, final_norm):
    act = act_ref[...]
    for c in range(0, y_ref.shape[1], tn):
        y_ref[:, c:c + tn] = x_ref[:, c:c + tn] + _dot(act, wd_ref[:, c:c + tn])
    if final_norm:
        def store(sl, y):
            y_ref[sl, :] = y
        _rmsnorm_rows(lambda sl: y_ref[sl, :], gf_ref[...], store, tm)


def _ffn_down(x2d, act, w_down, gf, *, tm, final_norm):
    t, d = x2d.shape
    dff = w_down.shape[0]
    tn = 512
    assert d % tn == 0 and t % tm == 0
    resident = lambda shape: pl.BlockSpec(shape, lambda i: (0, 0), pipeline_mode=pl.Buffered(1))
    return pl.pallas_call(
        functools.partial(_ffn_down_kernel, tm=tm, tn=tn, final_norm=final_norm),
        grid=(t // tm,),
        in_specs=[pl.BlockSpec((tm, d), lambda i: (i, 0)), pl.BlockSpec((tm, dff), lambda i: (i, 0)),
                  resident((dff, d)), resident((1, d))],
        out_specs=pl.BlockSpec((tm, d), lambda i: (i, 0)),
        out_shape=jax.ShapeDtypeStruct((t, d), F32),
        compiler_params=_params("parallel"),
        name="ffn_down",
    )(x2d, act, w_down, gf)


def _layer_slice(x, l):
    return x.reshape(x.shape[1:]) if x.shape[0] == 1 else x[l]


def kernel(x_prompt, x_sample, cache_k, cache_v, page_table, state_gdn, state_gdn_conv, state_ffn_conv,
           rms_mix_w, w_in, gdn_conv_w, gdn_a_log, gdn_dt_bias, gdn_norm_w, w_out, rms_ffn_w, w_up,
           ffn_conv_w, ffn_conv_b, w_down, rel_bias, final_norm_w):
    b, s, d = x_prompt.shape
    db, t_new, _ = x_sample.shape
    depth = w_in.shape[0]
    n_att = cache_k.shape[3]
    w_att = n_att * HEAD_DIM
    n_gdn = gdn_a_log.shape[1]
    w_gdn = n_gdn * HEAD_DIM
    dff = w_down.shape[1]
    assert w_att == w_gdn == 1024 and w_in.shape[2] == 3 * w_att + 4 * w_gdn + 2 * n_gdn
    n_main = 3 * w_att + 4 * w_gdn
    n_pool = cache_k.shape[1]
    tt_s = 128
    grp = tt_s // t_new
    tm = 512

    bias_tiles = _bias_prompt(rel_bias, MOBA_BLOCK)
    blast, bown, bfar = _bias_sample(rel_bias, MOBA_BLOCK, t_new)

    xp = x_prompt.reshape(b * s, d)
    xs = x_sample.reshape(db * t_new, d)
    outs_p = [[] for _ in range(5)]
    outs_s = [[] for _ in range(5)]
    for l in range(depth):
        w_in_b = w_in[l].astype(BF16)
        w_main = w_in_b
        w_ab = jnp.pad(w_in_b[:, n_main:], ((0, 0), (0, LANES - 2 * n_gdn)))
        w_out_b = w_out[l].astype(BF16)
        w_up_b = w_up[l].astype(BF16)
        w_down_b = w_down[l].astype(BF16)
        g_mix = rms_mix_w[l].reshape(1, d)
        g_ffn = rms_ffn_w[l].reshape(1, d)
        g_fin = final_norm_w.reshape(1, d)
        norm_w = gdn_norm_w[l].reshape(1, HEAD_DIM)
        a_log_row = jnp.pad(gdn_a_log[l].reshape(1, n_gdn), ((0, 0), (0, LANES - n_gdn)))
        dt_bias_row = jnp.pad(gdn_dt_bias[l].reshape(1, n_gdn), ((0, 0), (0, LANES - n_gdn)))
        conv_b = ffn_conv_b[l].reshape(1, 2 * dff)
        last = l == depth - 1

        widths = (w_att, w_att, w_att, 3 * w_gdn, w_gdn)
        q, k, v, gq, z, ab = _inproj(xp, g_mix, w_main, w_ab, widths, tm=tm // 2, tn=512)
        att = _moba_prompt(q.reshape(b, s, w_att), k.reshape(b, s, w_att), v.reshape(b, s, w_att),
                           bias_tiles)
        gdn, s_new = _gdn(gq.reshape(b, s, 3 * w_gdn), z.reshape(b, s, w_gdn), ab.reshape(b, s, LANES),
                          gdn_conv_w[l], jnp.zeros((b, SUBLANES, 3 * w_gdn), F32), a_log_row, dt_bias_row,
                          norm_w, jnp.zeros((b, n_gdn, HEAD_DIM, HEAD_DIM), F32), sample=False, tt=256)
        x1 = _outproj(xp, att.reshape(b * s, w_att), gdn.reshape(b * s, w_gdn), w_out_b, tm=tm)
        act, ug, uv = _ffn_up(x1, g_ffn, w_up_b, ffn_conv_w[l], conv_b, None, tm=2 * tm, seq_rows=s, sample=False)
        y = _ffn_down(x1, act, w_down_b, g_fin, tm=tm, final_norm=last)
        outs_p[0].append(k.reshape(b, s, n_att, HEAD_DIM))
        outs_p[1].append(v.reshape(b, s, n_att, HEAD_DIM))
        outs_p[2].append(s_new)
        outs_p[3].append(gq.reshape(b, s, 3 * w_gdn)[:, s - (GDN_CONV - 1):])
        tps = s // (2 * tm)
        outs_p[4].append(jnp.concatenate([ug, uv], axis=-1)[tps - 1::tps, SUBLANES - (FFN_CONV - 1):])
        xp = y

        ts = db * t_new
        q, k, v, gq, z, ab = _inproj(xs, g_mix, w_main, w_ab, widths, tm=tm // 2, tn=512)
        att = _moba_sample(q, k, v, cache_k.reshape((depth * n_pool,) + cache_k.shape[2:]),
                           cache_v.reshape((depth * n_pool,) + cache_v.shape[2:]), page_table + l * n_pool,
                           blast, bown, bfar, t_new=t_new)
        prev_g = jnp.moveaxis(state_gdn_conv[l], 1, 0).reshape(GDN_CONV - 1, db // grp, grp, 3 * w_gdn)
        gdn, s_new = _gdn(gq.reshape(db // grp, tt_s, 3 * w_gdn), z.reshape(db // grp, tt_s, w_gdn),
                          ab.reshape(db // grp, tt_s, LANES), gdn_conv_w[l], prev_g, a_log_row, dt_bias_row,
                          norm_w, _layer_slice(state_gdn, l), sample=True, tt=tt_s)
        x1 = _outproj(xs, att, gdn.reshape(ts, w_gdn), w_out_b, tm=tm)
        prev_f = tuple(state_ffn_conv[l, :, r] for r in range(FFN_CONV - 1))
        act, g0, g1, v0, v1 = _ffn_up(x1, g_ffn, w_up_b, ffn_conv_w[l], conv_b, prev_f, tm=2 * tm, seq_rows=t_new,
                                      sample=True)
        y = _ffn_down(x1, act, w_down_b, g_fin, tm=tm, final_norm=last)
        outs_s[0].append(k.reshape(db, t_new, n_att, HEAD_DIM))
        outs_s[1].append(v.reshape(db, t_new, n_att, HEAD_DIM))
        outs_s[2].append(s_new)
        outs_s[3].append(gq.reshape(db, t_new, 3 * w_gdn)[:, t_new - (GDN_CONV - 1):])
        outs_s[4].append(jnp.stack([jnp.concatenate([g0, v0], axis=-1), jnp.concatenate([g1, v1], axis=-1)], axis=1))
        xs = y

    return (xp.reshape(b, s, d), xs.reshape(db, t_new, d),
            *[jnp.stack(o) for o in outs_p], *[jnp.stack(o) for o in outs_s])
```

```python
import functools
import math

import jax
import jax.numpy as jnp
import numpy as np
from jax import lax
from jax.experimental import pallas as pl
from jax.experimental.pallas import tpu as pltpu

F32 = jnp.float32
BF16 = jnp.bfloat16
HI = lax.Precision.HIGHEST

HEAD_DIM = 128
MOBA_BLOCK = 256
MOBA_TOP_K = 3
PAGE_SIZE = 128
GDN_CONV = 4
GDN_CHUNK = 64
FFN_CONV = 3
NUM_BUCKETS = 32
MAX_DISTANCE = 128
EPS = 1e-6
NEG = -1e30

LANES = 128
SUBLANES = 8
VMEM_LIMIT = 56 * 1024 * 1024


def _bucket_thresholds():
    n = np.arange(0, 4 * MAX_DISTANCE)
    max_exact = NUM_BUCKETS // 2
    nf = np.maximum(n, 1).astype(np.float32)
    large = max_exact + (np.log(nf / np.float32(max_exact)) / np.float32(math.log(MAX_DISTANCE / max_exact))
                         * np.float32(NUM_BUCKETS - max_exact)).astype(np.int32)
    large = np.minimum(large, NUM_BUCKETS - 1)
    b = np.where(n < max_exact, n, large)
    return [int(np.argmax(b >= k)) for k in range(1, NUM_BUCKETS)]


BUCKET_THR = _bucket_thresholds()


def _dot(a, b, precision=None):
    return jnp.dot(a, b, preferred_element_type=F32, precision=precision)


def _dot_nt(a, b, precision=None):
    return lax.dot_general(a, b, (((1,), (1,)), ((), ())), preferred_element_type=F32, precision=precision)


def _dot_tn(a, b, precision=None):
    return lax.dot_general(a, b, (((0,), (0,)), ((), ())), preferred_element_type=F32, precision=precision)


def _silu(x):
    return x * jax.nn.sigmoid(x)


def _softplus(x):
    return jnp.maximum(x, 0.0) + jnp.log1p(jnp.exp(-jnp.abs(x)))


def _params(*semantics):
    return pltpu.CompilerParams(dimension_semantics=semantics, vmem_limit_bytes=VMEM_LIMIT)


def _rmsnorm_rows(load, g, store, rows, chunk=256):
    def body(r, carry):
        sl = pl.ds(pl.multiple_of(r * chunk, chunk), chunk)
        x = load(sl)
        ms = jnp.mean(x * x, axis=-1, keepdims=True)
        store(sl, x * lax.rsqrt(ms + EPS) * g)
        return carry
    lax.fori_loop(0, rows // chunk, body, 0)


def _inproj_kernel(x_ref, g_ref, w_ref, wab_ref, *rest, tn, offsets):
    out_refs, ab_ref = rest[:-1], rest[-1]
    x = x_ref[...]
    ms = jnp.mean(x * x, axis=-1, keepdims=True)
    h = (x * lax.rsqrt(ms + EPS) * g_ref[...]).astype(BF16)
    ab_ref[...] = _dot(h, wab_ref[...])
    for off, ref in zip(offsets, out_refs):
        for c in range(0, ref.shape[1], tn):
            ref[:, c:c + tn] = _dot(h, w_ref[:, off + c:off + c + tn])


def _inproj(x2d, g, w_main, w_ab, widths, *, tm, tn):
    t, d = x2d.shape
    assert t % tm == 0 and all(w % tn == 0 for w in widths) and sum(widths) <= w_main.shape[1]
    offsets = [sum(widths[:n]) for n in range(len(widths))]
    resident = lambda shape: pl.BlockSpec(shape, lambda i: (0, 0), pipeline_mode=pl.Buffered(1))
    return pl.pallas_call(
        functools.partial(_inproj_kernel, tn=tn, offsets=tuple(offsets)),
        grid=(t // tm,),
        in_specs=[pl.BlockSpec((tm, d), lambda i: (i, 0)), resident((1, d)),
                  resident(w_main.shape), resident((d, LANES))],
        out_specs=[pl.BlockSpec((tm, w), lambda i: (i, 0)) for w in widths]
        + [pl.BlockSpec((tm, LANES), lambda i: (i, 0))],
        out_shape=[jax.ShapeDtypeStruct((t, w), F32) for w in widths] + [jax.ShapeDtypeStruct((t, LANES), F32)],
        compiler_params=_params("parallel"),
        name="inproj",
    )(x2d, g, w_main, w_ab)


def _bias_chain(dist, value_of_bucket):
    val = value_of_bucket(0)
    for k in range(1, NUM_BUCKETS):
        val = jnp.where(dist >= BUCKET_THR[k - 1], value_of_bucket(k), val)
    return val


def _bias_prompt_kernel(rb_ref, tiles_ref, *, blk):
    h = pl.program_id(0)
    qi = lax.broadcasted_iota(jnp.int32, (blk, blk), 0)
    kj = lax.broadcasted_iota(jnp.int32, (blk, blk), 1)
    far = rb_ref[NUM_BUCKETS - 1, h]
    for delta in range(2):
        dist = delta * blk + qi - kj
        val = _bias_chain(dist, lambda k: rb_ref[k, h]) - far
        if delta == 0:
            val = jnp.where(dist >= 0, val, NEG)
        tiles_ref[0, delta] = val


def _bias_prompt(rel_bias, blk):
    nh = rel_bias.shape[1]
    assert blk + 1 >= BUCKET_THR[-1]
    return pl.pallas_call(
        functools.partial(_bias_prompt_kernel, blk=blk),
        grid=(nh,),
        in_specs=[pl.BlockSpec(memory_space=pltpu.SMEM)],
        out_specs=pl.BlockSpec((1, 2, blk, blk), lambda h: (h, 0, 0, 0)),
        out_shape=jax.ShapeDtypeStruct((nh, 2, blk, blk), F32),
        compiler_params=_params("arbitrary"),
        name="bias_prompt",
    )(rel_bias)


def _bias_sample_kernel(rbx_ref, last_ref, own_ref, far_ref, *, blk, t_new):
    tok = lax.broadcasted_iota(jnp.int32, (blk, LANES), 1) & (t_new - 1)
    pos = lax.broadcasted_iota(jnp.int32, (blk, LANES), 0)
    row = lambda k: rbx_ref[k:k + 1, :]
    last_ref[...] = _bias_chain(blk + tok - pos, row)
    tok = lax.broadcasted_iota(jnp.int32, (LANES, LANES), 1) & (t_new - 1)
    pos = lax.broadcasted_iota(jnp.int32, (LANES, LANES), 0)
    dist = tok - pos
    own_ref[...] = jnp.where((dist >= 0) & (pos < t_new), _bias_chain(dist, row), NEG)
    far_ref[...] = jnp.broadcast_to(row(NUM_BUCKETS - 1), (SUBLANES, LANES))


def _bias_sample(rel_bias, blk, t_new):
    nh = rel_bias.shape[1]
    assert nh * t_new <= LANES and blk >= LANES
    rbx = jnp.pad(jnp.repeat(rel_bias, t_new, axis=1), ((0, 0), (0, LANES - nh * t_new)))
    return pl.pallas_call(
        functools.partial(_bias_sample_kernel, blk=blk, t_new=t_new),
        out_shape=[jax.ShapeDtypeStruct((blk, LANES), F32), jax.ShapeDtypeStruct((LANES, LANES), F32),
                   jax.ShapeDtypeStruct((SUBLANES, LANES), F32)],
        name="bias_sample",
    )(rbx)


def _moba_prompt_kernel(q_ref, k_ref, v_ref, bias_ref, o_ref, kb_ref, vb_ref, *, nb, blk, n_sel):
    scale = HEAD_DIM ** -0.5
    s = nb * blk
    k = k_ref[0]
    kb_ref[:, 0:HEAD_DIM] = k.astype(BF16)
    pos_blk = lax.broadcasted_iota(jnp.int32, (s, LANES), 0) >> int(math.log2(blk))
    kb_ref[:, HEAD_DIM:] = jnp.where(pos_blk == lax.broadcasted_iota(jnp.int32, (s, LANES), 1), 1.0, 0.0).astype(BF16)
    vb_ref[...] = v_ref[0].astype(BF16)
    means = jnp.sum(k.reshape(nb, blk, HEAD_DIM), axis=1) * (1.0 / blk)
    means_s = _split(jnp.concatenate([means, jnp.zeros((LANES - nb, HEAD_DIM), F32)], axis=0))
    bidx = lax.broadcasted_iota(jnp.int32, (nb, blk), 0)
    blocks = range(nb)
    rows = [slice(c * blk, (c + 1) * blk) for c in blocks]
    qx = []
    for c in blocks:
        q = q_ref[0, rows[c], :]
        qc = (q * scale).astype(BF16)
        if c > n_sel:
            sc = jnp.where(bidx < c, _dot3(means_s, _split(q), nt=True)[:nb], NEG)
            rank = jnp.zeros((nb, blk), jnp.int32)
            for i in range(c):
                ri = sc[i:i + 1, :]
                tie = (bidx > i).astype(jnp.int32)
                rank = rank + jnp.where(ri > sc, 1, jnp.where(ri == sc, tie, 0))
            drop = jnp.where(bidx < c, jnp.where(rank < n_sel, 0.0, NEG), 0.0)
            drop = jnp.concatenate([drop, jnp.zeros((LANES - nb, blk), F32)], axis=0).T
            qc = jnp.concatenate([qc, drop.astype(BF16)], axis=1)
        qx.append(qc)
    parts = []
    for c in blocks:
        width = qx[c].shape[1]
        pc = []
        for j in range(c):
            sj = _dot_nt(qx[c], kb_ref[j * blk:(j + 1) * blk, 0:width])
            pc.append(sj + bias_ref[0, 1] if j == c - 1 else sj)
        pc.append(_dot_nt(qx[c], kb_ref[rows[c], 0:width]) + bias_ref[0, 0])
        parts.append(pc)
    m = []
    for c in blocks:
        mm = parts[c][0]
        for p in parts[c][1:]:
            mm = jnp.maximum(mm, p)
        m.append(jnp.max(mm, axis=1, keepdims=True))
    ps = [[jnp.exp(p - m[c]) for p in parts[c]] for c in blocks]
    l = []
    for c in blocks:
        tot = ps[c][0]
        for p in ps[c][1:]:
            tot = tot + p
        l.append(jnp.sum(tot, axis=1, keepdims=True))
    o = [_dot(jnp.concatenate([p.astype(BF16) for p in ps[c]], axis=1), vb_ref[0:(c + 1) * blk, :])
         for c in blocks]
    for c in blocks:
        o_ref[0, rows[c], :] = (o[c] / l[c]).astype(o_ref.dtype)


def _moba_prompt(q, k, v, bias_tiles):
    b, s, w = q.shape
    nh = w // HEAD_DIM
    blk = MOBA_BLOCK
    assert s % blk == 0
    nb = s // blk
    n_sel = min(MOBA_TOP_K, nb - 1)
    col = pl.BlockSpec((1, s, HEAD_DIM), lambda bi, h: (bi, 0, h))
    return pl.pallas_call(
        functools.partial(_moba_prompt_kernel, nb=nb, blk=blk, n_sel=n_sel),
        grid=(b, nh),
        in_specs=[col, col, col,
                  pl.BlockSpec((1, 2, blk, blk), lambda bi, h: (h, 0, 0, 0))],
        out_specs=col,
        out_shape=jax.ShapeDtypeStruct((b, s, w), BF16),
        scratch_shapes=[pltpu.VMEM((s, HEAD_DIM + LANES), BF16), pltpu.VMEM((s, HEAD_DIM), BF16)],
        compiler_params=_params("parallel", "parallel"),
        name="moba_prompt",
    )(q, k, v, bias_tiles)


def _moba_sample_kernel(pt_ref, q_ref, kn_ref, vn_ref, blast_ref, bown_ref, bfar_ref, *rest,
                        n_pages, ppb, n_sel):
    del pt_ref
    kps, vps, o_ref = rest[:n_pages], rest[n_pages:2 * n_pages], rest[2 * n_pages]
    scale = HEAD_DIM ** -0.5
    t_new, width = q_ref.shape
    nh = width // HEAD_DIM
    cur = n_pages // ppb
    blk = ppb * PAGE_SIZE
    rid = lax.broadcasted_iota(jnp.int32, (LANES, width), 0)
    cid = lax.broadcasted_iota(jnp.int32, (LANES, width), 1)
    q_rep = jnp.concatenate([q_ref[...]] * (LANES // t_new), axis=0)
    qt = jnp.where((rid >> int(math.log2(t_new))) == (cid >> int(math.log2(HEAD_DIM))), q_rep, 0.0)
    qt_s = (qt * scale).astype(BF16)

    def page_matrix(ref):
        return jnp.concatenate([ref[0, pl.ds(h, PAGE_SIZE, stride=nh), :] for h in range(nh)], axis=1)

    logits, ksums = [], []
    for j in range(n_pages):
        kj = page_matrix(kps[j])
        logits.append(_dot_nt(kj.astype(BF16), qt_s))
        ksums.append(jnp.sum(kj, axis=0, keepdims=True))
    means = []
    for jb in range(cur):
        tot = ksums[jb * ppb]
        for r in range(1, ppb):
            tot = tot + ksums[jb * ppb + r]
        means.append(tot * (1.0 / blk))
    sc = _dot_nt(jnp.concatenate(means, axis=0), qt, HI)
    bidx = lax.broadcasted_iota(jnp.int32, (cur, LANES), 0)
    rank = jnp.zeros((cur, LANES), jnp.int32)
    for i in range(cur):
        ri = sc[i:i + 1, :]
        tie = (bidx > i).astype(jnp.int32)
        rank = rank + jnp.where(ri > sc, 1, jnp.where(ri == sc, tie, 0))

    parts = []
    for jb in range(cur):
        sb = jnp.concatenate(logits[jb * ppb:(jb + 1) * ppb], axis=0)
        sb = sb + (blast_ref[...] if jb == cur - 1 else bfar_ref[0:1, :])
        parts.append(jnp.where(rank[jb:jb + 1, :] < n_sel, sb, NEG))
    pad = jnp.zeros((LANES - t_new, width), F32)
    kn = jnp.concatenate([kn_ref[...], pad], axis=0)
    vn = jnp.concatenate([vn_ref[...], pad], axis=0)
    parts.append(_dot_nt(kn.astype(BF16), qt_s) + bown_ref[...])
    mm = parts[0]
    for p in parts[1:-1]:
        mm = jnp.maximum(mm, p)
    m = jnp.maximum(jnp.max(mm, axis=0, keepdims=True), jnp.max(parts[-1], axis=0, keepdims=True))
    acc = jnp.zeros((LANES, width), F32)
    lsum = jnp.zeros((LANES, 1), F32)
    for jb in range(cur + 1):
        pt = jnp.exp(parts[jb] - m).T
        lsum = lsum + jnp.sum(pt, axis=1, keepdims=True)
        if jb < cur:
            vb = jnp.concatenate([page_matrix(vps[jb * ppb + r]).astype(BF16) for r in range(ppb)], axis=0)
        else:
            vb = vn.astype(BF16)
        acc = acc + _dot(pt.astype(BF16), vb)
    out = acc / lsum
    for h in range(nh):
        o_ref[:, h * HEAD_DIM:(h + 1) * HEAD_DIM] = out[h * t_new:(h + 1) * t_new, h * HEAD_DIM:(h + 1) * HEAD_DIM]


def _moba_sample(q, k_new, v_new, cache_k, cache_v, page_table, blast, bown, bfar, *, t_new):
    rows, width = q.shape
    db = rows // t_new
    n_pool = cache_k.shape[0]
    n_pages = page_table.shape[1]
    ppb = MOBA_BLOCK // PAGE_SIZE
    assert n_pages % ppb == 0, "new tokens must start a fresh MoBA block"
    assert t_new == SUBLANES
    cur = n_pages // ppb
    n_sel = min(MOBA_TOP_K, cur)
    assert n_sel > 0
    nh = width // HEAD_DIM
    ck = cache_k.reshape(n_pool, PAGE_SIZE * nh, HEAD_DIM)
    cv = cache_v.reshape(n_pool, PAGE_SIZE * nh, HEAD_DIM)
    new = pl.BlockSpec((t_new, width), lambda s, pt: (s, 0))
    const = lambda shape: pl.BlockSpec(shape, lambda s, pt: (0, 0))
    page = lambda j: pl.BlockSpec((1, PAGE_SIZE * nh, HEAD_DIM), lambda s, pt, j=j: (pt[s, j], 0, 0))
    grid_spec = pltpu.PrefetchScalarGridSpec(
        num_scalar_prefetch=1,
        grid=(db,),
        in_specs=[new, new, new, const(blast.shape), const(bown.shape), const(bfar.shape)]
        + [page(j) for j in range(n_pages)] * 2,
        out_specs=new,
    )
    return pl.pallas_call(
        functools.partial(_moba_sample_kernel, n_pages=n_pages, ppb=ppb, n_sel=n_sel),
        grid_spec=grid_spec,
        out_shape=jax.ShapeDtypeStruct((rows, width), F32),
        compiler_params=_params("parallel"),
        name="moba_sample",
    )(page_table, q, k_new, v_new, blast, bown, bfar, *([ck] * n_pages), *([cv] * n_pages))


def _split(x):
    hi = x.astype(BF16)
    return hi, (x - hi.astype(F32)).astype(BF16)


def _dot3(a, b, nt=False):
    lhs = jnp.concatenate([a[0], a[0], a[1]], axis=1)
    if nt:
        return _dot_nt(lhs, jnp.concatenate([b[0], b[1], b[0]], axis=1))
    return _dot(lhs, jnp.concatenate([b[0], b[1], b[0]], axis=0))


def _gdn_kernel(alog_ref, dtb_ref, q_ref, k_ref, v_ref, z_ref, ab_ref, cwq_ref, cwk_ref, cwv_ref,
                pq_ref, pk_ref, pv_ref, nw_ref, s0_ref, o_ref, so_ref, *scratch, tt, chunk, sample, n_heads, hp):
    assert hp == n_heads
    n_chunks = tt // chunk
    ins = [q_ref, k_ref, v_ref]
    cw_refs = [cwq_ref, cwk_ref, cwv_ref]
    prevs = [pq_ref, pk_ref, pv_ref]
    rid8 = lax.broadcasted_iota(jnp.int32, (SUBLANES, HEAD_DIM), 0)
    if sample:
        (shift_scr,) = scratch
    else:
        s_scr, tail_scr = scratch
        i = pl.program_id(2)

        @pl.when(i == 0)
        def _():
            s_scr[...] = s0_ref[0]
            for n in range(3):
                tail_scr[n] = prevs[n][0]

    ri = lax.broadcasted_iota(jnp.int32, (tt, tt), 0)
    ci = lax.broadcasted_iota(jnp.int32, (tt, tt), 1)
    shift_c = int(math.log2(chunk))
    same_block = lambda bits: (ri >> bits) == (ci >> bits)
    same = same_block(shift_c)
    incl = same & (ri >= ci)
    strict = same & (ri > ci)
    incl_t = same & (ri <= ci)
    eye = jnp.where(ri == ci, 1.0, 0.0)
    base_bits = 1
    base = same_block(base_bits)
    level_masks = [same_block(bits + 1) & (ri >> bits != ci >> bits) for bits in range(base_bits, shift_c)]
    sum_mat = jnp.concatenate([jnp.where(incl, 1.0, 0.0), jnp.where(same, 1.0, 0.0)], axis=0).astype(BF16)
    sum_mat3 = jnp.concatenate([sum_mat] * 3, axis=1)
    ab = ab_ref[0]
    heads = range(hp)
    cols = [slice(hh * HEAD_DIM, (hh + 1) * HEAD_DIM) for hh in heads]
    chunks = [slice(c * chunk, (c + 1) * chunk) for c in range(n_chunks)]


    xq, xk, xv = [], [], []
    for hh in heads:
        for n, dst in enumerate((xq, xk, xv)):
            x = ins[n][0, :, cols[hh]]
            w = cw_refs[n][:, cols[hh]]
            acc = None
            for shift in range(GDN_CONV - 1, 0, -1):
                rolled = pltpu.roll(x, shift, axis=0)
                if sample:
                    slot = (hh * 3 + n) * (GDN_CONV - 1) + shift - 1
                    shift_scr[slot] = rolled
                    for r in range(shift):
                        shift_scr[slot, pl.ds(r, n_chunks, stride=chunk), :] = (
                            prevs[n][GDN_CONV - 1 - shift + r, 0, :, cols[hh]])
                    xm = shift_scr[slot]
                else:
                    first = jnp.where(rid8 < shift, pltpu.roll(tail_scr[n, :, cols[hh]], shift, axis=0),
                                      rolled[:SUBLANES])
                    xm = jnp.concatenate([first, rolled[SUBLANES:]], axis=0)
                tap = GDN_CONV - 1 - shift
                term = xm * w[tap:tap + 1, :]
                acc = term if acc is None else acc + term
            acc = acc + x * w[GDN_CONV - 1:GDN_CONV, :]
            dst.append(_silu(acc))
            if not sample:
                tail_scr[n, :, cols[hh]] = x[tt - SUBLANES:]

    g_all = -jnp.exp(alog_ref[...]) * _softplus(ab + dtb_ref[...])
    beta_all = jax.nn.sigmoid(ab)
    g0 = g_all.astype(BF16)
    r1 = g_all - g0.astype(F32)
    g1 = r1.astype(BF16)
    g2 = (r1 - g1.astype(F32)).astype(BF16)
    sums = _dot(sum_mat3, jnp.concatenate([g0, g1, g2], axis=0))
    g_col = [g_all[:, hh:hh + 1] for hh in heads]
    beta = [beta_all[:, n_heads + hh:n_heads + hh + 1] for hh in heads]
    cg = [sums[:tt, hh:hh + 1] for hh in heads]
    gl = [sums[tt:, hh:hh + 1] for hh in heads]
    decay = []
    for hh in heads:
        cg_row = jnp.sum(jnp.where(incl_t, g_col[hh], 0.0), axis=0, keepdims=True)
        decay.append(jnp.exp(jnp.where(incl, cg[hh] - cg_row, NEG)))

    qn = [xq[hh] * lax.rsqrt(jnp.sum(xq[hh] * xq[hh], axis=-1, keepdims=True) + 1e-6) * (HEAD_DIM ** -0.5)
          for hh in heads]
    kn = [xk[hh] * lax.rsqrt(jnp.sum(xk[hh] * xk[hh], axis=-1, keepdims=True) + 1e-6) for hh in heads]
    kn_s = [_split(kn[hh]) for hh in heads]
    kk = [_dot3(kn_s[hh], kn_s[hh], nt=True) for hh in heads]
    lmat = [jnp.where(strict, beta[hh] * decay[hh] * kk[hh], 0.0) for hh in heads]

    tinv = [eye - jnp.where(base, lmat[hh], 0.0) for hh in heads]
    l_cat = [jnp.concatenate(_split(lmat[hh]), axis=1) for hh in heads]
    for mask in level_masks:
        t_hi = [tinv[hh].astype(BF16) for hh in heads]
        lt = [_dot(l_cat[hh], jnp.concatenate([t_hi[hh], t_hi[hh]], axis=0)).astype(BF16) for hh in heads]
        tinv = [tinv[hh] - jnp.where(mask, _dot(t_hi[hh], lt[hh]), 0.0) for hh in heads]

    eg = [jnp.exp(cg[hh]) for hh in heads]
    wu = [_dot3(_split(tinv[hh]),
                _split(jnp.concatenate([kn[hh] * (beta[hh] * eg[hh]), xv[hh] * beta[hh]], axis=1)))
          for hh in heads]
    u0 = [wu[hh][:, HEAD_DIM:] for hh in heads]
    amat = [(_dot_nt(qn[hh].astype(BF16), kn_s[hh][0]) * decay[hh]).astype(BF16) for hh in heads]
    wq = [[jnp.concatenate([wu[hh][cs, :HEAD_DIM], (qn[hh] * eg[hh])[cs]], axis=0).astype(BF16) for cs in chunks]
          for hh in heads]
    kd = [kn[hh] * jnp.exp(gl[hh] - cg[hh]) for hh in heads]
    egl = [jnp.exp(gl[hh]) for hh in heads]

    if sample:
        o = []
        for hh in heads:
            rs = [_dot(wq[hh][c], s0_ref[c, hh].astype(BF16)) for c in range(n_chunks)]
            u = u0[hh] - jnp.concatenate([r[:chunk] for r in rs], axis=0)
            o.append(jnp.concatenate([r[chunk:] for r in rs], axis=0) + _dot(amat[hh], u.astype(BF16)))
            for c, cs in enumerate(chunks):
                so_ref[c, hh] = (egl[hh][c * chunk:c * chunk + 1, :] * s0_ref[c, hh]
                                 + _dot_tn(kd[hh][cs], u[cs], HI))
    else:
        kd_t = [kd[hh].T.astype(BF16) for hh in heads]
        s = [s_scr[hh] for hh in heads]
        outs = [[] for _ in heads]
        for c, cs in enumerate(chunks):
            for hh in heads:
                r = _dot(wq[hh][c], s[hh].astype(BF16))
                u = (u0[hh][cs] - r[:chunk]).astype(BF16)
                pieces = [u]
                if c > 0:
                    pieces.insert(0, jnp.zeros((c * chunk, HEAD_DIM), BF16))
                if c < n_chunks - 1:
                    pieces.append(jnp.zeros((tt - (c + 1) * chunk, HEAD_DIM), BF16))
                u_full = jnp.concatenate(pieces, axis=0)
                outs[hh].append(r[chunk:] + _dot(amat[hh][cs, :], u_full))
                s[hh] = egl[hh][c * chunk:c * chunk + 1, :] * s[hh] + _dot(kd_t[hh], u_full)
        for hh in heads:
            s_scr[hh] = s[hh]
            so_ref[0, hh] = s[hh]
        o = [jnp.concatenate(outs[hh], axis=0) for hh in heads]

    for hh in heads:
        z = z_ref[0, :, cols[hh]]
        on = (o[hh] * lax.rsqrt(jnp.mean(o[hh] * o[hh], axis=-1, keepdims=True) + EPS) * nw_ref[...] * _silu(z))
        o_ref[0, :, cols[hh]] = on.astype(o_ref.dtype)


def _gdn(gqkv, z, ab, conv_w, prev, a_log, dt_bias, norm_w, s0, *, sample, tt):
    g, r, w3 = gqkv.shape
    w = w3 // 3
    nh = w // HEAD_DIM
    hp = nh
    assert r % tt == 0
    ng = 1
    wb = hp * HEAD_DIM
    smem = pl.BlockSpec((1, LANES), lambda *_: (0, 0))
    if sample:
        assert r == tt
        chunk = SUBLANES
        seqs = tt // chunk
        grid = (g, ng)
        col = lambda off: pl.BlockSpec((1, tt, wb), lambda gi, h, off=off: (gi, 0, off + h))
        prev_spec = lambda off: pl.BlockSpec((GDN_CONV - 1, 1, seqs, wb), lambda gi, h, off=off: (0, gi, 0, off + h))
        state = pl.BlockSpec((seqs, hp, HEAD_DIM, HEAD_DIM), lambda gi, h: (gi, h, 0, 0))
        cw = lambda off: pl.BlockSpec((GDN_CONV, wb), lambda gi, h, off=off: (0, off + h))
        row = pl.BlockSpec((1, HEAD_DIM), lambda gi, h: (0, 0))
        abs_ = pl.BlockSpec((1, tt, LANES), lambda gi, h: (gi, 0, 0))
        scratch = [pltpu.VMEM((hp * 3 * (GDN_CONV - 1), tt, HEAD_DIM), F32)]
        sem = ("parallel", "parallel")
    else:
        chunk = GDN_CHUNK
        grid = (g, ng, r // tt)
        col = lambda off: pl.BlockSpec((1, tt, wb), lambda gi, h, i, off=off: (gi, i, off + h))
        prev_spec = lambda off: pl.BlockSpec((1, SUBLANES, wb), lambda gi, h, i, off=off: (gi, 0, off + h))
        state = pl.BlockSpec((1, hp, HEAD_DIM, HEAD_DIM), lambda gi, h, i: (gi, h, 0, 0))
        cw = lambda off: pl.BlockSpec((GDN_CONV, wb), lambda gi, h, i, off=off: (0, off + h))
        row = pl.BlockSpec((1, HEAD_DIM), lambda gi, h, i: (0, 0))
        abs_ = pl.BlockSpec((1, tt, LANES), lambda gi, h, i: (gi, i, 0))
        scratch = [pltpu.VMEM((hp, HEAD_DIM, HEAD_DIM), F32), pltpu.VMEM((3, SUBLANES, wb), F32)]
        sem = ("parallel", "parallel", "arbitrary")
    return pl.pallas_call(
        functools.partial(_gdn_kernel, tt=tt, chunk=chunk, sample=sample, n_heads=nh, hp=hp),
        grid=grid,
        in_specs=[smem, smem, col(0), col(ng), col(2 * ng), col(0), abs_, cw(0), cw(ng), cw(2 * ng),
                  prev_spec(0), prev_spec(ng), prev_spec(2 * ng), row, state],
        out_specs=[col(0), state],
        out_shape=[jax.ShapeDtypeStruct((g, r, w), BF16), jax.ShapeDtypeStruct(s0.shape, F32)],
        scratch_shapes=scratch,
        compiler_params=_params(*sem),
        name="gdn_sample" if sample else "gdn_prompt",
    )(a_log, dt_bias, gqkv, gqkv, gqkv, z, ab, conv_w, conv_w, conv_w, prev, prev, prev, norm_w, s0)


def _outproj_kernel(x_ref, att_ref, gdn_ref, wa_ref, wb_ref, o_ref):
    o_ref[...] = (x_ref[...] + _dot(att_ref[...].astype(BF16), wa_ref[...])
                  + _dot(gdn_ref[...].astype(BF16), wb_ref[...]))


def _outproj(x2d, att, gdn, w_out, *, tm):
    t, d = x2d.shape
    wa = att.shape[1]
    assert gdn.shape[1] == wa and w_out.shape[0] == 2 * wa
    return pl.pallas_call(
        _outproj_kernel,
        grid=(t // tm,),
        in_specs=[pl.BlockSpec((tm, d), lambda i: (i, 0)), pl.BlockSpec((tm, wa), lambda i: (i, 0)),
                  pl.BlockSpec((tm, wa), lambda i: (i, 0)), pl.BlockSpec((wa, d), lambda i: (0, 0)),
                  pl.BlockSpec((wa, d), lambda i: (1, 0))],
        out_specs=pl.BlockSpec((tm, d), lambda i: (i, 0)),
        out_shape=jax.ShapeDtypeStruct((t, d), F32),
        compiler_params=_params("parallel"),
        name="outproj",
    )(x2d, att, gdn, w_out, w_out)


def _ffn_up_kernel(x_ref, g_ref, wg_ref, wv_ref, cwg_ref, cwv_ref, cbg_ref, cbv_ref, *rest,
                   tm, tf, sample, tiles_per_seq):
    if sample:
        (sg0_ref, sg1_ref, sv0_ref, sv1_ref, act_ref, og0_ref, og1_ref, ov0_ref, ov1_ref,
         h_ref, buf_g, buf_v) = rest
    else:
        act_ref, ug_ref, uv_ref, h_ref, carry_g, carry_v, buf_g, buf_v = rest
    i = pl.program_id(0)
    f = pl.program_id(1)

    @pl.when(f == 0)
    def _():
        def store(sl, y):
            h_ref[sl, :] = y.astype(BF16)
        _rmsnorm_rows(lambda sl: x_ref[sl, :], g_ref[...], store, tm)

    def sample_conv(w_ref, cw_ref, cb_ref, state_refs, buf_ref, out_refs):
        cw = cw_ref[...]
        n_seq = tm // SUBLANES
        seq_row = lambda r: pl.ds(r, n_seq, stride=SUBLANES)
        u = _dot(h_ref[...], w_ref[...])
        n_col = tf // LANES
        for c in range(n_col):
            lanes = slice(c * LANES, (c + 1) * LANES)
            buf_ref[0, c] = u[:, lanes]
            for r, o_ref in enumerate(out_refs):
                o_ref[:, lanes] = buf_ref[0, c, seq_row(SUBLANES - (FFN_CONV - 1) + r), :]
            for shift in range(1, FFN_CONV):
                buf_ref[shift, c] = pltpu.roll(u[:, lanes], shift, axis=0)
                for r in range(shift):
                    buf_ref[shift, c, seq_row(r), :] = state_refs[FFN_CONV - 1 - shift + r][:, lanes]
        shifted = {shift: jnp.concatenate([buf_ref[shift, c] for c in range(n_col)], axis=1)
                   for shift in range(1, FFN_CONV)}
        return shifted[2] * cw[0:1, :] + shifted[1] * cw[1:2, :] + u * cw[2:3, :] + cb_ref[...]

    if sample:
        yg = sample_conv(wg_ref, cwg_ref, cbg_ref, (sg0_ref, sg1_ref), buf_g, (og0_ref, og1_ref))
        yv = sample_conv(wv_ref, cwv_ref, cbv_ref, (sv0_ref, sv1_ref), buf_v, (ov0_ref, ov1_ref))
        act_ref[...] = (_silu(yg) * yv).astype(BF16)
    else:
        ys = []
        for w_ref, cw_ref, cb_ref, buf_ref, carry_ref, out_ref in ((wg_ref, cwg_ref, cbg_ref, buf_g, carry_g, ug_ref),
                                                                   (wv_ref, cwv_ref, cbv_ref, buf_v, carry_v, uv_ref)):
            buf_ref[0:SUBLANES, :] = jnp.where(i % tiles_per_seq == 0, 0.0, carry_ref[f])
            buf_ref[SUBLANES:, :] = _dot(h_ref[...], w_ref[...])
            last = buf_ref[tm:, :]
            carry_ref[f] = last
            out_ref[0] = last
            cw = cw_ref[...]
            taps = [buf_ref[SUBLANES - shift:SUBLANES - shift + tm, :] for shift in range(FFN_CONV - 1, -1, -1)]
            ys.append(taps[0] * cw[0:1, :] + taps[1] * cw[1:2, :] + taps[2] * cw[2:3, :] + cb_ref[...])
        act_ref[...] = (_silu(ys[0]) * ys[1]).astype(BF16)


def _ffn_up(x2d, g, w_up, conv_w, conv_b, prev, *, tm, seq_rows, sample):
    t, d = x2d.shape
    dff = w_up.shape[1] // 2
    tf = 512
    nf = dff // tf
    assert dff % tf == 0 and t % tm == 0
    gate = lambda shape: pl.BlockSpec(shape, lambda i, f: (0, f))
    val = lambda shape: pl.BlockSpec(shape, lambda i, f: (0, nf + f))
    in_specs = [pl.BlockSpec((tm, d), lambda i, f: (i, 0)), pl.BlockSpec((1, d), lambda i, f: (0, 0)),
                gate((d, tf)), val((d, tf)), gate((FFN_CONV, tf)), val((FFN_CONV, tf)),
                gate((1, tf)), val((1, tf))]
    args = [x2d, g, w_up, w_up, conv_w, conv_w, conv_b, conv_b]
    scratch = [pltpu.VMEM((tm, d), BF16)]
    if sample:
        assert seq_rows == SUBLANES and FFN_CONV == 3
        tiles_per_seq = 0
        n_seq = tm // SUBLANES
        in_specs += [pl.BlockSpec((n_seq, tf), lambda i, f: (i, f))] * 2
        in_specs += [pl.BlockSpec((n_seq, tf), lambda i, f: (i, nf + f))] * 2
        args += [prev[0], prev[1], prev[0], prev[1]]
        u_specs = [pl.BlockSpec((n_seq, tf), lambda i, f: (i, f))] * 4
        u_shapes = [jax.ShapeDtypeStruct((t // SUBLANES, dff), F32)] * 4
        scratch += [pltpu.VMEM((FFN_CONV, tf // LANES, tm, LANES), F32)] * 2
    else:
        assert seq_rows % tm == 0
        tiles_per_seq = seq_rows // tm
        u_specs = [pl.BlockSpec((1, SUBLANES, tf), lambda i, f: (i, 0, f))] * 2
        u_shapes = [jax.ShapeDtypeStruct((t // tm, SUBLANES, dff), F32)] * 2
        scratch += [pltpu.VMEM((nf, SUBLANES, tf), F32)] * 2 + [pltpu.VMEM((SUBLANES + tm, tf), F32)] * 2
    return pl.pallas_call(
        functools.partial(_ffn_up_kernel, tm=tm, tf=tf, sample=sample, tiles_per_seq=tiles_per_seq),
        grid=(t // tm, nf),
        in_specs=in_specs,
        out_specs=[pl.BlockSpec((tm, tf), lambda i, f: (i, f))] + u_specs,
        out_shape=[jax.ShapeDtypeStruct((t, dff), BF16)] + u_shapes,
        scratch_shapes=scratch,
        compiler_params=_params("arbitrary", "arbitrary"),
        name="ffn_up_sample" if sample else "ffn_up_prompt",
    )(*args)


def _ffn_down_kernel(x_ref, act_ref, wd_ref, gf_ref, y_ref, *, tm, tn, final_norm):
    act = act_ref[...]
    for c in range(0, y_ref.shape[1], tn):
        y_ref[:, c:c + tn] = x_ref[:, c:c + tn] + _dot(act, wd_ref[:, c:c + tn])
    if final_norm:
        def store(sl, y):
            y_ref[sl, :] = y
        _rmsnorm_rows(lambda sl: y_ref[sl, :], gf_ref[...], store, tm)


def _ffn_down(x2d, act, w_down, gf, *, tm, final_norm):
    t, d = x2d.shape
    dff = w_down.shape[0]
    tn = 512
    assert d % tn == 0 and t % tm == 0
    resident = lambda shape: pl.BlockSpec(shape, lambda i: (0, 0), pipeline_mode=pl.Buffered(1))
    return pl.pallas_call(
        functools.partial(_ffn_down_kernel, tm=tm, tn=tn, final_norm=final_norm),
        grid=(t // tm,),
        in_specs=[pl.BlockSpec((tm, d), lambda i: (i, 0)), pl.BlockSpec((tm, dff), lambda i: (i, 0)),
                  resident((dff, d)), resident((1, d))],
        out_specs=pl.BlockSpec((tm, d), lambda i: (i, 0)),
        out_shape=jax.ShapeDtypeStruct((t, d), F32),
        compiler_params=_params("parallel"),
        name="ffn_down",
    )(x2d, act, w_down, gf)


def _layer_slice(x, l):
    return x.reshape(x.shape[1:]) if x.shape[0] == 1 else x[l]


def kernel(x_prompt, x_sample, cache_k, cache_v, page_table, state_gdn, state_gdn_conv, state_ffn_conv,
           rms_mix_w, w_in, gdn_conv_w, gdn_a_log, gdn_dt_bias, gdn_norm_w, w_out, rms_ffn_w, w_up,
           ffn_conv_w, ffn_conv_b, w_down, rel_bias, final_norm_w):
    b, s, d = x_prompt.shape
    db, t_new, _ = x_sample.shape
    depth = w_in.shape[0]
    n_att = cache_k.shape[3]
    w_att = n_att * HEAD_DIM
    n_gdn = gdn_a_log.shape[1]
    w_gdn = n_gdn * HEAD_DIM
    dff = w_down.shape[1]
    assert w_att == w_gdn == 1024 and w_in.shape[2] == 3 * w_att + 4 * w_gdn + 2 * n_gdn
    n_main = 3 * w_att + 4 * w_gdn
    n_pool = cache_k.shape[1]
    tt_s = 128
    grp = tt_s // t_new
    tm = 512

    bias_tiles = _bias_prompt(rel_bias, MOBA_BLOCK)
    blast, bown, bfar = _bias_sample(rel_bias, MOBA_BLOCK, t_new)

    xp = x_prompt.reshape(b * s, d)
    xs = x_sample.reshape(db * t_new, d)
    outs_p = [[] for _ in range(5)]
    outs_s = [[] for _ in range(5)]
    for l in range(depth):
        w_in_b = w_in[l].astype(BF16)
        w_main = w_in_b
        w_ab = jnp.pad(w_in_b[:, n_main:], ((0, 0), (0, LANES - 2 * n_gdn)))
        w_out_b = w_out[l].astype(BF16)
        w_up_b = w_up[l].astype(BF16)
        w_down_b = w_down[l].astype(BF16)
        g_mix = rms_mix_w[l].reshape(1, d)
        g_ffn = rms_ffn_w[l].reshape(1, d)
        g_fin = final_norm_w.reshape(1, d)
        norm_w = gdn_norm_w[l].reshape(1, HEAD_DIM)
        a_log_row = jnp.pad(gdn_a_log[l].reshape(1, n_gdn), ((0, 0), (0, LANES - n_gdn)))
        dt_bias_row = jnp.pad(gdn_dt_bias[l].reshape(1, n_gdn), ((0, 0), (0, LANES - n_gdn)))
        conv_b = ffn_conv_b[l].reshape(1, 2 * dff)
        last = l == depth - 1

        widths = (w_att, w_att, w_att, 3 * w_gdn, w_gdn)
        q, k, v, gq, z, ab = _inproj(xp, g_mix, w_main, w_ab, widths, tm=tm // 2, tn=512)
        att = _moba_prompt(q.reshape(b, s, w_att), k.reshape(b, s, w_att), v.reshape(b, s, w_att),
                           bias_tiles)
        gdn, s_new = _gdn(gq.reshape(b, s, 3 * w_gdn), z.reshape(b, s, w_gdn), ab.reshape(b, s, LANES),
                          gdn_conv_w[l], jnp.zeros((b, SUBLANES, 3 * w_gdn), F32), a_log_row, dt_bias_row,
                          norm_w, jnp.zeros((b, n_gdn, HEAD_DIM, HEAD_DIM), F32), sample=False, tt=256)
        x1 = _outproj(xp, att.reshape(b * s, w_att), gdn.reshape(b * s, w_gdn), w_out_b, tm=tm)
        act, ug, uv = _ffn_up(x1, g_ffn, w_up_b, ffn_conv_w[l], conv_b, None, tm=2 * tm, seq_rows=s, sample=False)
        y = _ffn_down(x1, act, w_down_b, g_fin, tm=tm, final_norm=last)
        outs_p[0].append(k.reshape(b, s, n_att, HEAD_DIM))
        outs_p[1].append(v.reshape(b, s, n_att, HEAD_DIM))
        outs_p[2].append(s_new)
        outs_p[3].append(gq.reshape(b, s, 3 * w_gdn)[:, s - (GDN_CONV - 1):])
        tps = s // (2 * tm)
        outs_p[4].append(jnp.concatenate([ug, uv], axis=-1)[tps - 1::tps, SUBLANES - (FFN_CONV - 1):])
        xp = y

        ts = db * t_new
        q, k, v, gq, z, ab = _inproj(xs, g_mix, w_main, w_ab, widths, tm=tm // 2, tn=512)
        att = _moba_sample(q, k, v, cache_k.reshape((depth * n_pool,) + cache_k.shape[2:]),
                           cache_v.reshape((depth * n_pool,) + cache_v.shape[2:]), page_table + l * n_pool,
                           blast, bown, bfar, t_new=t_new)
        prev_g = jnp.moveaxis(state_gdn_conv[l], 1, 0).reshape(GDN_CONV - 1, db // grp, grp, 3 * w_gdn)
        gdn, s_new = _gdn(gq.reshape(db // grp, tt_s, 3 * w_gdn), z.reshape(db // grp, tt_s, w_gdn),
                          ab.reshape(db // grp, tt_s, LANES), gdn_conv_w[l], prev_g, a_log_row, dt_bias_row,
                          norm_w, _layer_slice(state_gdn, l), sample=True, tt=tt_s)
        x1 = _outproj(xs, att, gdn.reshape(ts, w_gdn), w_out_b, tm=tm)
        prev_f = tuple(state_ffn_conv[l, :, r] for r in range(FFN_CONV - 1))
        act, g0, g1, v0, v1 = _ffn_up(x1, g_ffn, w_up_b, ffn_conv_w[l], conv_b, prev_f, tm=2 * tm, seq_rows=t_new,
                                      sample=True)
        y = _ffn_down(x1, act, w_down_b, g_fin, tm=tm, final_norm=last)
        outs_s[0].append(k.reshape(db, t_new, n_att, HEAD_DIM))
        outs_s[1].append(v.reshape(db, t_new, n_att, HEAD_DIM))
        outs_s[2].append(s_new)
        outs_s[3].append(gq.reshape(db, t_new, 3 * w_gdn)[:, t_new - (GDN_CONV - 1):])
        outs_s[4].append(jnp.stack([jnp.concatenate([g0, v0], axis=-1), jnp.concatenate([g1, v1], axis=-1)], axis=1))
        xs = y

    return (xp.reshape(b, s, d), xs.reshape(db, t_new, d),
            *[jnp.stack(o) for o in outs_p], *[jnp.stack(o) for o in outs_s])
```
